```python
import jax, jax.numpy as jnp
from jax import lax
import numpy as np

D_MODEL = 2048
BATCH = 4
SEQ = 4096
DEPTH = 4

GRID_W = 64
CTX_LEN = 256
N_MOD = 9
D_FF = 5632
EPS = 1e-6
A_HEADS = 8
A_DK = 128
A_DV = 128
A_WK = A_HEADS * A_DK
A_WV = A_HEADS * A_DV
HGRN_CHUNK = 16
B_GROUPS = 4
B_GROUP_CH = 128
B_WIDTH = B_GROUPS * B_GROUP_CH
B_CHUNK = 128
C_GROUPS = 4
C_GROUP_CH = 128
C_WIDTH = C_GROUPS * C_GROUP_CH
C_KSIZE = 3
IN_SPLITS = (A_WK, A_WV, A_WK, A_WK, A_WV, B_WIDTH, B_WIDTH, C_WIDTH, C_WIDTH, C_WIDTH, D_MODEL, D_MODEL, D_MODEL)
IN_WIDTH = sum(IN_SPLITS)

kernel_name = "hybrid_hgrn2_gmlp_shortconv_dit_block"


def rms_norm(x, g):
    xf = x.astype(jnp.float32)
    y = xf * lax.rsqrt(jnp.mean(xf * xf, axis=-1, keepdims=True) + EPS)
    return (y * g.astype(jnp.float32)).astype(x.dtype)


def layer_norm(x, g, b):
    xf = x.astype(jnp.float32)
    mu = jnp.mean(xf, axis=-1, keepdims=True)
    d = xf - mu
    var = jnp.mean(d * d, axis=-1, keepdims=True)
    return (d * lax.rsqrt(var + EPS) * g.astype(jnp.float32) + b.astype(jnp.float32)).astype(x.dtype)


def adaln_in(s, g, shift, scale):
    return rms_norm(s, g) * (1 + scale) + shift


def swiglu(h, wg, wu, wd):
    return (jax.nn.silu(h @ wg) * (h @ wu)) @ wd


def to_heads(a, n_heads):
    bsz, L, w = a.shape
    return a.reshape(bsz, L, n_heads, w // n_heads).transpose(0, 2, 1, 3)


def log_forget(z, lb):
    zf = z.astype(jnp.float32)
    return jnp.logaddexp(jnp.log1p(-lb) + jax.nn.log_sigmoid(zf), jnp.log(lb))


def gla_chunk_scan(q, k, v, logf, s0):
    bsz, nh, L, dk = q.shape
    dv = v.shape[-1]
    n = L // HGRN_CHUNK

    def blk(a):
        return a.reshape(bsz, nh, n, HGRN_CHUNK, a.shape[-1])

    q, k, v, logf = blk(q), blk(k), blk(v), blk(logf)
    cum = jnp.cumsum(logf, axis=3)
    tri = jnp.tril(jnp.ones((HGRN_CHUNK, HGRN_CHUNK), bool))[:, :, None]
    diff = cum[:, :, :, :, None, :] - cum[:, :, :, None, :, :]
    decay = jnp.where(tri, jnp.exp(jnp.where(tri, diff, 0.0)), 0.0)
    scores = jnp.einsum('bhntk,bhntsk,bhnsk->bhnts', q, decay, k)
    o_intra = jnp.einsum('bhnts,bhnsv->bhntv', scores, v)
    q_dec = q * jnp.exp(cum)
    k_dec = k * jnp.exp(cum[:, :, :, -1:, :] - cum)
    g_last = jnp.exp(cum[:, :, :, -1, :])
    xs = (jnp.moveaxis(q_dec, 2, 0), jnp.moveaxis(k_dec, 2, 0),
          jnp.moveaxis(v, 2, 0), jnp.moveaxis(g_last, 2, 0))

    def step(s, inp):
        qd, kd, vv, gl = inp
        o = jnp.einsum('bhtk,bhkv->bhtv', qd, s)
        s = gl[..., None] * s + jnp.einsum('bhsk,bhsv->bhkv', kd, vv)
        return s, o

    s_fin, o_inter = lax.scan(step, s0, xs)
    o = o_intra + jnp.moveaxis(o_inter, 0, 2)
    return o.reshape(bsz, nh, L, dv), s_fin


def hgrn2_bidir(q, i, z_f, z_b, lb_f, lb_b, s0_f, s0_b):
    qh = to_heads(q, A_HEADS).astype(jnp.float32) * (A_DK ** -0.5)
    vh = to_heads(i, A_HEADS).astype(jnp.float32)
    lf_f = log_forget(to_heads(z_f, A_HEADS), lb_f.reshape(A_HEADS, 1, A_DK))
    lf_b = log_forget(to_heads(z_b, A_HEADS), lb_b.reshape(A_HEADS, 1, A_DK))
    o_f, s_f = gla_chunk_scan(qh, -jnp.expm1(lf_f), vh, lf_f, s0_f)

    def rev(a):
        return jnp.flip(a, axis=2)

    o_b, s_b = gla_chunk_scan(rev(qh), rev(-jnp.expm1(lf_b)), rev(vh), rev(lf_b), s0_b)
    return o_f + rev(o_b), s_f, s_b


def conv3(x, w):
    xp = jnp.pad(x, [(0, 0)] * (x.ndim - 2) + [(1, 1), (0, 0)])
    return w[0] * xp[..., :-2, :] + w[1] * xp[..., 1:-1, :] + w[2] * xp[..., 2:, :]


def chunk_spatial_gate(u, v, ln_g, ln_b, w_s, b_s):
    u = jax.nn.gelu(u)
    v = layer_norm(jax.nn.gelu(v), ln_g, ln_b)
    bsz, L, _ = v.shape
    vb = v.reshape(bsz, L // B_CHUNK, B_CHUNK, B_GROUPS, B_GROUP_CH)
    mixed = jnp.einsum('gts,bnsgc->bntgc', w_s, vb) + b_s.T[:, :, None]
    return u * mixed.reshape(bsz, L, B_WIDTH)


def mixer_stream(h, s0_f, s0_b, w_in, lb_f, lb_b, a_norm_g, ln_g, ln_b, w_s, b_s,
                 conv_w, w_pa, w_pb, w_pc, w_o, grid_rows, need_out):
    bsz, L, _ = h.shape
    z = h @ w_in
    offs = [int(o) for o in np.cumsum(IN_SPLITS)[:-1]]
    (q, i, z_f, z_b, g_a, u, v, h_c, gate_b, gate_c, m_a, m_b, m_c) = jnp.split(z, offs, axis=-1)
    o, s_f, s_b = hgrn2_bidir(q, i, z_f, z_b, lb_f, lb_b, s0_f, s0_b)
    if not need_out:
        return None, s_f, s_b
    o = o.transpose(0, 2, 1, 3)
    g_a = g_a.reshape(bsz, L, A_HEADS, A_DV).astype(jnp.float32)
    y_a = (rms_norm(o, a_norm_g) * jax.nn.silu(g_a)).reshape(bsz, L, A_WV).astype(h.dtype)
    y_b = chunk_spatial_gate(u, v, ln_g, ln_b, w_s, b_s)
    t = gate_c * h_c
    if grid_rows is None:
        t = conv3(t, conv_w)
    else:
        t = conv3(t.reshape(bsz, grid_rows, GRID_W, C_WIDTH), conv_w).reshape(bsz, L, C_WIDTH)
    y_c = gate_b * t
    merged = (jax.nn.sigmoid(m_a) * (y_a @ w_pa) + jax.nn.sigmoid(m_b) * (y_b @ w_pb)
              + jax.nn.sigmoid(m_c) * (y_c @ w_pc))
    return merged @ w_o, s_f, s_b


def setup_inputs(seed: int = 0) -> dict:
    key = jax.random.key(seed)
    ks = jax.random.split(key, 24)

    def nrm(k, shape, scale):
        return jax.random.normal(k, shape, jnp.float32) * scale

    D = D_MODEL
    return {
        "x": nrm(ks[0], (BATCH, SEQ, D), 1.0),
        "c": nrm(ks[1], (BATCH, D), 1.0),
        "ctx": nrm(ks[2], (BATCH, CTX_LEN, D), 1.0),
        "c_ctx": nrm(ks[3], (D,), 1.0),
        "w_mod": nrm(ks[4], (DEPTH, D, N_MOD * D), 0.5 * D ** -0.5),
        "b_mod": nrm(ks[5], (DEPTH, N_MOD * D), 0.05),
        "norm_g": 1.0 + nrm(ks[6], (DEPTH, 3, D), 0.05),
        "final_norm_g": 1.0 + nrm(ks[7], (D,), 0.05),
        "ffn_w_gate": nrm(ks[8], (DEPTH, 2, D, D_FF), D ** -0.5),
        "ffn_w_up": nrm(ks[9], (DEPTH, 2, D, D_FF), D ** -0.5),
        "ffn_w_down": nrm(ks[10], (DEPTH, 2, D_FF, D), D_FF ** -0.5),
        "w_in": nrm(ks[11], (DEPTH, D, IN_WIDTH), D ** -0.5),
        "hgrn_lb_logits": nrm(ks[12], (DEPTH, 2, A_WK), 0.5),
        "hgrn_out_norm_g": 1.0 + nrm(ks[13], (DEPTH, A_DV), 0.05),
        "gmlp_ln_g": 1.0 + nrm(ks[14], (DEPTH, B_WIDTH), 0.05),
        "gmlp_ln_b": nrm(ks[15], (DEPTH, B_WIDTH), 0.02),
        "gmlp_w_s": nrm(ks[16], (DEPTH, B_GROUPS, B_CHUNK, B_CHUNK), B_CHUNK ** -0.5),
        "gmlp_b_s": 1.0 + nrm(ks[17], (DEPTH, B_GROUPS, B_CHUNK), 0.1),
        "conv_w": nrm(ks[18], (DEPTH, C_KSIZE, C_WIDTH), C_KSIZE ** -0.5),
        "w_proj_a": nrm(ks[19], (DEPTH, A_WV, D), A_WV ** -0.5),
        "w_proj_b": nrm(ks[20], (DEPTH, B_WIDTH, D), B_WIDTH ** -0.5),
        "w_proj_c": nrm(ks[21], (DEPTH, C_WIDTH, D), C_WIDTH ** -0.5),
        "w_out": nrm(ks[22], (DEPTH, D, D), D ** -0.5),
    }


def reference(x, c, ctx, c_ctx, w_mod, b_mod, norm_g, final_norm_g, ffn_w_gate, ffn_w_up,
              ffn_w_down, w_in, hgrn_lb_logits, hgrn_out_norm_g, gmlp_ln_g, gmlp_ln_b,
              gmlp_w_s, gmlp_b_s, conv_w, w_proj_a, w_proj_b, w_proj_c, w_out):
    bsz, L, _ = x.shape
    rows = L // GRID_W
    lb_all = jnp.cumsum(jax.nn.softmax(hgrn_lb_logits.astype(jnp.float32), axis=0), axis=0)
    lb_all = lb_all - lb_all[:1]
    s_zero = jnp.zeros((bsz, A_HEADS, A_DK, A_DV), jnp.float32)
    y, cs = x, ctx
    for l in range(DEPTH):
        last = l == DEPTH - 1
        mx = jnp.split((jax.nn.silu(c) @ w_mod[l] + b_mod[l])[:, None, :], N_MOD, axis=-1)
        mc = jnp.split((jax.nn.silu(c_ctx) @ w_mod[l] + b_mod[l])[None, None, :], N_MOD, axis=-1)
        y = y + 0.5 * mx[2] * swiglu(adaln_in(y, norm_g[l, 0], mx[0], mx[1]),
                                     ffn_w_gate[l, 0], ffn_w_up[l, 0], ffn_w_down[l, 0])
        cs = cs + 0.5 * mc[2] * swiglu(adaln_in(cs, norm_g[l, 0], mc[0], mc[1]),
                                       ffn_w_gate[l, 0], ffn_w_up[l, 0], ffn_w_down[l, 0])
        mix_w = (w_in[l], lb_all[l, 0], lb_all[l, 1], hgrn_out_norm_g[l], gmlp_ln_g[l], gmlp_ln_b[l],
                 gmlp_w_s[l], gmlp_b_s[l], conv_w[l], w_proj_a[l], w_proj_b[l], w_proj_c[l], w_out[l])
        out_c, s_f, s_b = mixer_stream(adaln_in(cs, norm_g[l, 1], mc[3], mc[4]), s_zero, s_zero,
                                       *mix_w, None, not last)
        out_x, _, _ = mixer_stream(adaln_in(y, norm_g[l, 1], mx[3], mx[4]), s_f, s_b,
                                   *mix_w, rows, True)
        y = y + mx[5] * out_x
        if not last:
            cs = cs + mc[5] * out_c
            cs = cs + 0.5 * mc[8] * swiglu(adaln_in(cs, norm_g[l, 2], mc[6], mc[7]),
                                           ffn_w_gate[l, 1], ffn_w_up[l, 1], ffn_w_down[l, 1])
        y = y + 0.5 * mx[8] * swiglu(adaln_in(y, norm_g[l, 2], mx[6], mx[7]),
                                     ffn_w_gate[l, 1], ffn_w_up[l, 1], ffn_w_down[l, 1])
    return rms_norm(y, final_norm_g)
```

```python
import functools
import math

import numpy as np
import jax
import jax.numpy as jnp
from jax import lax
from jax.experimental import pallas as pl
from jax.experimental.pallas import tpu as pltpu

F32 = jnp.float32
BF16 = jnp.bfloat16

EPS = 1e-6
N_MOD = 9
GRID_W = 64
A_HEADS = 8
A_DK = 128
A_DV = 128
B_GROUPS = 4
B_GROUP_CH = 128
B_CHUNK = 128
C_WIDTH = 512

LANES = 128
VMEM_LIMIT_BYTES = 60 * 1024 * 1024

HGRN_CHUNK = 128
FFN_TM = 544
FFN_TF = 512
IN_TM = 544
IN_TN = 512
MIX_TM = 256
MOD_TN = 1024
NORM_TM = 256


def _params(*sem):
    return pltpu.CompilerParams(dimension_semantics=sem, vmem_limit_bytes=VMEM_LIMIT_BYTES)


def _sigmoid(a):
    return 1.0 / (1.0 + jnp.exp(-a))


def _gelu_tanh(a):
    return 0.5 * a * (1.0 + jnp.tanh(math.sqrt(2.0 / math.pi) * (a + 0.044715 * (a * a * a))))


def _adaln(x, g, shift, scale):
    ms = jnp.mean(x * x, axis=-1, keepdims=True)
    return (x * lax.rsqrt(ms + EPS) * g) * (1.0 + scale) + shift


def _pick(is_ctx, mc_ref, mx_ref, k):
    return jnp.where(is_ctx, mc_ref[k:k + 1, :], mx_ref[k:k + 1, :])


def _rows_are_ctx(tile, tm, ctx_len):
    row = tile * tm + lax.broadcasted_iota(jnp.int32, (tm, 1), 0)
    return row < ctx_len


def _mod_kernel(c_ref, w_ref, b_ref, o_ref):
    c = c_ref[...]
    s = (c * _sigmoid(c)).astype(BF16)
    o_ref[...] = jnp.dot(s, w_ref[...].astype(BF16), preferred_element_type=F32) + b_ref[...]


def _modulation(cvec, w_mod, b_mod):
    depth, d, nw = w_mod.shape
    rows = cvec.shape[0]
    tn = MOD_TN
    return pl.pallas_call(
        _mod_kernel,
        grid=(depth, nw // tn),
        in_specs=[
            pl.BlockSpec((rows, d), lambda l, j: (0, 0)),
            pl.BlockSpec((None, d, tn), lambda l, j: (l, 0, j)),
            pl.BlockSpec((None, 1, tn), lambda l, j: (l, 0, j)),
        ],
        out_specs=pl.BlockSpec((None, rows, tn), lambda l, j: (l, 0, j)),
        out_shape=jax.ShapeDtypeStruct((depth, rows, nw), F32),
        compiler_params=_params("arbitrary", "arbitrary"),
        name="modulation",
    )(cvec, w_mod, b_mod.reshape(depth, 1, nw))


def _ffn_kernel(x_ref, mx_ref, mc_ref, g_ref, wg_ref, wu_ref, wd_ref, o_ref, h_sc, acc_sc,
                *, slot, tm, ctx_len):
    i = pl.program_id(1)
    j = pl.program_id(2)
    is_ctx = _rows_are_ctx(i, tm, ctx_len)

    @pl.when(j == 0)
    def _():
        h = _adaln(x_ref[...], g_ref[...], _pick(is_ctx, mc_ref, mx_ref, 3 * slot),
                   _pick(is_ctx, mc_ref, mx_ref, 3 * slot + 1))
        h_sc[...] = h.astype(BF16)
        acc_sc[...] = jnp.zeros_like(acc_sc)

    h = h_sc[...]
    a = jnp.dot(h, wg_ref[...], preferred_element_type=F32)
    u = jnp.dot(h, wu_ref[...], preferred_element_type=F32)
    act = ((a * _sigmoid(a)) * u).astype(BF16)
    acc_sc[...] += jnp.dot(act, wd_ref[...], preferred_element_type=F32)

    @pl.when(j == pl.num_programs(2) - 1)
    def _():
        gate = _pick(is_ctx, mc_ref, mx_ref, 3 * slot + 2)
        o_ref[...] = x_ref[...] + (0.5 * gate) * acc_sc[...]


def _ffn(xa, mods, g, wg, wu, wd, *, slot, ctx_len):
    bsz, t, d = xa.shape
    f = wg.shape[1]
    tm, tf = FFN_TM, FFN_TF
    ctx_row = bsz
    kern = functools.partial(_ffn_kernel, slot=slot, tm=tm, ctx_len=ctx_len)
    return pl.pallas_call(
        kern,
        grid=(bsz, t // tm, f // tf),
        in_specs=[
            pl.BlockSpec((None, tm, d), lambda b, i, j: (b, i, 0)),
            pl.BlockSpec((None, N_MOD, d), lambda b, i, j: (b, 0, 0)),
            pl.BlockSpec((None, N_MOD, d), lambda b, i, j: (ctx_row, 0, 0)),
            pl.BlockSpec((1, d), lambda b, i, j: (0, 0)),
            pl.BlockSpec((d, tf), lambda b, i, j: (0, j)),
            pl.BlockSpec((d, tf), lambda b, i, j: (0, j)),
            pl.BlockSpec((tf, d), lambda b, i, j: (j, 0)),
        ],
        out_specs=pl.BlockSpec((None, tm, d), lambda b, i, j: (b, i, 0)),
        out_shape=jax.ShapeDtypeStruct(xa.shape, F32),
        scratch_shapes=[pltpu.VMEM((tm, d), BF16), pltpu.VMEM((tm, d), F32)],
        compiler_params=_params("arbitrary", "arbitrary", "arbitrary"),
        name=f"ffn{slot}",
    )(xa, mods, mods, g.reshape(1, d), wg, wu, wd)


def _log_forget(z, la, lbl):
    ls = jnp.minimum(z, 0.0) - jnp.log(1.0 + jnp.exp(-jnp.abs(z)))
    a = la + ls
    hi = jnp.maximum(a, lbl)
    return hi + jnp.log(1.0 + jnp.exp(-jnp.abs(a - lbl)))


def _inproj_kernel(x_ref, mx_ref, mc_ref, g_ref, w_ref, la_ref, lbl_ref, o_ref, h_sc,
                   *, tm, tn, ctx_len, lf_lo, lf_hi):
    i = pl.program_id(1)
    j = pl.program_id(2)

    @pl.when(j == 0)
    def _():
        is_ctx = _rows_are_ctx(i, tm, ctx_len)
        h = _adaln(x_ref[...], g_ref[...], _pick(is_ctx, mc_ref, mx_ref, 3),
                   _pick(is_ctx, mc_ref, mx_ref, 4))
        h_sc[...] = h.astype(BF16)

    z = jnp.dot(h_sc[...], w_ref[...], preferred_element_type=F32)
    is_lf = jnp.logical_and(j * tn >= lf_lo, j * tn < lf_hi)

    @pl.when(is_lf)
    def _():
        o_ref[...] = _log_forget(z, la_ref[...], lbl_ref[...]).astype(o_ref.dtype)

    @pl.when(jnp.logical_not(is_lf))
    def _():
        o_ref[...] = z.astype(o_ref.dtype)


def _inproj(xa, mods, g, w, la, lbl, *, ctx_len, lf_lo):
    bsz, t, d = xa.shape
    nw = w.shape[1]
    tm, tn = IN_TM, IN_TN
    lf_w = la.shape[1]
    n_lf = lf_w // tn
    ctx_row = bsz
    kern = functools.partial(_inproj_kernel, tm=tm, tn=tn, ctx_len=ctx_len,
                             lf_lo=lf_lo, lf_hi=lf_lo + lf_w)

    def lf_idx(b, i, j):
        return (0, jnp.clip(j - lf_lo // tn, 0, n_lf - 1))

    return pl.pallas_call(
        kern,
        grid=(bsz, t // tm, nw // tn),
        in_specs=[
            pl.BlockSpec((None, tm, d), lambda b, i, j: (b, i, 0)),
            pl.BlockSpec((None, N_MOD, d), lambda b, i, j: (b, 0, 0)),
            pl.BlockSpec((None, N_MOD, d), lambda b, i, j: (ctx_row, 0, 0)),
            pl.BlockSpec((1, d), lambda b, i, j: (0, 0)),
            pl.BlockSpec((d, tn), lambda b, i, j: (0, j)),
            pl.BlockSpec((1, tn), lf_idx),
            pl.BlockSpec((1, tn), lf_idx),
        ],
        out_specs=pl.BlockSpec((None, tm, tn), lambda b, i, j: (b, i, j)),
        out_shape=jax.ShapeDtypeStruct((bsz, t, nw), BF16),
        scratch_shapes=[pltpu.VMEM((tm, d), BF16)],
        compiler_params=_params("arbitrary", "arbitrary", "arbitrary"),
        name="inproj",
    )(xa, mods, mods, g.reshape(1, d), w, la, lbl)


def _hgrn_tables(c):
    levels = int(math.log2(c))
    t = np.arange(c)[:, None]
    r = np.arange(c)[None, :]
    sums = np.zeros((2, 2 + levels, c, c), np.float32)
    masks = np.zeros((2, levels, c, c), np.float32)
    sums[0, 0] = r <= t
    sums[0, 1] = r > t
    sums[1, 0] = r >= t
    sums[1, 1] = r < t
    for l in range(levels):
        h = 1 << l
        mid = (t // (2 * h)) * 2 * h + h - 1
        later = (t & h) != 0
        sums[0, 2 + l] = np.where(later, (r > mid) & (r <= t), (r > t) & (r <= mid))
        sums[1, 2 + l] = np.where(later, (r > mid) & (r < t), (r >= t) & (r <= mid))
        same = (t // (2 * h)) == (r // (2 * h))
        masks[0, l] = same & later & ((r & h) == 0)
        masks[1, l] = same & (~later) & ((r & h) != 0)
    return sums.reshape(2, (2 + levels) * c, c), masks


_NT = (((1,), (1,)), ((), ()))
_TN = (((0,), (0,)), ((), ()))


def _hgrn_kernel(q_ref, v_ref, lff_ref, lfb_ref, ga_ref, sums_ref, masks_ref, gn_ref, y_ref,
                 of_sc, ob_sc, st_sc, *, n_ctx_chunks, n_chunks, scale):
    c = HGRN_CHUNK
    levels = masks_ref.shape[1]

    def chunk(direction, ci, lf_ref, o_sc):
        rows = pl.ds(pl.multiple_of(ci * c, c), c)
        lf = lf_ref[rows, :]
        e = jnp.dot(sums_ref[direction], lf, preferred_element_type=F32)
        q = q_ref[rows, :].astype(F32)
        v = v_ref[rows, :]
        k = 1.0 - jnp.exp(lf.astype(F32))
        st = st_sc[direction]
        qd = (q * jnp.exp(e[0:c])).astype(BF16)
        kd = (k * jnp.exp(e[c:2 * c])).astype(BF16)
        o = lax.dot_general(qd, st.astype(BF16), _NT, preferred_element_type=F32)
        a = jnp.zeros((c, c), F32)
        for l in range(levels):
            w = jnp.exp(e[(2 + l) * c:(3 + l) * c])
            p = lax.dot_general((q * w).astype(BF16), (k * w).astype(BF16), _NT,
                                preferred_element_type=F32)
            a = a + p * masks_ref[direction, l]
        o = o + jnp.dot(a.astype(BF16), v, preferred_element_type=F32)
        o = o + jnp.sum(q * k, axis=-1, keepdims=True) * v.astype(F32)
        o_sc[rows, :] = o
        total = (c - 1) if direction == 0 else 0
        g_all = jnp.exp(e[total:total + 1])
        st_sc[direction] = st * g_all + lax.dot_general(v, kd, _TN, preferred_element_type=F32)

    st_sc[...] = jnp.zeros_like(st_sc)

    def body(j, carry):
        chunk(0, j, lff_ref, of_sc)
        cb = jnp.where(j < n_ctx_chunks, n_ctx_chunks - 1 - j, n_chunks - 1 + n_ctx_chunks - j)
        chunk(1, cb, lfb_ref, ob_sc)
        return carry

    lax.fori_loop(0, n_chunks, body, 0)

    def readout(j, carry):
        rows = pl.ds(pl.multiple_of(j * c, c), c)
        o = (of_sc[rows, :] + ob_sc[rows, :]) * scale
        ms = jnp.mean(o * o, axis=-1, keepdims=True)
        ga = ga_ref[rows, :].astype(F32)
        y = (o * lax.rsqrt(ms + EPS) * gn_ref[...]) * (ga * _sigmoid(ga))
        y_ref[rows, :] = y.astype(y_ref.dtype)
        return carry

    lax.fori_loop(0, n_chunks, readout, 0)


def _hgrn(z, sums, masks, gn, *, ctx_len, col_q, col_v, col_ff, col_fb, col_ga):
    bsz, t, _ = z.shape
    c = HGRN_CHUNK
    dk = A_DK
    kern = functools.partial(_hgrn_kernel, n_ctx_chunks=ctx_len // c, n_chunks=t // c,
                             scale=A_DK ** -0.5)

    def col(start):
        return pl.BlockSpec((None, t, dk), lambda b, h: (b, 0, start // dk + h))

    return pl.pallas_call(
        kern,
        grid=(bsz, A_HEADS),
        in_specs=[
            col(col_q), col(col_v), col(col_ff), col(col_fb), col(col_ga),
            pl.BlockSpec(sums.shape, lambda b, h: (0, 0, 0)),
            pl.BlockSpec(masks.shape, lambda b, h: (0, 0, 0, 0)),
            pl.BlockSpec((1, A_DV), lambda b, h: (0, 0)),
        ],
        out_specs=pl.BlockSpec((None, t, A_DV), lambda b, h: (b, 0, h)),
        out_shape=jax.ShapeDtypeStruct((bsz, t, A_HEADS * A_DV), BF16),
        scratch_shapes=[pltpu.VMEM((t, A_DV), F32), pltpu.VMEM((t, A_DV), F32),
                        pltpu.VMEM((2, A_DV, dk), F32)],
        compiler_params=_params("arbitrary", "arbitrary"),
        name="hgrn",
    )(z, z, z, z, z, sums, masks, gn.reshape(1, A_DV))


def _mix_kernel(x_ref, ya_ref, ma_ref, mb_ref, mcg_ref, u_ref, v_ref, hc_ref, gb_ref, gc_ref,
                mx_ref, mc_ref, lng_ref, lnb_ref, ws_ref, bs_ref, cw_ref,
                wpa_ref, wpb_ref, wpc_ref, wo_ref, o_ref, yb_sc, *, tm, ctx_len):
    i = pl.program_id(1)
    tile_is_ctx = i * tm < ctx_len

    vg = _gelu_tanh(v_ref[...].astype(F32))
    mu = jnp.mean(vg, axis=-1, keepdims=True)
    dv = vg - mu
    var = jnp.mean(dv * dv, axis=-1, keepdims=True)
    vn = (dv * lax.rsqrt(var + EPS) * lng_ref[...] + lnb_ref[...]).astype(BF16)
    for n in range(tm // B_CHUNK):
        rs = slice(n * B_CHUNK, (n + 1) * B_CHUNK)
        for g in range(B_GROUPS):
            cs = slice(g * B_GROUP_CH, (g + 1) * B_GROUP_CH)
            mixed = jnp.dot(ws_ref[g], vn[rs, cs], preferred_element_type=F32) + bs_ref[g]
            yb_sc[rs, cs] = (_gelu_tanh(u_ref[rs, cs].astype(F32)) * mixed).astype(BF16)

    period = jnp.where(tile_is_ctx, ctx_len, GRID_W)
    pos = (i * tm + lax.broadcasted_iota(jnp.int32, (tm, 1), 0)) % period
    tt = gc_ref[...].astype(F32) * hc_ref[...].astype(F32)
    prev = jnp.where(pos == 0, 0.0, pltpu.roll(tt, 1, axis=0))
    nxt = jnp.where(pos == period - 1, 0.0, pltpu.roll(tt, tm - 1, axis=0))
    conv = cw_ref[0:1, :] * prev + cw_ref[1:2, :] * tt + cw_ref[2:3, :] * nxt
    yc = (gb_ref[...].astype(F32) * conv).astype(BF16)

    merged = _sigmoid(ma_ref[...].astype(F32)) * jnp.dot(
        ya_ref[...], wpa_ref[...], preferred_element_type=F32)
    merged += _sigmoid(mb_ref[...].astype(F32)) * jnp.dot(
        yb_sc[...], wpb_ref[...], preferred_element_type=F32)
    merged += _sigmoid(mcg_ref[...].astype(F32)) * jnp.dot(
        yc, wpc_ref[...], preferred_element_type=F32)
    out = jnp.dot(merged.astype(BF16), wo_ref[...], preferred_element_type=F32)
    gate = jnp.where(tile_is_ctx, mc_ref[5:6, :], mx_ref[5:6, :])
    o_ref[...] = x_ref[...] + gate * out


def _mix(xa, ya, z, mods, ln_g, ln_b, ws, bs, cw, wpa, wpb, wpc, wo, *, ctx_len, col_m, col_s):
    bsz, t, d = xa.shape
    tm = MIX_TM
    assert ctx_len % tm == 0 and tm % B_CHUNK == 0 and tm % GRID_W == 0
    bw = B_GROUPS * B_GROUP_CH
    ctx_row = bsz
    kern = functools.partial(_mix_kernel, tm=tm, ctx_len=ctx_len)

    def zcol(start, width):
        return pl.BlockSpec((None, tm, width), lambda b, i: (b, i, start // width))

    def const(shape):
        zeros = (0,) * len(shape)
        return pl.BlockSpec(shape, lambda b, i: zeros, pipeline_mode=pl.Buffered(1))

    return pl.pallas_call(
        kern,
        grid=(bsz, t // tm),
        in_specs=[
            pl.BlockSpec((None, tm, d), lambda b, i: (b, i, 0)),
            pl.BlockSpec((None, tm, ya.shape[2]), lambda b, i: (b, i, 0)),
            zcol(col_m, d), zcol(col_m + d, d), zcol(col_m + 2 * d, d),
            zcol(col_s, bw), zcol(col_s + bw, bw), zcol(col_s + 2 * bw, bw),
            zcol(col_s + 3 * bw, bw), zcol(col_s + 4 * bw, bw),
            pl.BlockSpec((None, N_MOD, d), lambda b, i: (b, 0, 0)),
            pl.BlockSpec((None, N_MOD, d), lambda b, i: (ctx_row, 0, 0)),
            const((1, bw)), const((1, bw)),
            const(ws.shape), const(bs.shape), const(cw.shape),
            const(wpa.shape), const(wpb.shape), const(wpc.shape), const(wo.shape),
        ],
        out_specs=pl.BlockSpec((None, tm, d), lambda b, i: (b, i, 0)),
        out_shape=jax.ShapeDtypeStruct(xa.shape, F32),
        scratch_shapes=[pltpu.VMEM((tm, bw), BF16)],
        compiler_params=_params("arbitrary", "arbitrary"),
        name="mix",
    )(xa, ya, z, z, z, z, z, z, z, z, mods, mods, ln_g.reshape(1, bw), ln_b.reshape(1, bw),
      ws, bs, cw, wpa, wpb, wpc, wo)


def _final_norm_kernel(x_ref, g_ref, o_ref):
    x = x_ref[...]
    ms = jnp.mean(x * x, axis=-1, keepdims=True)
    o_ref[...] = x * lax.rsqrt(ms + EPS) * g_ref[...]


def _final_norm(xa, g, *, ctx_len):
    bsz, t, d = xa.shape
    tm = NORM_TM
    assert ctx_len % tm == 0
    skip = ctx_len // tm
    return pl.pallas_call(
        _final_norm_kernel,
        grid=(bsz, (t - ctx_len) // tm),
        in_specs=[pl.BlockSpec((None, tm, d), lambda b, i: (b, i + skip, 0)),
                  pl.BlockSpec((1, d), lambda b, i: (0, 0))],
        out_specs=pl.BlockSpec((None, tm, d), lambda b, i: (b, i, 0)),
        out_shape=jax.ShapeDtypeStruct((bsz, t - ctx_len, d), F32),
        compiler_params=_params("arbitrary", "arbitrary"),
        name="final_norm",
    )(xa, g.reshape(1, d))


def _permute_w_in(w, d):
    wk = A_HEADS * A_DK
    wv = A_HEADS * A_DV
    bw = B_GROUPS * B_GROUP_CH
    sizes = (wk, wv, wk, wk, wv, bw, bw, C_WIDTH, C_WIDTH, C_WIDTH, d, d, d)
    offs = np.concatenate([[0], np.cumsum(sizes)])
    parts = [w[:, offs[k]:offs[k + 1]] for k in range(len(sizes))]
    q, iv, zf, zb, ga, u, v, hc, gb, gc, ma, mb, mc = parts
    cols = dict(q=0, v=wk, ff=wk + wv, fb=2 * wk + wv, m=3 * wk + wv,
                ga=3 * wk + wv + 3 * d, s=3 * wk + 2 * wv + 3 * d)
    return jnp.concatenate([q, iv, zf, zb, ma, mb, mc, ga, u, v, hc, gb, gc], axis=1), cols


def kernel(x, c, ctx, c_ctx, w_mod, b_mod, norm_g, final_norm_g, ffn_w_gate, ffn_w_up,
           ffn_w_down, w_in, hgrn_lb_logits, hgrn_out_norm_g, gmlp_ln_g, gmlp_ln_b,
           gmlp_w_s, gmlp_b_s, conv_w, w_proj_a, w_proj_b, w_proj_c, w_out):
    bsz, _, d = x.shape
    depth = w_mod.shape[0]
    ctx_len = ctx.shape[1]
    xa = jnp.concatenate([ctx, x], axis=1)

    mod_rows = 8 * (-(-(bsz + 1) // 8))
    cvec = jnp.zeros((mod_rows, d), F32).at[:bsz].set(c).at[bsz].set(c_ctx)
    mods = _modulation(cvec, w_mod, b_mod).reshape(depth, mod_rows, N_MOD, d)

    lb = jnp.cumsum(jax.nn.softmax(hgrn_lb_logits.astype(F32), axis=0), axis=0)
    lb = (lb - lb[:1]).reshape(depth, 1, -1)
    la_all = jnp.log1p(-lb)
    lbl_all = jnp.log(lb)

    sums_np, masks_np = _hgrn_tables(HGRN_CHUNK)
    sums = jnp.asarray(sums_np, BF16)
    masks = jnp.asarray(masks_np, F32)

    for l in range(depth):
        w_in_l, cols = _permute_w_in(w_in[l].astype(BF16), d)
        xa = _ffn(xa, mods[l], norm_g[l, 0], ffn_w_gate[l, 0].astype(BF16),
                  ffn_w_up[l, 0].astype(BF16), ffn_w_down[l, 0].astype(BF16),
                  slot=0, ctx_len=ctx_len)
        z = _inproj(xa, mods[l], norm_g[l, 1], w_in_l, la_all[l], lbl_all[l],
                    ctx_len=ctx_len, lf_lo=cols["ff"])
        ya = _hgrn(z, sums, masks, hgrn_out_norm_g[l], ctx_len=ctx_len, col_q=cols["q"],
                   col_v=cols["v"], col_ff=cols["ff"], col_fb=cols["fb"], col_ga=cols["ga"])
        xa = _mix(xa, ya, z, mods[l], gmlp_ln_g[l], gmlp_ln_b[l], gmlp_w_s[l].astype(BF16),
                  gmlp_b_s[l][:, :, None], conv_w[l], w_proj_a[l].astype(BF16),
                  w_proj_b[l].astype(BF16), w_proj_c[l].astype(BF16), w_out[l].astype(BF16),
                  ctx_len=ctx_len, col_m=cols["m"], col_s=cols["s"])
        xa = _ffn(xa, mods[l], norm_g[l, 2], ffn_w_gate[l, 1].astype(BF16),
                  ffn_w_up[l, 1].astype(BF16), ffn_w_down[l, 1].astype(BF16),
                  slot=2, ctx_len=ctx_len)
    return _final_norm(xa, final_norm_g, ctx_len=ctx_len)
```

```python
import functools
import math

import numpy as np
import jax
import jax.numpy as jnp
from jax import lax
from jax.experimental import pallas as pl
from jax.experimental.pallas import tpu as pltpu

F32 = jnp.float32
BF16 = jnp.bfloat16

EPS = 1e-6
N_MOD = 9
GRID_W = 64
A_HEADS = 8
A_DK = 128
A_DV = 128
B_GROUPS = 4
B_GROUP_CH = 128
B_CHUNK = 128
C_WIDTH = 512

VMEM_LIMIT_BYTES = 60 * 1024 * 1024

HGRN_CHUNK = 128
ROW_BLOCK = 16
FFN_TM = 544
FFN_TF = 512
IN_TM = 544
IN_TN = 1536
MIX_TM = 256
MOD_TN = 1024
NORM_TM = 256


def _params(*sem):
    return pltpu.CompilerParams(dimension_semantics=sem, vmem_limit_bytes=VMEM_LIMIT_BYTES)


def _sigmoid(a):
    return 0.5 * (jnp.tanh(0.5 * a) + 1.0)


def _gelu_tanh(a):
    return 0.5 * a * (1.0 + jnp.tanh(math.sqrt(2.0 / math.pi) * (a + 0.044715 * (a * a * a))))


def _row_blocks(tile, tm, ctx_len, fn):
    def body(r, carry):
        start = pl.multiple_of(r * ROW_BLOCK, ROW_BLOCK)
        fn(pl.ds(start, ROW_BLOCK), tile * tm + start < ctx_len)
        return carry

    n = tm // ROW_BLOCK
    lax.fori_loop(0, n, body, 0, unroll=2 if n % 2 == 0 else 1)


def _adaln_rows(x_ref, h_ref, vec_sc, g_ref, mx_ref, mc_ref, k_shift, tile, tm, ctx_len,
                zero_ref=None):
    d = x_ref.shape[-1]
    for s, m_ref in enumerate((mx_ref, mc_ref)):
        gain = g_ref[...] * (1.0 + m_ref[k_shift + 1:k_shift + 2, :])
        vec_sc[s, 0] = jnp.broadcast_to(gain, (ROW_BLOCK, d))
        vec_sc[s, 1] = jnp.broadcast_to(m_ref[k_shift:k_shift + 1, :], (ROW_BLOCK, d))

    def block(rows, is_ctx):
        s = is_ctx.astype(jnp.int32)
        x = x_ref[rows, :]
        ms = jnp.mean(x * x, axis=-1, keepdims=True)
        h = (x * lax.rsqrt(ms + EPS)) * vec_sc[s, 0] + vec_sc[s, 1]
        h_ref[rows, :] = h.astype(h_ref.dtype)
        if zero_ref is not None:
            zero_ref[rows, :] = jnp.zeros((ROW_BLOCK, d), zero_ref.dtype)

    _row_blocks(tile, tm, ctx_len, block)


def _mod_kernel(c_ref, w_ref, b_ref, o_ref):
    c = c_ref[...]
    s = (c * _sigmoid(c)).astype(BF16)
    o_ref[...] = jnp.dot(s, w_ref[...].astype(BF16), preferred_element_type=F32) + b_ref[...]


def _modulation(cvec, w_mod, b_mod):
    depth, d, nw = w_mod.shape
    rows = cvec.shape[0]
    tn = MOD_TN
    return pl.pallas_call(
        _mod_kernel,
        grid=(depth, nw // tn),
        in_specs=[
            pl.BlockSpec((rows, d), lambda l, j: (0, 0)),
            pl.BlockSpec((None, d, tn), lambda l, j: (l, 0, j)),
            pl.BlockSpec((None, 1, tn), lambda l, j: (l, 0, j)),
        ],
        out_specs=pl.BlockSpec((None, rows, tn), lambda l, j: (l, 0, j)),
        out_shape=jax.ShapeDtypeStruct((depth, rows, nw), F32),
        compiler_params=_params("arbitrary", "arbitrary"),
        name="modulation",
    )(cvec, w_mod, b_mod.reshape(depth, 1, nw))


def _ffn_kernel(x_ref, mx_ref, mc_ref, g_ref, wg_ref, wu_ref, wd_ref, o_ref, h_sc, acc_sc, vec_sc,
                *, slot, tm, ctx_len):
    i = pl.program_id(1)
    j = pl.program_id(2)

    @pl.when(j == 0)
    def _():
        _adaln_rows(x_ref, h_sc, vec_sc, g_ref, mx_ref, mc_ref, 3 * slot, i, tm, ctx_len,
                    zero_ref=acc_sc)

    h = h_sc[...]
    a = jnp.dot(h, wg_ref[...], preferred_element_type=F32)
    u = jnp.dot(h, wu_ref[...], preferred_element_type=F32)
    act = ((a * _sigmoid(a)) * u).astype(BF16)
    acc_sc[...] += jnp.dot(act, wd_ref[...], preferred_element_type=F32)

    @pl.when(j == pl.num_programs(2) - 1)
    def _():
        k_gate = 3 * slot + 2

        def block(rows, is_ctx):
            gate = jnp.where(is_ctx, mc_ref[k_gate:k_gate + 1, :], mx_ref[k_gate:k_gate + 1, :])
            o_ref[rows, :] = x_ref[rows, :] + (0.5 * gate) * acc_sc[rows, :]

        _row_blocks(i, tm, ctx_len, block)


def _ffn(xa, mods, norm_g, wg, wu, wd, *, layer, slot, which, ctx_len):
    bsz, t, d = xa.shape
    f = wg.shape[-1]
    tm, tf = FFN_TM, FFN_TF
    ctx_row = bsz
    kern = functools.partial(_ffn_kernel, slot=slot, tm=tm, ctx_len=ctx_len)
    return pl.pallas_call(
        kern,
        grid=(bsz, t // tm, f // tf),
        in_specs=[
            pl.BlockSpec((None, tm, d), lambda b, i, j: (b, i, 0)),
            pl.BlockSpec((None, None, N_MOD, d), lambda b, i, j: (layer, b, 0, 0)),
            pl.BlockSpec((None, None, N_MOD, d), lambda b, i, j: (layer, ctx_row, 0, 0)),
            pl.BlockSpec((None, None, 1, d), lambda b, i, j: (layer, slot, 0, 0)),
            pl.BlockSpec((None, None, d, tf), lambda b, i, j: (layer, which, 0, j)),
            pl.BlockSpec((None, None, d, tf), lambda b, i, j: (layer, which, 0, j)),
            pl.BlockSpec((None, None, tf, d), lambda b, i, j: (layer, which, j, 0)),
        ],
        out_specs=pl.BlockSpec((None, tm, d), lambda b, i, j: (b, i, 0)),
        out_shape=jax.ShapeDtypeStruct(xa.shape, F32),
        scratch_shapes=[pltpu.VMEM((tm, d), BF16), pltpu.VMEM((tm, d), F32),
                        pltpu.VMEM((2, 2, ROW_BLOCK, d), F32)],
        compiler_params=_params("arbitrary", "arbitrary", "arbitrary"),
        name=f"ffn{slot}",
    )(xa, mods, mods, norm_g, wg, wu, wd)


def _log_forget(z, la, lbl):
    ls = jnp.minimum(z, 0.0) - jnp.log(1.0 + jnp.exp(-jnp.abs(z)))
    a = la + ls
    hi = jnp.maximum(a, lbl)
    return hi + jnp.log(1.0 + jnp.exp(-jnp.abs(a - lbl)))


def _inproj_kernel(x_ref, mx_ref, mc_ref, g_ref, w_ref, la_ref, lbl_ref, o_ref, h_sc, vec_sc,
                   *, tm, tn, ctx_len, lf_lo, lf_hi):
    i = pl.program_id(1)
    j = pl.program_id(2)

    @pl.when(j == 0)
    def _():
        _adaln_rows(x_ref, h_sc, vec_sc, g_ref, mx_ref, mc_ref, 3, i, tm, ctx_len)

    z = jnp.dot(h_sc[...], w_ref[...], preferred_element_type=F32)
    o_ref[...] = z.astype(o_ref.dtype)

    for jj in range(lf_lo // tn, pl.cdiv(lf_hi, tn)):
        lo = max(lf_lo, jj * tn)
        hi = min(lf_hi, (jj + 1) * tn)
        cols = slice(lo - jj * tn, hi - jj * tn)
        lfc = slice(lo - lf_lo, hi - lf_lo)

        @pl.when(j == jj)
        def _(cols=cols, lfc=lfc):
            o_ref[:, cols] = _log_forget(z[:, cols], la_ref[:, lfc], lbl_ref[:, lfc]).astype(
                o_ref.dtype)


def _inproj(xa, mods, norm_g, w, la, lbl, *, layer, ctx_len, lf_lo):
    bsz, t, d = xa.shape
    nw = w.shape[-1]
    tm, tn = IN_TM, IN_TN
    lf_w = la.shape[-1]
    ctx_row = bsz
    kern = functools.partial(_inproj_kernel, tm=tm, tn=tn, ctx_len=ctx_len,
                             lf_lo=lf_lo, lf_hi=lf_lo + lf_w)
    return pl.pallas_call(
        kern,
        grid=(bsz, t // tm, nw // tn),
        in_specs=[
            pl.BlockSpec((None, tm, d), lambda b, i, j: (b, i, 0)),
            pl.BlockSpec((None, None, N_MOD, d), lambda b, i, j: (layer, b, 0, 0)),
            pl.BlockSpec((None, None, N_MOD, d), lambda b, i, j: (layer, ctx_row, 0, 0)),
            pl.BlockSpec((None, None, 1, d), lambda b, i, j: (layer, 1, 0, 0)),
            pl.BlockSpec((None, d, tn), lambda b, i, j: (layer, 0, j)),
            pl.BlockSpec((None, 1, lf_w), lambda b, i, j: (layer, 0, 0)),
            pl.BlockSpec((None, 1, lf_w), lambda b, i, j: (layer, 0, 0)),
        ],
        out_specs=pl.BlockSpec((None, tm, tn), lambda b, i, j: (b, i, j)),
        out_shape=jax.ShapeDtypeStruct((bsz, t, nw), BF16),
        scratch_shapes=[pltpu.VMEM((tm, d), BF16), pltpu.VMEM((2, 2, ROW_BLOCK, d), F32)],
        compiler_params=_params("arbitrary", "arbitrary", "arbitrary"),
        name="inproj",
    )(xa, mods, mods, norm_g, w, la, lbl)


def _hgrn_tables(c):
    levels = int(math.log2(c))
    t = np.arange(c)[:, None]
    r = np.arange(c)[None, :]
    sums = np.zeros((2, 2 + levels, c, c), np.float32)
    masks = np.zeros((2, levels, c, c), np.float32)
    sums[0, 0] = r <= t
    sums[0, 1] = r > t
    sums[1, 0] = r >= t
    sums[1, 1] = r < t
    for l in range(levels):
        h = 1 << l
        mid = (t // (2 * h)) * 2 * h + h - 1
        later = (t & h) != 0
        sums[0, 2 + l] = np.where(later, (r > mid) & (r <= t), (r > t) & (r <= mid))
        sums[1, 2 + l] = np.where(later, (r > mid) & (r < t), (r >= t) & (r <= mid))
        same = (t // (2 * h)) == (r // (2 * h))
        masks[0, l] = same & later & ((r & h) == 0)
        masks[1, l] = same & (~later) & ((r & h) != 0)
    return sums.reshape(2, (2 + levels) * c, c), masks


_NT = (((1,), (1,)), ((), ()))
_TN = (((0,), (0,)), ((), ()))


def _hgrn_kernel(q_ref, v_ref, lff_ref, lfb_ref, ga_ref, sums_ref, masks_ref, gn_ref, y_ref,
                 of_sc, ob_sc, *, n_ctx_chunks, n_chunks, scale):
    c = HGRN_CHUNK
    levels = masks_ref.shape[1]

    def chunk(direction, ci, lf_ref, o_sc, st):
        rows = pl.ds(pl.multiple_of(ci * c, c), c)
        lf = lf_ref[rows, :]
        e = jnp.dot(sums_ref[direction], lf, preferred_element_type=F32)
        q = q_ref[rows, :].astype(F32)
        v = v_ref[rows, :]
        k = 1.0 - jnp.exp(lf.astype(F32))
        qd = (q * jnp.exp(e[0:c])).astype(BF16)
        kd = (k * jnp.exp(e[c:2 * c])).astype(BF16)
        o = lax.dot_general(qd, st.astype(BF16), _NT, preferred_element_type=F32)
        a = jnp.zeros((c, c), F32)
        for l in range(levels):
            w = jnp.exp(e[(2 + l) * c:(3 + l) * c])
            p = lax.dot_general((q * w).astype(BF16), (k * w).astype(BF16), _NT,
                                preferred_element_type=F32)
            a = a + p * masks_ref[direction, l]
        o = o + jnp.dot(a.astype(BF16), v, preferred_element_type=F32)
        o = o + jnp.sum(q * k, axis=-1, keepdims=True) * v.astype(F32)
        o_sc[rows, :] = o
        total = (c - 1) if direction == 0 else 0
        g_all = jnp.exp(e[total:total + 1])
        return st * g_all + lax.dot_general(v, kd, _TN, preferred_element_type=F32)

    def body(j, carry):
        sf, sb = carry
        sf = chunk(0, j, lff_ref, of_sc, sf)
        cb = jnp.where(j < n_ctx_chunks, n_ctx_chunks - 1 - j, n_chunks - 1 + n_ctx_chunks - j)
        sb = chunk(1, cb, lfb_ref, ob_sc, sb)
        return sf, sb

    zero = jnp.zeros((A_DV, A_DK), F32)
    lax.fori_loop(0, n_chunks, body, (zero, zero), unroll=2)

    def readout(j, carry):
        rows = pl.ds(pl.multiple_of(j * c, c), c)
        o = (of_sc[rows, :] + ob_sc[rows, :]) * scale
        ms = jnp.mean(o * o, axis=-1, keepdims=True)
        ga = ga_ref[rows, :].astype(F32)
        y = (o * lax.rsqrt(ms + EPS) * gn_ref[...]) * (ga * _sigmoid(ga))
        y_ref[rows, :] = y.astype(y_ref.dtype)
        return carry

    lax.fori_loop(0, n_chunks, readout, 0)


def _hgrn(z, sums, masks, gn, *, layer, ctx_len, col_q, col_v, col_ff, col_fb, col_ga):
    bsz, t, _ = z.shape
    c = HGRN_CHUNK
    dk = A_DK
    kern = functools.partial(_hgrn_kernel, n_ctx_chunks=ctx_len // c, n_chunks=t // c,
                             scale=A_DK ** -0.5)

    def col(start):
        return pl.BlockSpec((None, t, dk), lambda b, h: (b, 0, start // dk + h))

    return pl.pallas_call(
        kern,
        grid=(bsz, A_HEADS),
        in_specs=[
            col(col_q), col(col_v), col(col_ff), col(col_fb), col(col_ga),
            pl.BlockSpec(sums.shape, lambda b, h: (0, 0, 0)),
            pl.BlockSpec(masks.shape, lambda b, h: (0, 0, 0, 0)),
            pl.BlockSpec((None, 1, A_DV), lambda b, h: (layer, 0, 0)),
        ],
        out_specs=pl.BlockSpec((None, t, A_DV), lambda b, h: (b, 0, h)),
        out_shape=jax.ShapeDtypeStruct((bsz, t, A_HEADS * A_DV), BF16),
        scratch_shapes=[pltpu.VMEM((t, A_DV), F32), pltpu.VMEM((t, A_DV), F32)],
        compiler_params=_params("arbitrary", "arbitrary"),
        name="hgrn",
    )(z, z, z, z, z, sums, masks, gn)


def _mix_kernel(x_ref, ya_ref, ma_ref, mb_ref, mcg_ref, u_ref, v_ref, hc_ref, gb_ref, gc_ref,
                mx_ref, mc_ref, lng_ref, lnb_ref, ws_ref, bs_ref, cw_ref,
                wpa_ref, wpb_ref, wpc_ref, wo_ref, o_ref, yb_sc, *, tm, ctx_len):
    i = pl.program_id(1)
    tile_is_ctx = i * tm < ctx_len

    vg = _gelu_tanh(v_ref[...].astype(F32))
    mu = jnp.mean(vg, axis=-1, keepdims=True)
    dv = vg - mu
    var = jnp.mean(dv * dv, axis=-1, keepdims=True)
    vn = (dv * lax.rsqrt(var + EPS) * lng_ref[...] + lnb_ref[...]).astype(BF16)
    for n in range(tm // B_CHUNK):
        rs = slice(n * B_CHUNK, (n + 1) * B_CHUNK)
        for g in range(B_GROUPS):
            cs = slice(g * B_GROUP_CH, (g + 1) * B_GROUP_CH)
            mixed = jnp.dot(ws_ref[g], vn[rs, cs], preferred_element_type=F32) + bs_ref[g]
            yb_sc[rs, cs] = (_gelu_tanh(u_ref[rs, cs].astype(F32)) * mixed).astype(BF16)

    period = jnp.where(tile_is_ctx, ctx_len, GRID_W)
    pos = (i * tm + lax.broadcasted_iota(jnp.int32, (tm, 1), 0)) % period
    tt = gc_ref[...].astype(F32) * hc_ref[...].astype(F32)
    prev = jnp.where(pos == 0, 0.0, pltpu.roll(tt, 1, axis=0))
    nxt = jnp.where(pos == period - 1, 0.0, pltpu.roll(tt, tm - 1, axis=0))
    conv = cw_ref[0:1, :] * prev + cw_ref[1:2, :] * tt + cw_ref[2:3, :] * nxt
    yc = (gb_ref[...].astype(F32) * conv).astype(BF16)

    merged = _sigmoid(ma_ref[...].astype(F32)) * jnp.dot(
        ya_ref[...], wpa_ref[...], preferred_element_type=F32)
    merged += _sigmoid(mb_ref[...].astype(F32)) * jnp.dot(
        yb_sc[...], wpb_ref[...], preferred_element_type=F32)
    merged += _sigmoid(mcg_ref[...].astype(F32)) * jnp.dot(
        yc, wpc_ref[...], preferred_element_type=F32)
    out = jnp.dot(merged.astype(BF16), wo_ref[...], preferred_element_type=F32)
    gate = jnp.where(tile_is_ctx, mc_ref[5:6, :], mx_ref[5:6, :])
    o_ref[...] = x_ref[...] + gate * out


def _mix(xa, ya, z, mods, ln_g, ln_b, ws, bs, cw, wpa, wpb, wpc, wo, *, layer, ctx_len, col_m,
         col_s):
    bsz, t, d = xa.shape
    tm = MIX_TM
    assert ctx_len % tm == 0 and tm % B_CHUNK == 0 and tm % GRID_W == 0
    bw = B_GROUPS * B_GROUP_CH
    ctx_row = bsz
    kern = functools.partial(_mix_kernel, tm=tm, ctx_len=ctx_len)

    def zcol(start, width):
        return pl.BlockSpec((None, tm, width), lambda b, i: (b, i, start // width))

    def per_layer(arr):
        zeros = (0,) * (arr.ndim - 1)
        return pl.BlockSpec((None,) + arr.shape[1:], lambda b, i: (layer,) + zeros,
                            pipeline_mode=pl.Buffered(1))

    return pl.pallas_call(
        kern,
        grid=(bsz, t // tm),
        in_specs=[
            pl.BlockSpec((None, tm, d), lambda b, i: (b, i, 0)),
            pl.BlockSpec((None, tm, ya.shape[2]), lambda b, i: (b, i, 0)),
            zcol(col_m, d), zcol(col_m + d, d), zcol(col_m + 2 * d, d),
            zcol(col_s, bw), zcol(col_s + bw, bw), zcol(col_s + 2 * bw, bw),
            zcol(col_s + 3 * bw, bw), zcol(col_s + 4 * bw, bw),
            pl.BlockSpec((None, None, N_MOD, d), lambda b, i: (layer, b, 0, 0)),
            pl.BlockSpec((None, None, N_MOD, d), lambda b, i: (layer, ctx_row, 0, 0)),
            per_layer(ln_g), per_layer(ln_b), per_layer(ws), per_layer(bs), per_layer(cw),
            per_layer(wpa), per_layer(wpb), per_layer(wpc), per_layer(wo),
        ],
        out_specs=pl.BlockSpec((None, tm, d), lambda b, i: (b, i, 0)),
        out_shape=jax.ShapeDtypeStruct(xa.shape, F32),
        scratch_shapes=[pltpu.VMEM((tm, bw), BF16)],
        compiler_params=_params("arbitrary", "arbitrary"),
        name="mix",
    )(xa, ya, z, z, z, z, z, z, z, z, mods, mods, ln_g, ln_b, ws, bs, cw, wpa, wpb, wpc, wo)


def _final_norm_kernel(x_ref, g_ref, o_ref):
    x = x_ref[...]
    ms = jnp.mean(x * x, axis=-1, keepdims=True)
    o_ref[...] = x * lax.rsqrt(ms + EPS) * g_ref[...]


def _final_norm(xa, g, *, ctx_len):
    bsz, t, d = xa.shape
    tm = NORM_TM
    assert ctx_len % tm == 0
    skip = ctx_len // tm
    return pl.pallas_call(
        _final_norm_kernel,
        grid=(bsz, (t - ctx_len) // tm),
        in_specs=[pl.BlockSpec((None, tm, d), lambda b, i: (b, i + skip, 0)),
                  pl.BlockSpec((1, d), lambda b, i: (0, 0))],
        out_specs=pl.BlockSpec((None, tm, d), lambda b, i: (b, i, 0)),
        out_shape=jax.ShapeDtypeStruct((bsz, t - ctx_len, d), F32),
        compiler_params=_params("arbitrary", "arbitrary"),
        name="final_norm",
    )(xa, g.reshape(1, d))


def _permute_w_in(w, d):
    wk = A_HEADS * A_DK
    wv = A_HEADS * A_DV
    bw = B_GROUPS * B_GROUP_CH
    sizes = (wk, wv, wk, wk, wv, bw, bw, C_WIDTH, C_WIDTH, C_WIDTH, d, d, d)
    offs = np.concatenate([[0], np.cumsum(sizes)])
    parts = [w[..., offs[k]:offs[k + 1]] for k in range(len(sizes))]
    q, iv, zf, zb, ga, u, v, hc, gb, gc, ma, mb, mc = parts
    cols = dict(q=0, v=wk, ff=wk + wv, fb=2 * wk + wv, m=3 * wk + wv,
                ga=3 * wk + wv + 3 * d, s=3 * wk + 2 * wv + 3 * d)
    return jnp.concatenate([q, iv, zf, zb, ma, mb, mc, ga, u, v, hc, gb, gc], axis=-1), cols


def kernel(x, c, ctx, c_ctx, w_mod, b_mod, norm_g, final_norm_g, ffn_w_gate, ffn_w_up,
           ffn_w_down, w_in, hgrn_lb_logits, hgrn_out_norm_g, gmlp_ln_g, gmlp_ln_b,
           gmlp_w_s, gmlp_b_s, conv_w, w_proj_a, w_proj_b, w_proj_c, w_out):
    bsz, _, d = x.shape
    depth = w_mod.shape[0]
    ctx_len = ctx.shape[1]
    assert ctx_len % ROW_BLOCK == 0
    xa = jnp.concatenate([ctx, x], axis=1)

    mod_rows = 8 * (-(-(bsz + 1) // 8))
    cvec = jnp.zeros((mod_rows, d), F32).at[:bsz].set(c).at[bsz].set(c_ctx)
    mods = _modulation(cvec, w_mod, b_mod).reshape(depth, mod_rows, N_MOD, d)

    lb = jnp.cumsum(jax.nn.softmax(hgrn_lb_logits.astype(F32), axis=0), axis=0)
    lb = (lb - lb[:1]).reshape(depth, 1, -1)
    la = jnp.log1p(-lb)
    lbl = jnp.log(lb)

    sums_np, masks_np = _hgrn_tables(HGRN_CHUNK)
    sums = jnp.asarray(sums_np, BF16)
    masks = jnp.asarray(masks_np, F32)

    wg, wu, wd = (w.astype(BF16) for w in (ffn_w_gate, ffn_w_up, ffn_w_down))
    w_in_p, cols = _permute_w_in(w_in.astype(BF16), d)
    ws, wpa, wpb, wpc, wo = (w.astype(BF16) for w in (gmlp_w_s, w_proj_a, w_proj_b, w_proj_c,
                                                      w_out))
    norm_g4 = norm_g.reshape(depth, -1, 1, d)
    gn = hgrn_out_norm_g.reshape(depth, 1, -1)
    ln_g = gmlp_ln_g.reshape(depth, 1, -1)
    ln_b = gmlp_ln_b.reshape(depth, 1, -1)
    bs = gmlp_b_s[..., None]

    for l in range(depth):
        xa = _ffn(xa, mods, norm_g4, wg, wu, wd, layer=l, slot=0, which=0, ctx_len=ctx_len)
        z = _inproj(xa, mods, norm_g4, w_in_p, la, lbl, layer=l, ctx_len=ctx_len,
                    lf_lo=cols["ff"])
        ya = _hgrn(z, sums, masks, gn, layer=l, ctx_len=ctx_len, col_q=cols["q"],
                   col_v=cols["v"], col_ff=cols["ff"], col_fb=cols["fb"], col_ga=cols["ga"])
        xa = _mix(xa, ya, z, mods, ln_g, ln_b, ws, bs, conv_w, wpa, wpb, wpc, wo, layer=l,
                  ctx_len=ctx_len, col_m=cols["m"], col_s=cols["s"])
        xa = _ffn(xa, mods, norm_g4, wg, wu, wd, layer=l, slot=2, which=1, ctx_len=ctx_len)
    return _final_norm(xa, final_norm_g, ctx_len=ctx_len)
```

```python
import functools
import math

import numpy as np
import jax
import jax.numpy as jnp
from jax import lax
from jax.experimental import pallas as pl
from jax.experimental.pallas import tpu as pltpu

F32 = jnp.float32
BF16 = jnp.bfloat16

EPS = 1e-6
N_MOD = 9
GRID_W = 64
A_HEADS = 8
A_DK = 128
A_DV = 128
B_GROUPS = 4
B_GROUP_CH = 128
B_CHUNK = 128
C_WIDTH = 512

VMEM_LIMIT_BYTES = 60 * 1024 * 1024

HGRN_CHUNK = 128
ROW_BLOCK = 16
FFN_TM = 544
FFN_TF = 512
IN_TM = 544
IN_TN = 1536
MIX_TM = 256
MOD_TN = 1024
NORM_TM = 256


def _params(*sem):
    return pltpu.CompilerParams(dimension_semantics=sem, vmem_limit_bytes=VMEM_LIMIT_BYTES)


def _sigmoid(a):
    return 0.5 * (jnp.tanh(0.5 * a) + 1.0)


def _gelu_tanh(a):
    return 0.5 * a * (1.0 + jnp.tanh(math.sqrt(2.0 / math.pi) * (a + 0.044715 * (a * a * a))))


def _row_blocks(tile, tm, ctx_len, fn, *, trips):
    def body(r, carry):
        start = pl.multiple_of(r * ROW_BLOCK, ROW_BLOCK)
        fn(pl.ds(start, ROW_BLOCK), tile * tm + start < ctx_len)
        return carry

    n = tm // ROW_BLOCK
    lax.fori_loop(0, n, body, 0, unroll=n // trips if n % trips == 0 else 1)


def _adaln_rows(x_ref, h_ref, vec_sc, g_ref, mx_ref, mc_ref, k_shift, tile, tm, ctx_len,
                zero_ref=None):
    d = x_ref.shape[-1]
    for s, m_ref in enumerate((mx_ref, mc_ref)):
        gain = g_ref[...] * (1.0 + m_ref[k_shift + 1:k_shift + 2, :])
        vec_sc[s, 0] = jnp.broadcast_to(gain, (ROW_BLOCK, d))
        vec_sc[s, 1] = jnp.broadcast_to(m_ref[k_shift:k_shift + 1, :], (ROW_BLOCK, d))

    def block(rows, is_ctx):
        s = is_ctx.astype(jnp.int32)
        x = x_ref[rows, :]
        ms = jnp.mean(x * x, axis=-1, keepdims=True)
        h = (x * lax.rsqrt(ms + EPS)) * vec_sc[s, 0] + vec_sc[s, 1]
        h_ref[rows, :] = h.astype(h_ref.dtype)
        if zero_ref is not None:
            zero_ref[rows, :] = jnp.zeros((ROW_BLOCK, d), zero_ref.dtype)

    _row_blocks(tile, tm, ctx_len, block, trips=2)


def _mod_kernel(c_ref, w_ref, b_ref, o_ref):
    c = c_ref[...]
    s = (c * _sigmoid(c)).astype(BF16)
    o_ref[...] = jnp.dot(s, w_ref[...].astype(BF16), preferred_element_type=F32) + b_ref[...]


def _modulation(cvec, w_mod, b_mod):
    depth, d, nw = w_mod.shape
    rows = cvec.shape[0]
    tn = MOD_TN
    return pl.pallas_call(
        _mod_kernel,
        grid=(depth, nw // tn),
        in_specs=[
            pl.BlockSpec((rows, d), lambda l, j: (0, 0)),
            pl.BlockSpec((None, d, tn), lambda l, j: (l, 0, j)),
            pl.BlockSpec((None, 1, tn), lambda l, j: (l, 0, j)),
        ],
        out_specs=pl.BlockSpec((None, rows, tn), lambda l, j: (l, 0, j)),
        out_shape=jax.ShapeDtypeStruct((depth, rows, nw), F32),
        compiler_params=_params("arbitrary", "arbitrary"),
        name="modulation",
    )(cvec, w_mod, b_mod.reshape(depth, 1, nw))


def _ffn_kernel(x_ref, mx_ref, mc_ref, g_ref, wg_ref, wu_ref, wd_ref, o_ref, h_sc, acc_sc, vec_sc,
                *, slot, tm, ctx_len):
    i = pl.program_id(1)
    j = pl.program_id(2)

    @pl.when(j == 0)
    def _():
        _adaln_rows(x_ref, h_sc, vec_sc, g_ref, mx_ref, mc_ref, 3 * slot, i, tm, ctx_len,
                    zero_ref=acc_sc)

    h = h_sc[...]
    a = jnp.dot(h, wg_ref[...], preferred_element_type=F32)
    u = jnp.dot(h, wu_ref[...], preferred_element_type=F32)
    act = ((a * _sigmoid(a)) * u).astype(BF16)
    acc_sc[...] += jnp.dot(act, wd_ref[...], preferred_element_type=F32)

    @pl.when(j == pl.num_programs(2) - 1)
    def _():
        k_gate = 3 * slot + 2

        def block(rows, is_ctx):
            gate = jnp.where(is_ctx, mc_ref[k_gate:k_gate + 1, :], mx_ref[k_gate:k_gate + 1, :])
            o_ref[rows, :] = x_ref[rows, :] + (0.5 * gate) * acc_sc[rows, :]

        _row_blocks(i, tm, ctx_len, block, trips=tm // (2 * ROW_BLOCK))


def _ffn(xa, mods, norm_g, wg, wu, wd, *, layer, slot, which, ctx_len):
    bsz, t, d = xa.shape
    f = wg.shape[-1]
    tm, tf = FFN_TM, FFN_TF
    ctx_row = bsz
    kern = functools.partial(_ffn_kernel, slot=slot, tm=tm, ctx_len=ctx_len)
    return pl.pallas_call(
        kern,
        grid=(bsz, t // tm, f // tf),
        in_specs=[
            pl.BlockSpec((None, tm, d), lambda b, i, j: (b, i, 0)),
            pl.BlockSpec((None, None, N_MOD, d), lambda b, i, j: (layer, b, 0, 0)),
            pl.BlockSpec((None, None, N_MOD, d), lambda b, i, j: (layer, ctx_row, 0, 0)),
            pl.BlockSpec((None, None, 1, d), lambda b, i, j: (layer, slot, 0, 0)),
            pl.BlockSpec((None, None, d, tf), lambda b, i, j: (layer, which, 0, j)),
            pl.BlockSpec((None, None, d, tf), lambda b, i, j: (layer, which, 0, j)),
            pl.BlockSpec((None, None, tf, d), lambda b, i, j: (layer, which, j, 0)),
        ],
        out_specs=pl.BlockSpec((None, tm, d), lambda b, i, j: (b, i, 0)),
        out_shape=jax.ShapeDtypeStruct(xa.shape, F32),
        scratch_shapes=[pltpu.VMEM((tm, d), BF16), pltpu.VMEM((tm, d), F32),
                        pltpu.VMEM((2, 2, ROW_BLOCK, d), F32)],
        compiler_params=_params("arbitrary", "arbitrary", "arbitrary"),
        name=f"ffn{slot}",
    )(xa, mods, mods, norm_g, wg, wu, wd)


def _log_forget(z, la, lbl):
    ls = jnp.minimum(z, 0.0) - jnp.log(1.0 + jnp.exp(-jnp.abs(z)))
    a = la + ls
    hi = jnp.maximum(a, lbl)
    return hi + jnp.log(1.0 + jnp.exp(-jnp.abs(a - lbl)))


def _inproj_kernel(x_ref, mx_ref, mc_ref, g_ref, w_ref, la_ref, lbl_ref, o_ref, h_sc, vec_sc,
                   *, tm, tn, ctx_len, lf_lo, lf_hi):
    i = pl.program_id(1)
    j = pl.program_id(2)

    @pl.when(j == 0)
    def _():
        _adaln_rows(x_ref, h_sc, vec_sc, g_ref, mx_ref, mc_ref, 3, i, tm, ctx_len)

    z = jnp.dot(h_sc[...], w_ref[...], preferred_element_type=F32)
    o_ref[...] = z.astype(o_ref.dtype)

    for jj in range(lf_lo // tn, pl.cdiv(lf_hi, tn)):
        lo = max(lf_lo, jj * tn)
        hi = min(lf_hi, (jj + 1) * tn)
        cols = slice(lo - jj * tn, hi - jj * tn)
        lfc = slice(lo - lf_lo, hi - lf_lo)

        @pl.when(j == jj)
        def _(cols=cols, lfc=lfc):
            o_ref[:, cols] = _log_forget(z[:, cols], la_ref[:, lfc], lbl_ref[:, lfc]).astype(
                o_ref.dtype)


def _inproj(xa, mods, norm_g, w, la, lbl, *, layer, ctx_len, lf_lo):
    bsz, t, d = xa.shape
    nw = w.shape[-1]
    tm, tn = IN_TM, IN_TN
    lf_w = la.shape[-1]
    ctx_row = bsz
    kern = functools.partial(_inproj_kernel, tm=tm, tn=tn, ctx_len=ctx_len,
                             lf_lo=lf_lo, lf_hi=lf_lo + lf_w)
    return pl.pallas_call(
        kern,
        grid=(bsz, t // tm, nw // tn),
        in_specs=[
            pl.BlockSpec((None, tm, d), lambda b, i, j: (b, i, 0)),
            pl.BlockSpec((None, None, N_MOD, d), lambda b, i, j: (layer, b, 0, 0)),
            pl.BlockSpec((None, None, N_MOD, d), lambda b, i, j: (layer, ctx_row, 0, 0)),
            pl.BlockSpec((None, None, 1, d), lambda b, i, j: (layer, 1, 0, 0)),
            pl.BlockSpec((None, d, tn), lambda b, i, j: (layer, 0, j)),
            pl.BlockSpec((None, 1, lf_w), lambda b, i, j: (layer, 0, 0)),
            pl.BlockSpec((None, 1, lf_w), lambda b, i, j: (layer, 0, 0)),
        ],
        out_specs=pl.BlockSpec((None, tm, tn), lambda b, i, j: (b, i, j)),
        out_shape=jax.ShapeDtypeStruct((bsz, t, nw), BF16),
        scratch_shapes=[pltpu.VMEM((tm, d), BF16), pltpu.VMEM((2, 2, ROW_BLOCK, d), F32)],
        compiler_params=_params("arbitrary", "arbitrary", "arbitrary"),
        name="inproj",
    )(xa, mods, mods, norm_g, w, la, lbl)


def _hgrn_tables(c):
    levels = int(math.log2(c))
    t = np.arange(c)[:, None]
    r = np.arange(c)[None, :]
    sums = np.zeros((2, 2 + levels, c, c), np.float32)
    masks = np.zeros((2, levels + 1, c, c), np.float32)
    masks[:, levels] = np.eye(c)
    sums[0, 0] = r <= t
    sums[0, 1] = r > t
    sums[1, 0] = r >= t
    sums[1, 1] = r < t
    for l in range(levels):
        h = 1 << l
        mid = (t // (2 * h)) * 2 * h + h - 1
        later = (t & h) != 0
        sums[0, 2 + l] = np.where(later, (r > mid) & (r <= t), (r > t) & (r <= mid))
        sums[1, 2 + l] = np.where(later, (r > mid) & (r < t), (r >= t) & (r <= mid))
        same = (t // (2 * h)) == (r // (2 * h))
        masks[0, l] = same & later & ((r & h) == 0)
        masks[1, l] = same & (~later) & ((r & h) != 0)
    return sums.reshape(2, (2 + levels) * c, c), masks


_NT = (((1,), (1,)), ((), ()))
_TN = (((0,), (0,)), ((), ()))


def _hgrn_kernel(q_ref, v_ref, lff_ref, lfb_ref, ga_ref, sums_ref, masks_ref, gn_ref, y_ref,
                 of_sc, ob_sc, e_sc, a_sc, qd_sc, kd_sc, g_sc, *, n_ctx_chunks, n_chunks, scale):
    c = HGRN_CHUNK
    levels = masks_ref.shape[1] - 1
    lf_refs = (lff_ref, lfb_ref)
    o_scs = (of_sc, ob_sc)

    def rows_of(d, p):
        if d == 0:
            ci = p
        else:
            ci = jnp.where(p < n_ctx_chunks, n_ctx_chunks - 1 - p, n_chunks - 1 + n_ctx_chunks - p)
        return pl.ds(pl.multiple_of(ci * c, c), c)

    def stage_sums(p):
        for d in range(2):
            lf = lf_refs[d][rows_of(d, p), :]
            e_sc[d, p % 2] = jnp.dot(sums_ref[d], lf, preferred_element_type=F32)

    def stage_scores(p):
        slot = p % 2
        for d in range(2):
            rows = rows_of(d, p)
            qb = q_ref[rows, :]
            kb = (1.0 - jnp.exp(lf_refs[d][rows, :].astype(F32))).astype(BF16)
            qd_sc[d, slot] = qb * jnp.exp(e_sc[d, slot, 0:c, :]).astype(BF16)
            kd_sc[d, slot] = kb * jnp.exp(e_sc[d, slot, c:2 * c, :]).astype(BF16)
            a = lax.dot_general(qb, kb, _NT, preferred_element_type=F32).astype(BF16)
            a = a * masks_ref[d, levels]
            for l in range(levels):
                w = jnp.exp(e_sc[d, slot, (2 + l) * c:(3 + l) * c, :]).astype(BF16)
                p_l = lax.dot_general(qb * w, kb * w, _NT, preferred_element_type=F32)
                a = a + p_l.astype(BF16) * masks_ref[d, l]
            a_sc[d, slot] = a
            total = (c - 1) if d == 0 else 0
            g_sc[d, slot] = jnp.exp(e_sc[d, slot, total:total + 1, :])

    def stage_state(p, states):
        slot = p % 2
        new = []
        for d in range(2):
            rows = rows_of(d, p)
            v = v_ref[rows, :]
            st = states[d]
            o = lax.dot_general(qd_sc[d, slot], st.astype(BF16), _NT, preferred_element_type=F32)
            o_scs[d][rows, :] = o + jnp.dot(a_sc[d, slot], v, preferred_element_type=F32)
            new.append(st * g_sc[d, slot] + lax.dot_general(v, kd_sc[d, slot], _TN,
                                                             preferred_element_type=F32))
        return tuple(new)

    def body(j, states):
        states = stage_state(j - 2, states)
        stage_scores(j - 1)
        stage_sums(j)
        return states

    zero = jnp.zeros((A_DV, A_DK), F32)
    stage_sums(0)
    stage_scores(0)
    stage_sums(1)
    states = lax.fori_loop(2, n_chunks, body, (zero, zero))
    states = stage_state(n_chunks - 2, states)
    stage_scores(n_chunks - 1)
    stage_state(n_chunks - 1, states)

    def readout(j, carry):
        rows = pl.ds(pl.multiple_of(j * c, c), c)
        o = (of_sc[rows, :] + ob_sc[rows, :]) * scale
        ms = jnp.mean(o * o, axis=-1, keepdims=True)
        ga = ga_ref[rows, :].astype(F32)
        y = (o * lax.rsqrt(ms + EPS) * gn_ref[...]) * (ga * _sigmoid(ga))
        y_ref[rows, :] = y.astype(y_ref.dtype)
        return carry

    lax.fori_loop(0, n_chunks, readout, 0)


def _hgrn(z, sums, masks, gn, *, layer, ctx_len, col_q, col_v, col_ff, col_fb, col_ga):
    bsz, t, _ = z.shape
    c = HGRN_CHUNK
    dk = A_DK
    assert t // c >= 2
    kern = functools.partial(_hgrn_kernel, n_ctx_chunks=ctx_len // c, n_chunks=t // c,
                             scale=A_DK ** -0.5)

    def col(start):
        return pl.BlockSpec((None, t, dk), lambda b, h: (b, 0, start // dk + h))

    return pl.pallas_call(
        kern,
        grid=(bsz, A_HEADS),
        in_specs=[
            col(col_q), col(col_v), col(col_ff), col(col_fb), col(col_ga),
            pl.BlockSpec(sums.shape, lambda b, h: (0, 0, 0)),
            pl.BlockSpec(masks.shape, lambda b, h: (0, 0, 0, 0)),
            pl.BlockSpec((None, 1, A_DV), lambda b, h: (layer, 0, 0)),
        ],
        out_specs=pl.BlockSpec((None, t, A_DV), lambda b, h: (b, 0, h)),
        out_shape=jax.ShapeDtypeStruct((bsz, t, A_HEADS * A_DV), BF16),
        scratch_shapes=[
            pltpu.VMEM((t, A_DV), F32), pltpu.VMEM((t, A_DV), F32),
            pltpu.VMEM((2, 2) + sums.shape[1:], F32),
            pltpu.VMEM((2, 2, c, c), BF16),
            pltpu.VMEM((2, 2, c, dk), BF16),
            pltpu.VMEM((2, 2, c, dk), BF16),
            pltpu.VMEM((2, 2, 1, dk), F32),
        ],
        compiler_params=_params("arbitrary", "arbitrary"),
        name="hgrn",
    )(z, z, z, z, z, sums, masks, gn)


def _mix_kernel(x_ref, ya_ref, ma_ref, mb_ref, mcg_ref, u_ref, v_ref, hc_ref, gb_ref, gc_ref,
                mx_ref, mc_ref, lng_ref, lnb_ref, ws_ref, bs_ref, cw_ref,
                wpa_ref, wpb_ref, wpc_ref, wo_ref, o_ref, yb_sc, *, tm, ctx_len):
    i = pl.program_id(1)
    tile_is_ctx = i * tm < ctx_len

    vg = _gelu_tanh(v_ref[...].astype(F32))
    mu = jnp.mean(vg, axis=-1, keepdims=True)
    dv = vg - mu
    var = jnp.mean(dv * dv, axis=-1, keepdims=True)
    vn = (dv * lax.rsqrt(var + EPS) * lng_ref[...] + lnb_ref[...]).astype(BF16)
    for n in range(tm // B_CHUNK):
        rs = slice(n * B_CHUNK, (n + 1) * B_CHUNK)
        for g in range(B_GROUPS):
            cs = slice(g * B_GROUP_CH, (g + 1) * B_GROUP_CH)
            mixed = jnp.dot(ws_ref[g], vn[rs, cs], preferred_element_type=F32) + bs_ref[g]
            yb_sc[rs, cs] = (_gelu_tanh(u_ref[rs, cs].astype(F32)) * mixed).astype(BF16)

    period = jnp.where(tile_is_ctx, ctx_len, GRID_W)
    pos = (i * tm + lax.broadcasted_iota(jnp.int32, (tm, 1), 0)) % period
    tt = gc_ref[...].astype(F32) * hc_ref[...].astype(F32)
    prev = jnp.where(pos == 0, 0.0, pltpu.roll(tt, 1, axis=0))
    nxt = jnp.where(pos == period - 1, 0.0, pltpu.roll(tt, tm - 1, axis=0))
    conv = cw_ref[0:1, :] * prev + cw_ref[1:2, :] * tt + cw_ref[2:3, :] * nxt
    yc = (gb_ref[...].astype(F32) * conv).astype(BF16)

    def gated2(m_ref, p):
        return p + jnp.tanh((0.5 * m_ref[...]).astype(F32)) * p

    merged2 = gated2(ma_ref, jnp.dot(ya_ref[...], wpa_ref[...], preferred_element_type=F32))
    merged2 += gated2(mb_ref, jnp.dot(yb_sc[...], wpb_ref[...], preferred_element_type=F32))
    merged2 += gated2(mcg_ref, jnp.dot(yc, wpc_ref[...], preferred_element_type=F32))
    out2 = jnp.dot(merged2.astype(BF16), wo_ref[...], preferred_element_type=F32)
    gate = jnp.where(tile_is_ctx, mc_ref[5:6, :], mx_ref[5:6, :])
    o_ref[...] = x_ref[...] + (0.5 * gate) * out2


def _mix(xa, ya, z, mods, ln_g, ln_b, ws, bs, cw, wpa, wpb, wpc, wo, *, layer, ctx_len, col_m,
         col_s):
    bsz, t, d = xa.shape
    tm = MIX_TM
    assert ctx_len % tm == 0 and tm % B_CHUNK == 0 and tm % GRID_W == 0
    bw = B_GROUPS * B_GROUP_CH
    ctx_row = bsz
    kern = functools.partial(_mix_kernel, tm=tm, ctx_len=ctx_len)

    def zcol(start, width):
        return pl.BlockSpec((None, tm, width), lambda b, i: (b, i, start // width))

    def per_layer(arr):
        zeros = (0,) * (arr.ndim - 1)
        return pl.BlockSpec((None,) + arr.shape[1:], lambda b, i: (layer,) + zeros,
                            pipeline_mode=pl.Buffered(1))

    return pl.pallas_call(
        kern,
        grid=(bsz, t // tm),
        in_specs=[
            pl.BlockSpec((None, tm, d), lambda b, i: (b, i, 0)),
            pl.BlockSpec((None, tm, ya.shape[2]), lambda b, i: (b, i, 0)),
            zcol(col_m, d), zcol(col_m + d, d), zcol(col_m + 2 * d, d),
            zcol(col_s, bw), zcol(col_s + bw, bw), zcol(col_s + 2 * bw, bw),
            zcol(col_s + 3 * bw, bw), zcol(col_s + 4 * bw, bw),
            pl.BlockSpec((None, None, N_MOD, d), lambda b, i: (layer, b, 0, 0)),
            pl.BlockSpec((None, None, N_MOD, d), lambda b, i: (layer, ctx_row, 0, 0)),
            per_layer(ln_g), per_layer(ln_b), per_layer(ws), per_layer(bs), per_layer(cw),
            per_layer(wpa), per_layer(wpb), per_layer(wpc), per_layer(wo),
        ],
        out_specs=pl.BlockSpec((None, tm, d), lambda b, i: (b, i, 0)),
        out_shape=jax.ShapeDtypeStruct(xa.shape, F32),
        scratch_shapes=[pltpu.VMEM((tm, bw), BF16)],
        compiler_params=_params("arbitrary", "arbitrary"),
        name="mix",
    )(xa, ya, z, z, z, z, z, z, z, z, mods, mods, ln_g, ln_b, ws, bs, cw, wpa, wpb, wpc, wo)


def _final_norm_kernel(x_ref, g_ref, o_ref):
    x = x_ref[...]
    ms = jnp.mean(x * x, axis=-1, keepdims=True)
    o_ref[...] = x * lax.rsqrt(ms + EPS) * g_ref[...]


def _final_norm(xa, g, *, ctx_len):
    bsz, t, d = xa.shape
    tm = NORM_TM
    assert ctx_len % tm == 0
    skip = ctx_len // tm
    return pl.pallas_call(
        _final_norm_kernel,
        grid=(bsz, (t - ctx_len) // tm),
        in_specs=[pl.BlockSpec((None, tm, d), lambda b, i: (b, i + skip, 0)),
                  pl.BlockSpec((1, d), lambda b, i: (0, 0))],
        out_specs=pl.BlockSpec((None, tm, d), lambda b, i: (b, i, 0)),
        out_shape=jax.ShapeDtypeStruct((bsz, t - ctx_len, d), F32),
        compiler_params=_params("arbitrary", "arbitrary"),
        name="final_norm",
    )(xa, g.reshape(1, d))


def _permute_w_in(w, d):
    wk = A_HEADS * A_DK
    wv = A_HEADS * A_DV
    bw = B_GROUPS * B_GROUP_CH
    sizes = (wk, wv, wk, wk, wv, bw, bw, C_WIDTH, C_WIDTH, C_WIDTH, d, d, d)
    offs = np.concatenate([[0], np.cumsum(sizes)])
    parts = [w[..., offs[k]:offs[k + 1]] for k in range(len(sizes))]
    q, iv, zf, zb, ga, u, v, hc, gb, gc, ma, mb, mc = parts
    cols = dict(q=0, v=wk, ff=wk + wv, fb=2 * wk + wv, m=3 * wk + wv,
                ga=3 * wk + wv + 3 * d, s=3 * wk + 2 * wv + 3 * d)
    return jnp.concatenate([q, iv, zf, zb, ma, mb, mc, ga, u, v, hc, gb, gc], axis=-1), cols


def kernel(x, c, ctx, c_ctx, w_mod, b_mod, norm_g, final_norm_g, ffn_w_gate, ffn_w_up,
           ffn_w_down, w_in, hgrn_lb_logits, hgrn_out_norm_g, gmlp_ln_g, gmlp_ln_b,
           gmlp_w_s, gmlp_b_s, conv_w, w_proj_a, w_proj_b, w_proj_c, w_out):
    bsz, _, d = x.shape
    depth = w_mod.shape[0]
    ctx_len = ctx.shape[1]
    assert ctx_len % ROW_BLOCK == 0
    xa = jnp.concatenate([ctx, x], axis=1)

    mod_rows = 8 * (-(-(bsz + 1) // 8))
    cvec = jnp.zeros((mod_rows, d), F32).at[:bsz].set(c).at[bsz].set(c_ctx)
    mods = _modulation(cvec, w_mod, b_mod).reshape(depth, mod_rows, N_MOD, d)

    lb = jnp.cumsum(jax.nn.softmax(hgrn_lb_logits.astype(F32), axis=0), axis=0)
    lb = (lb - lb[:1]).reshape(depth, 1, -1)
    la = jnp.log1p(-lb)
    lbl = jnp.log(lb)

    sums_np, masks_np = _hgrn_tables(HGRN_CHUNK)
    sums = jnp.asarray(sums_np, BF16)
    masks = jnp.asarray(masks_np, BF16)

    wg, wu, wd = (w.astype(BF16) for w in (ffn_w_gate, ffn_w_up, ffn_w_down))
    w_in_p, cols = _permute_w_in(w_in.astype(BF16), d)
    ws, wpa, wpb, wpc, wo = (w.astype(BF16) for w in (gmlp_w_s, w_proj_a, w_proj_b, w_proj_c,
                                                      w_out))
    norm_g4 = norm_g.reshape(depth, -1, 1, d)
    gn = hgrn_out_norm_g.reshape(depth, 1, -1)
    ln_g = gmlp_ln_g.reshape(depth, 1, -1)
    ln_b = gmlp_ln_b.reshape(depth, 1, -1)
    bs = gmlp_b_s[..., None]

    for l in range(depth):
        xa = _ffn(xa, mods, norm_g4, wg, wu, wd, layer=l, slot=0, which=0, ctx_len=ctx_len)
        z = _inproj(xa, mods, norm_g4, w_in_p, la, lbl, layer=l, ctx_len=ctx_len,
                    lf_lo=cols["ff"])
        ya = _hgrn(z, sums, masks, gn, layer=l, ctx_len=ctx_len, col_q=cols["q"],
                   col_v=cols["v"], col_ff=cols["ff"], col_fb=cols["fb"], col_ga=cols["ga"])
        xa = _mix(xa, ya, z, mods, ln_g, ln_b, ws, bs, conv_w, wpa, wpb, wpc, wo, layer=l,
                  ctx_len=ctx_len, col_m=cols["m"], col_s=cols["s"])
        xa = _ffn(xa, mods, norm_g4, wg, wu, wd, layer=l, slot=2, which=1, ctx_len=ctx_len)
    return _final_norm(xa, final_norm_g, ctx_len=ctx_len)
```

```python
import functools
import math

import numpy as np
import jax
import jax.numpy as jnp
from jax import lax
from jax.experimental import pallas as pl
from jax.experimental.pallas import tpu as pltpu

F32 = jnp.float32
BF16 = jnp.bfloat16

EPS = 1e-6
N_MOD = 9
GRID_W = 64
A_HEADS = 8
A_DK = 128
A_DV = 128
B_GROUPS = 4
B_GROUP_CH = 128
B_CHUNK = 128
C_WIDTH = 512

VMEM_LIMIT_BYTES = 60 * 1024 * 1024

HGRN_CHUNK = 128
ROW_BLOCK = 16
FFN_TM = 544
FFN_TF = 512
IN_TM = 544
IN_TN = 1536
MIX_TM = 256
MOD_TN = 1024
NORM_TM = 256


def _params(*sem):
    return pltpu.CompilerParams(dimension_semantics=sem, vmem_limit_bytes=VMEM_LIMIT_BYTES)


def _sigmoid(a):
    return 0.5 * (jnp.tanh(0.5 * a) + 1.0)


def _gelu_tanh(a):
    return 0.5 * a * (1.0 + jnp.tanh(math.sqrt(2.0 / math.pi) * (a + 0.044715 * (a * a * a))))


def _row_blocks(tile, tm, ctx_len, fn, *, trips):
    def body(r, carry):
        start = pl.multiple_of(r * ROW_BLOCK, ROW_BLOCK)
        fn(pl.ds(start, ROW_BLOCK), tile * tm + start < ctx_len)
        return carry

    n = tm // ROW_BLOCK
    lax.fori_loop(0, n, body, 0, unroll=n // trips if n % trips == 0 else 1)


def _adaln_rows(x_ref, h_ref, vec_sc, g_ref, mx_ref, mc_ref, k_shift, tile, tm, ctx_len,
                zero_ref=None):
    d = x_ref.shape[-1]
    for s, m_ref in enumerate((mx_ref, mc_ref)):
        gain = g_ref[...] * (1.0 + m_ref[k_shift + 1:k_shift + 2, :])
        vec_sc[s, 0] = jnp.broadcast_to(gain, (ROW_BLOCK, d))
        vec_sc[s, 1] = jnp.broadcast_to(m_ref[k_shift:k_shift + 1, :], (ROW_BLOCK, d))

    def block(rows, is_ctx):
        s = is_ctx.astype(jnp.int32)
        x = x_ref[rows, :]
        ms = jnp.mean(x * x, axis=-1, keepdims=True)
        h = (x * lax.rsqrt(ms + EPS)) * vec_sc[s, 0] + vec_sc[s, 1]
        h_ref[rows, :] = h.astype(h_ref.dtype)
        if zero_ref is not None:
            zero_ref[rows, :] = jnp.zeros((ROW_BLOCK, d), zero_ref.dtype)

    _row_blocks(tile, tm, ctx_len, block, trips=2)


def _mod_kernel(c_ref, w_ref, b_ref, o_ref):
    c = c_ref[...]
    s = (c * _sigmoid(c)).astype(BF16)
    o_ref[...] = jnp.dot(s, w_ref[...].astype(BF16), preferred_element_type=F32) + b_ref[...]


def _modulation(cvec, w_mod, b_mod):
    depth, d, nw = w_mod.shape
    rows = cvec.shape[0]
    tn = MOD_TN
    return pl.pallas_call(
        _mod_kernel,
        grid=(depth, nw // tn),
        in_specs=[
            pl.BlockSpec((rows, d), lambda l, j: (0, 0)),
            pl.BlockSpec((None, d, tn), lambda l, j: (l, 0, j)),
            pl.BlockSpec((None, 1, tn), lambda l, j: (l, 0, j)),
        ],
        out_specs=pl.BlockSpec((None, rows, tn), lambda l, j: (l, 0, j)),
        out_shape=jax.ShapeDtypeStruct((depth, rows, nw), F32),
        compiler_params=_params("arbitrary", "arbitrary"),
        name="modulation",
    )(cvec, w_mod, b_mod.reshape(depth, 1, nw))


def _ffn_kernel(x_ref, mx_ref, mc_ref, g_ref, wg_ref, wu_ref, wd_ref, o_ref, h_sc, acc_sc, vec_sc,
                *, slot, tm, ctx_len):
    i = pl.program_id(1)
    j = pl.program_id(2)

    @pl.when(j == 0)
    def _():
        _adaln_rows(x_ref, h_sc, vec_sc, g_ref, mx_ref, mc_ref, 3 * slot, i, tm, ctx_len,
                    zero_ref=acc_sc)

    h = h_sc[...]
    a = jnp.dot(h, wg_ref[...], preferred_element_type=F32)
    u = jnp.dot(h, wu_ref[...], preferred_element_type=F32)
    act = ((a * _sigmoid(a)) * u).astype(BF16)
    acc_sc[...] += jnp.dot(act, wd_ref[...], preferred_element_type=F32)

    @pl.when(j == pl.num_programs(2) - 1)
    def _():
        k_gate = 3 * slot + 2

        def block(rows, is_ctx):
            gate = jnp.where(is_ctx, mc_ref[k_gate:k_gate + 1, :], mx_ref[k_gate:k_gate + 1, :])
            o_ref[rows, :] = x_ref[rows, :] + (0.5 * gate) * acc_sc[rows, :]

        _row_blocks(i, tm, ctx_len, block, trips=tm // (2 * ROW_BLOCK))


def _ffn(xa, mods, norm_g, wg, wu, wd, *, layer, slot, which, ctx_len):
    bsz, t, d = xa.shape
    f = wg.shape[-1]
    tm, tf = FFN_TM, FFN_TF
    ctx_row = bsz
    kern = functools.partial(_ffn_kernel, slot=slot, tm=tm, ctx_len=ctx_len)
    return pl.pallas_call(
        kern,
        grid=(bsz, t // tm, f // tf),
        in_specs=[
            pl.BlockSpec((None, tm, d), lambda b, i, j: (b, i, 0)),
            pl.BlockSpec((None, None, N_MOD, d), lambda b, i, j: (layer, b, 0, 0)),
            pl.BlockSpec((None, None, N_MOD, d), lambda b, i, j: (layer, ctx_row, 0, 0)),
            pl.BlockSpec((None, None, 1, d), lambda b, i, j: (layer, slot, 0, 0)),
            pl.BlockSpec((None, None, d, tf), lambda b, i, j: (layer, which, 0, j)),
            pl.BlockSpec((None, None, d, tf), lambda b, i, j: (layer, which, 0, j)),
            pl.BlockSpec((None, None, tf, d), lambda b, i, j: (layer, which, j, 0)),
        ],
        out_specs=pl.BlockSpec((None, tm, d), lambda b, i, j: (b, i, 0)),
        out_shape=jax.ShapeDtypeStruct(xa.shape, F32),
        scratch_shapes=[pltpu.VMEM((tm, d), BF16), pltpu.VMEM((tm, d), F32),
                        pltpu.VMEM((2, 2, ROW_BLOCK, d), F32)],
        compiler_params=_params("arbitrary", "arbitrary", "arbitrary"),
        name=f"ffn{slot}",
    )(xa, mods, mods, norm_g, wg, wu, wd)


LOG2_E = 1.4426950408889634


def _log2_forget(z, la, lbl):
    ls = jnp.minimum(z, 0.0) - jnp.log(1.0 + jnp.exp(-jnp.abs(z)))
    a = la + ls
    hi = jnp.maximum(a, lbl)
    return (hi + jnp.log(1.0 + jnp.exp(-jnp.abs(a - lbl)))) * LOG2_E


def _inproj_kernel(x_ref, mx_ref, mc_ref, g_ref, w_ref, la_ref, lbl_ref, o_ref, h_sc, vec_sc,
                   *, tm, tn, ctx_len, lf_lo, lf_hi):
    i = pl.program_id(1)
    j = pl.program_id(2)

    @pl.when(j == 0)
    def _():
        _adaln_rows(x_ref, h_sc, vec_sc, g_ref, mx_ref, mc_ref, 3, i, tm, ctx_len)

    z = jnp.dot(h_sc[...], w_ref[...], preferred_element_type=F32)
    o_ref[...] = z.astype(o_ref.dtype)

    for jj in range(lf_lo // tn, pl.cdiv(lf_hi, tn)):
        lo = max(lf_lo, jj * tn)
        hi = min(lf_hi, (jj + 1) * tn)
        cols = slice(lo - jj * tn, hi - jj * tn)
        lfc = slice(lo - lf_lo, hi - lf_lo)

        @pl.when(j == jj)
        def _(cols=cols, lfc=lfc):
            o_ref[:, cols] = _log2_forget(z[:, cols], la_ref[:, lfc], lbl_ref[:, lfc]).astype(
                o_ref.dtype)


def _inproj(xa, mods, norm_g, w, la, lbl, *, layer, ctx_len, lf_lo):
    bsz, t, d = xa.shape
    nw = w.shape[-1]
    tm, tn = IN_TM, IN_TN
    lf_w = la.shape[-1]
    ctx_row = bsz
    kern = functools.partial(_inproj_kernel, tm=tm, tn=tn, ctx_len=ctx_len,
                             lf_lo=lf_lo, lf_hi=lf_lo + lf_w)
    return pl.pallas_call(
        kern,
        grid=(bsz, t // tm, nw // tn),
        in_specs=[
            pl.BlockSpec((None, tm, d), lambda b, i, j: (b, i, 0)),
            pl.BlockSpec((None, None, N_MOD, d), lambda b, i, j: (layer, b, 0, 0)),
            pl.BlockSpec((None, None, N_MOD, d), lambda b, i, j: (layer, ctx_row, 0, 0)),
            pl.BlockSpec((None, None, 1, d), lambda b, i, j: (layer, 1, 0, 0)),
            pl.BlockSpec((None, d, tn), lambda b, i, j: (layer, 0, j)),
            pl.BlockSpec((None, 1, lf_w), lambda b, i, j: (layer, 0, 0)),
            pl.BlockSpec((None, 1, lf_w), lambda b, i, j: (layer, 0, 0)),
        ],
        out_specs=pl.BlockSpec((None, tm, tn), lambda b, i, j: (b, i, j)),
        out_shape=jax.ShapeDtypeStruct((bsz, t, nw), BF16),
        scratch_shapes=[pltpu.VMEM((tm, d), BF16), pltpu.VMEM((2, 2, ROW_BLOCK, d), F32)],
        compiler_params=_params("arbitrary", "arbitrary", "arbitrary"),
        name="inproj",
    )(xa, mods, mods, norm_g, w, la, lbl)


def _hgrn_tables(c):
    levels = int(math.log2(c))
    t = np.arange(c)[:, None]
    r = np.arange(c)[None, :]
    sums = np.zeros((2, 2 + levels, c, c), np.float32)
    masks = np.zeros((2, levels + 1, c, c), np.float32)
    masks[:, levels] = np.eye(c)
    sums[0, 0] = r <= t
    sums[0, 1] = r > t
    sums[1, 0] = r >= t
    sums[1, 1] = r < t
    for l in range(levels):
        h = 1 << l
        mid = (t // (2 * h)) * 2 * h + h - 1
        later = (t & h) != 0
        sums[0, 2 + l] = np.where(later, (r > mid) & (r <= t), (r > t) & (r <= mid))
        sums[1, 2 + l] = np.where(later, (r > mid) & (r < t), (r >= t) & (r <= mid))
        same = (t // (2 * h)) == (r // (2 * h))
        masks[0, l] = same & later & ((r & h) == 0)
        masks[1, l] = same & (~later) & ((r & h) != 0)
    return sums.reshape(2, (2 + levels) * c, c), masks


_NT = (((1,), (1,)), ((), ()))
_TN = (((0,), (0,)), ((), ()))


def _hgrn_kernel(q_ref, v_ref, lff_ref, lfb_ref, ga_ref, sums_ref, masks_ref, gn_ref, y_ref,
                 of_sc, ob_sc, e_sc, a_sc, qd_sc, kd_sc, g_sc, *, n_ctx_chunks, n_chunks, scale):
    c = HGRN_CHUNK
    levels = masks_ref.shape[1] - 1
    lf_refs = (lff_ref, lfb_ref)
    o_scs = (of_sc, ob_sc)

    def rows_of(d, p):
        if d == 0:
            ci = p
        else:
            ci = jnp.where(p < n_ctx_chunks, n_ctx_chunks - 1 - p, n_chunks - 1 + n_ctx_chunks - p)
        return pl.ds(pl.multiple_of(ci * c, c), c)

    def stage_sums(p, slot):
        for d in range(2):
            lf = lf_refs[d][rows_of(d, p), :]
            e_sc[d, slot] = jnp.dot(sums_ref[d], lf, preferred_element_type=F32)

    def stage_scores(p, slot):
        for d in range(2):
            rows = rows_of(d, p)
            qb = q_ref[rows, :]
            kb = (1.0 - jnp.exp2(lf_refs[d][rows, :].astype(F32))).astype(BF16)
            qd_sc[d, slot] = qb * jnp.exp2(e_sc[d, slot, 0:c, :]).astype(BF16)
            kd_sc[d, slot] = kb * jnp.exp2(e_sc[d, slot, c:2 * c, :]).astype(BF16)
            kbt = kb.T
            a = jnp.dot(qb, kbt, preferred_element_type=F32).astype(BF16)
            a = a * masks_ref[d, levels]
            for l in range(levels):
                w = jnp.exp2(e_sc[d, slot, (2 + l) * c:(3 + l) * c, :]).astype(BF16)
                p_l = jnp.dot(qb * w, kbt * w.T, preferred_element_type=F32)
                a = a + p_l.astype(BF16) * masks_ref[d, l]
            a_sc[d, slot] = a
            total = (c - 1) if d == 0 else 0
            g_sc[d, slot] = jnp.exp2(e_sc[d, slot, total:total + 1, :])

    def stage_state(p, slot, states):
        new = []
        for d in range(2):
            rows = rows_of(d, p)
            v = v_ref[rows, :]
            st = states[d]
            o = lax.dot_general(qd_sc[d, slot], st.astype(BF16), _NT, preferred_element_type=F32)
            o_scs[d][rows, :] = o + jnp.dot(a_sc[d, slot], v, preferred_element_type=F32)
            new.append(st * g_sc[d, slot] + lax.dot_general(v, kd_sc[d, slot], _TN,
                                                             preferred_element_type=F32))
        return tuple(new)

    def step(j, par, states):
        stage_sums(j, par)
        states = stage_state(j - 2, par, states)
        stage_scores(j - 1, 1 - par)
        return states

    def body(m, states):
        j = 2 + 2 * m
        return step(j + 1, 1, step(j, 0, states))

    assert n_chunks % 2 == 0
    last = n_chunks - 1
    zero = jnp.zeros((A_DV, A_DK), F32)
    stage_sums(0, 0)
    stage_scores(0, 0)
    stage_sums(1, 1)
    states = lax.fori_loop(0, (n_chunks - 2) // 2, body, (zero, zero))
    states = stage_state(last - 1, (last - 1) % 2, states)
    stage_scores(last, last % 2)
    stage_state(last, last % 2, states)

    def readout(j, carry):
        rows = pl.ds(pl.multiple_of(j * c, c), c)
        o = (of_sc[rows, :] + ob_sc[rows, :]) * scale
        ms = jnp.mean(o * o, axis=-1, keepdims=True)
        ga = ga_ref[rows, :].astype(F32)
        y = (o * lax.rsqrt(ms + EPS) * gn_ref[...]) * (ga * _sigmoid(ga))
        y_ref[rows, :] = y.astype(y_ref.dtype)
        return carry

    lax.fori_loop(0, n_chunks, readout, 0, unroll=2)


def _hgrn(z, sums, masks, gn, *, layer, ctx_len, col_q, col_v, col_ff, col_fb, col_ga):
    bsz, t, _ = z.shape
    c = HGRN_CHUNK
    dk = A_DK
    assert t // c >= 2
    kern = functools.partial(_hgrn_kernel, n_ctx_chunks=ctx_len // c, n_chunks=t // c,
                             scale=A_DK ** -0.5)

    def col(start):
        return pl.BlockSpec((None, t, dk), lambda b, h: (b, 0, start // dk + h))

    return pl.pallas_call(
        kern,
        grid=(bsz, A_HEADS),
        in_specs=[
            col(col_q), col(col_v), col(col_ff), col(col_fb), col(col_ga),
            pl.BlockSpec(sums.shape, lambda b, h: (0, 0, 0)),
            pl.BlockSpec(masks.shape, lambda b, h: (0, 0, 0, 0)),
            pl.BlockSpec((None, 1, A_DV), lambda b, h: (layer, 0, 0)),
        ],
        out_specs=pl.BlockSpec((None, t, A_DV), lambda b, h: (b, 0, h)),
        out_shape=jax.ShapeDtypeStruct((bsz, t, A_HEADS * A_DV), BF16),
        scratch_shapes=[
            pltpu.VMEM((t, A_DV), F32), pltpu.VMEM((t, A_DV), F32),
            pltpu.VMEM((2, 2) + sums.shape[1:], F32),
            pltpu.VMEM((2, 2, c, c), BF16),
            pltpu.VMEM((2, 2, c, dk), BF16),
            pltpu.VMEM((2, 2, c, dk), BF16),
            pltpu.VMEM((2, 2, 1, dk), F32),
        ],
        compiler_params=_params("arbitrary", "arbitrary"),
        name="hgrn",
    )(z, z, z, z, z, sums, masks, gn)


def _mix_kernel(x_ref, ya_ref, ma_ref, mb_ref, mcg_ref, u_ref, v_ref, hc_ref, gb_ref, gc_ref,
                mx_ref, mc_ref, lng_ref, lnb_ref, ws_ref, bs_ref, cw_ref,
                wpa_ref, wpb_ref, wpc_ref, wo_ref, o_ref, yb_sc, *, tm, ctx_len):
    i = pl.program_id(1)
    tile_is_ctx = i * tm < ctx_len

    vg = _gelu_tanh(v_ref[...].astype(F32))
    mu = jnp.mean(vg, axis=-1, keepdims=True)
    dv = vg - mu
    var = jnp.mean(dv * dv, axis=-1, keepdims=True)
    vn = (dv * lax.rsqrt(var + EPS) * lng_ref[...] + lnb_ref[...]).astype(BF16)
    for n in range(tm // B_CHUNK):
        rs = slice(n * B_CHUNK, (n + 1) * B_CHUNK)
        for g in range(B_GROUPS):
            cs = slice(g * B_GROUP_CH, (g + 1) * B_GROUP_CH)
            mixed = jnp.dot(ws_ref[g], vn[rs, cs], preferred_element_type=F32) + bs_ref[g]
            yb_sc[rs, cs] = (_gelu_tanh(u_ref[rs, cs].astype(F32)) * mixed).astype(BF16)

    period = jnp.where(tile_is_ctx, ctx_len, GRID_W)
    pos = (i * tm + lax.broadcasted_iota(jnp.int32, (tm, 1), 0)) % period
    tt = gc_ref[...].astype(F32) * hc_ref[...].astype(F32)
    prev = jnp.where(pos == 0, 0.0, pltpu.roll(tt, 1, axis=0))
    nxt = jnp.where(pos == period - 1, 0.0, pltpu.roll(tt, tm - 1, axis=0))
    conv = cw_ref[0:1, :] * prev + cw_ref[1:2, :] * tt + cw_ref[2:3, :] * nxt
    yc = (gb_ref[...].astype(F32) * conv).astype(BF16)

    def gated2(m_ref, p):
        return p + jnp.tanh((0.5 * m_ref[...]).astype(F32)) * p

    merged2 = gated2(ma_ref, jnp.dot(ya_ref[...], wpa_ref[...], preferred_element_type=F32))
    merged2 += gated2(mb_ref, jnp.dot(yb_sc[...], wpb_ref[...], preferred_element_type=F32))
    merged2 += gated2(mcg_ref, jnp.dot(yc, wpc_ref[...], preferred_element_type=F32))
    out2 = jnp.dot(merged2.astype(BF16), wo_ref[...], preferred_element_type=F32)
    gate = jnp.where(tile_is_ctx, mc_ref[5:6, :], mx_ref[5:6, :])
    o_ref[...] = x_ref[...] + (0.5 * gate) * out2


def _mix(xa, ya, z, mods, ln_g, ln_b, ws, bs, cw, wpa, wpb, wpc, wo, *, layer, ctx_len, col_m,
         col_s):
    bsz, t, d = xa.shape
    tm = MIX_TM
    assert ctx_len % tm == 0 and tm % B_CHUNK == 0 and tm % GRID_W == 0
    bw = B_GROUPS * B_GROUP_CH
    ctx_row = bsz
    kern = functools.partial(_mix_kernel, tm=tm, ctx_len=ctx_len)

    def zcol(start, width):
        return pl.BlockSpec((None, tm, width), lambda b, i: (b, i, start // width))

    def per_layer(arr):
        zeros = (0,) * (arr.ndim - 1)
        return pl.BlockSpec((None,) + arr.shape[1:], lambda b, i: (layer,) + zeros,
                            pipeline_mode=pl.Buffered(1))

    return pl.pallas_call(
        kern,
        grid=(bsz, t // tm),
        in_specs=[
            pl.BlockSpec((None, tm, d), lambda b, i: (b, i, 0)),
            pl.BlockSpec((None, tm, ya.shape[2]), lambda b, i: (b, i, 0)),
            zcol(col_m, d), zcol(col_m + d, d), zcol(col_m + 2 * d, d),
            zcol(col_s, bw), zcol(col_s + bw, bw), zcol(col_s + 2 * bw, bw),
            zcol(col_s + 3 * bw, bw), zcol(col_s + 4 * bw, bw),
            pl.BlockSpec((None, None, N_MOD, d), lambda b, i: (layer, b, 0, 0)),
            pl.BlockSpec((None, None, N_MOD, d), lambda b, i: (layer, ctx_row, 0, 0)),
            per_layer(ln_g), per_layer(ln_b), per_layer(ws), per_layer(bs), per_layer(cw),
            per_layer(wpa), per_layer(wpb), per_layer(wpc), per_layer(wo),
        ],
        out_specs=pl.BlockSpec((None, tm, d), lambda b, i: (b, i, 0)),
        out_shape=jax.ShapeDtypeStruct(xa.shape, F32),
        scratch_shapes=[pltpu.VMEM((tm, bw), BF16)],
        compiler_params=_params("arbitrary", "arbitrary"),
        name="mix",
    )(xa, ya, z, z, z, z, z, z, z, z, mods, mods, ln_g, ln_b, ws, bs, cw, wpa, wpb, wpc, wo)


def _final_norm_kernel(x_ref, g_ref, o_ref):
    x = x_ref[...]
    ms = jnp.mean(x * x, axis=-1, keepdims=True)
    o_ref[...] = x * lax.rsqrt(ms + EPS) * g_ref[...]


def _final_norm(xa, g, *, ctx_len):
    bsz, t, d = xa.shape
    tm = NORM_TM
    assert ctx_len % tm == 0
    skip = ctx_len // tm
    return pl.pallas_call(
        _final_norm_kernel,
        grid=(bsz, (t - ctx_len) // tm),
        in_specs=[pl.BlockSpec((None, tm, d), lambda b, i: (b, i + skip, 0)),
                  pl.BlockSpec((1, d), lambda b, i: (0, 0))],
        out_specs=pl.BlockSpec((None, tm, d), lambda b, i: (b, i, 0)),
        out_shape=jax.ShapeDtypeStruct((bsz, t - ctx_len, d), F32),
        compiler_params=_params("arbitrary", "arbitrary"),
        name="final_norm",
    )(xa, g.reshape(1, d))


def _permute_w_in(w, d):
    wk = A_HEADS * A_DK
    wv = A_HEADS * A_DV
    bw = B_GROUPS * B_GROUP_CH
    sizes = (wk, wv, wk, wk, wv, bw, bw, C_WIDTH, C_WIDTH, C_WIDTH, d, d, d)
    offs = np.concatenate([[0], np.cumsum(sizes)])
    parts = [w[..., offs[k]:offs[k + 1]] for k in range(len(sizes))]
    q, iv, zf, zb, ga, u, v, hc, gb, gc, ma, mb, mc = parts
    cols = dict(q=0, v=wk, ff=wk + wv, fb=2 * wk + wv, m=3 * wk + wv,
                ga=3 * wk + wv + 3 * d, s=3 * wk + 2 * wv + 3 * d)
    return jnp.concatenate([q, iv, zf, zb, ma, mb, mc, ga, u, v, hc, gb, gc], axis=-1), cols


def kernel(x, c, ctx, c_ctx, w_mod, b_mod, norm_g, final_norm_g, ffn_w_gate, ffn_w_up,
           ffn_w_down, w_in, hgrn_lb_logits, hgrn_out_norm_g, gmlp_ln_g, gmlp_ln_b,
           gmlp_w_s, gmlp_b_s, conv_w, w_proj_a, w_proj_b, w_proj_c, w_out):
    bsz, _, d = x.shape
    depth = w_mod.shape[0]
    ctx_len = ctx.shape[1]
    assert ctx_len % ROW_BLOCK == 0
    xa = jnp.concatenate([ctx, x], axis=1)

    mod_rows = 8 * (-(-(bsz + 1) // 8))
    cvec = jnp.zeros((mod_rows, d), F32).at[:bsz].set(c).at[bsz].set(c_ctx)
    mods = _modulation(cvec, w_mod, b_mod).reshape(depth, mod_rows, N_MOD, d)

    lb = jnp.cumsum(jax.nn.softmax(hgrn_lb_logits.astype(F32), axis=0), axis=0)
    lb = (lb - lb[:1]).reshape(depth, 1, -1)
    la = jnp.log1p(-lb)
    lbl = jnp.log(lb)

    sums_np, masks_np = _hgrn_tables(HGRN_CHUNK)
    sums = jnp.asarray(sums_np, BF16)
    masks = jnp.asarray(masks_np, BF16)

    wg, wu, wd = (w.astype(BF16) for w in (ffn_w_gate, ffn_w_up, ffn_w_down))
    w_in_p, cols = _permute_w_in(w_in.astype(BF16), d)
    ws, wpa, wpb, wpc, wo = (w.astype(BF16) for w in (gmlp_w_s, w_proj_a, w_proj_b, w_proj_c,
                                                      w_out))
    norm_g4 = norm_g.reshape(depth, -1, 1, d)
    gn = hgrn_out_norm_g.reshape(depth, 1, -1)
    ln_g = gmlp_ln_g.reshape(depth, 1, -1)
    ln_b = gmlp_ln_b.reshape(depth, 1, -1)
    bs = gmlp_b_s[..., None]

    for l in range(depth):
        xa = _ffn(xa, mods, norm_g4, wg, wu, wd, layer=l, slot=0, which=0, ctx_len=ctx_len)
        z = _inproj(xa, mods, norm_g4, w_in_p, la, lbl, layer=l, ctx_len=ctx_len,
                    lf_lo=cols["ff"])
        ya = _hgrn(z, sums, masks, gn, layer=l, ctx_len=ctx_len, col_q=cols["q"],
                   col_v=cols["v"], col_ff=cols["ff"], col_fb=cols["fb"], col_ga=cols["ga"])
        xa = _mix(xa, ya, z, mods, ln_g, ln_b, ws, bs, conv_w, wpa, wpb, wpc, wo, layer=l,
                  ctx_len=ctx_len, col_m=cols["m"], col_s=cols["s"])
        xa = _ffn(xa, mods, norm_g4, wg, wu, wd, layer=l, slot=2, which=1, ctx_len=ctx_len)
    return _final_norm(xa, final_norm_g, ctx_len=ctx_len)
```

```python
import functools
import math

import numpy as np
import jax
import jax.numpy as jnp
from jax import lax
from jax.experimental import pallas as pl
from jax.experimental.pallas import tpu as pltpu

F32 = jnp.float32
BF16 = jnp.bfloat16

EPS = 1e-6
N_MOD = 9
GRID_W = 64
A_HEADS = 8
A_DK = 128
A_DV = 128
B_GROUPS = 4
B_GROUP_CH = 128
B_CHUNK = 128
C_WIDTH = 512

VMEM_LIMIT_BYTES = 60 * 1024 * 1024

HGRN_CHUNK = 128
ROW_BLOCK = 16
FFN_TM = 544
FFN_TF = 512
IN_TM = 1088
IN_TN = 1536
MIX_TM = 256
MIX_PIECE = 512
MOD_TN = 1024
NORM_TM = 256


def _params(*sem):
    return pltpu.CompilerParams(dimension_semantics=sem, vmem_limit_bytes=VMEM_LIMIT_BYTES)


def _sigmoid(a):
    return 0.5 * (jnp.tanh(0.5 * a) + 1.0)


def _gelu_tanh(a):
    return 0.5 * a * (1.0 + jnp.tanh(math.sqrt(2.0 / math.pi) * (a + 0.044715 * (a * a * a))))


def _row_blocks(tile, tm, ctx_len, fn, *, trips):
    def body(r, carry):
        start = pl.multiple_of(r * ROW_BLOCK, ROW_BLOCK)
        fn(pl.ds(start, ROW_BLOCK), tile * tm + start < ctx_len)
        return carry

    n = tm // ROW_BLOCK
    lax.fori_loop(0, n, body, 0, unroll=n // trips if n % trips == 0 else 1)


def _adaln_rows(x_ref, h_ref, vec_sc, g_ref, mx_ref, mc_ref, k_shift, tile, tm, ctx_len,
                zero_ref=None):
    d = x_ref.shape[-1]
    for s, m_ref in enumerate((mx_ref, mc_ref)):
        gain = g_ref[...] * (1.0 + m_ref[k_shift + 1:k_shift + 2, :])
        vec_sc[s, 0] = jnp.broadcast_to(gain, (ROW_BLOCK, d))
        vec_sc[s, 1] = jnp.broadcast_to(m_ref[k_shift:k_shift + 1, :], (ROW_BLOCK, d))

    def block(rows, is_ctx):
        s = is_ctx.astype(jnp.int32)
        x = x_ref[rows, :]
        ms = jnp.mean(x * x, axis=-1, keepdims=True)
        h = (x * lax.rsqrt(ms + EPS)) * vec_sc[s, 0] + vec_sc[s, 1]
        h_ref[rows, :] = h.astype(h_ref.dtype)
        if zero_ref is not None:
            zero_ref[rows, :] = jnp.zeros((ROW_BLOCK, d), zero_ref.dtype)

    _row_blocks(tile, tm, ctx_len, block, trips=2)


def _mod_kernel(c_ref, w_ref, b_ref, o_ref):
    c = c_ref[...]
    s = (c * _sigmoid(c)).astype(BF16)
    o_ref[...] = jnp.dot(s, w_ref[...].astype(BF16), preferred_element_type=F32) + b_ref[...]


def _modulation(cvec, w_mod, b_mod):
    depth, d, nw = w_mod.shape
    rows = cvec.shape[0]
    tn = MOD_TN
    return pl.pallas_call(
        _mod_kernel,
        grid=(depth, nw // tn),
        in_specs=[
            pl.BlockSpec((rows, d), lambda l, j: (0, 0)),
            pl.BlockSpec((None, d, tn), lambda l, j: (l, 0, j)),
            pl.BlockSpec((None, 1, tn), lambda l, j: (l, 0, j)),
        ],
        out_specs=pl.BlockSpec((None, rows, tn), lambda l, j: (l, 0, j)),
        out_shape=jax.ShapeDtypeStruct((depth, rows, nw), F32),
        compiler_params=_params("arbitrary", "arbitrary"),
        name="modulation",
    )(cvec, w_mod, b_mod.reshape(depth, 1, nw))


def _ffn_kernel(x_ref, mx_ref, mc_ref, g_ref, wg_ref, wu_ref, wd_ref, o_ref, h_sc, acc_sc, vec_sc,
                *, slot, tm, ctx_len):
    i = pl.program_id(1)
    j = pl.program_id(2)

    @pl.when(j == 0)
    def _():
        _adaln_rows(x_ref, h_sc, vec_sc, g_ref, mx_ref, mc_ref, 3 * slot, i, tm, ctx_len,
                    zero_ref=acc_sc)

    h = h_sc[...]
    a = jnp.dot(h, wg_ref[...], preferred_element_type=F32)
    u = jnp.dot(h, wu_ref[...], preferred_element_type=F32)
    act = ((a * _sigmoid(a)) * u).astype(BF16)
    acc_sc[...] += jnp.dot(act, wd_ref[...], preferred_element_type=F32)

    @pl.when(j == pl.num_programs(2) - 1)
    def _():
        k_gate = 3 * slot + 2

        def block(rows, is_ctx):
            gate = jnp.where(is_ctx, mc_ref[k_gate:k_gate + 1, :], mx_ref[k_gate:k_gate + 1, :])
            o_ref[rows, :] = x_ref[rows, :] + (0.5 * gate) * acc_sc[rows, :]

        _row_blocks(i, tm, ctx_len, block, trips=tm // (2 * ROW_BLOCK))


def _ffn(xa, mods, norm_g, wg, wu, wd, *, layer, slot, which, ctx_len):
    bsz, t, d = xa.shape
    f = wg.shape[-1]
    tm, tf = FFN_TM, FFN_TF
    ctx_row = bsz
    kern = functools.partial(_ffn_kernel, slot=slot, tm=tm, ctx_len=ctx_len)
    return pl.pallas_call(
        kern,
        grid=(bsz, t // tm, f // tf),
        in_specs=[
            pl.BlockSpec((None, tm, d), lambda b, i, j: (b, i, 0)),
            pl.BlockSpec((None, None, N_MOD, d), lambda b, i, j: (layer, b, 0, 0)),
            pl.BlockSpec((None, None, N_MOD, d), lambda b, i, j: (layer, ctx_row, 0, 0)),
            pl.BlockSpec((None, None, 1, d), lambda b, i, j: (layer, slot, 0, 0)),
            pl.BlockSpec((None, None, d, tf), lambda b, i, j: (layer, which, 0, j)),
            pl.BlockSpec((None, None, d, tf), lambda b, i, j: (layer, which, 0, j)),
            pl.BlockSpec((None, None, tf, d), lambda b, i, j: (layer, which, j, 0)),
        ],
        out_specs=pl.BlockSpec((None, tm, d), lambda b, i, j: (b, i, 0)),
        out_shape=jax.ShapeDtypeStruct(xa.shape, F32),
        scratch_shapes=[pltpu.VMEM((tm, d), BF16), pltpu.VMEM((tm, d), F32),
                        pltpu.VMEM((2, 2, ROW_BLOCK, d), F32)],
        compiler_params=_params("arbitrary", "arbitrary", "arbitrary"),
        name=f"ffn{slot}",
    )(xa, mods, mods, norm_g, wg, wu, wd)


LOG2_E = 1.4426950408889634


def _log2_forget(z, la, lbl):
    ls = jnp.minimum(z, 0.0) - jnp.log(1.0 + jnp.exp(-jnp.abs(z)))
    a = la + ls
    hi = jnp.maximum(a, lbl)
    return (hi + jnp.log(1.0 + jnp.exp(-jnp.abs(a - lbl)))) * LOG2_E


def _inproj_kernel(x_ref, mx_ref, mc_ref, g_ref, w_ref, la_ref, lbl_ref, o_ref, h_sc, vec_sc,
                   *, tm, tn, ctx_len, lf_lo, lf_hi):
    i = pl.program_id(1)
    j = pl.program_id(2)

    @pl.when(j == 0)
    def _():
        _adaln_rows(x_ref, h_sc, vec_sc, g_ref, mx_ref, mc_ref, 3, i, tm, ctx_len)

    z = jnp.dot(h_sc[...], w_ref[...], preferred_element_type=F32)
    o_ref[...] = z.astype(o_ref.dtype)

    for jj in range(lf_lo // tn, pl.cdiv(lf_hi, tn)):
        lo = max(lf_lo, jj * tn)
        hi = min(lf_hi, (jj + 1) * tn)
        cols = slice(lo - jj * tn, hi - jj * tn)
        lfc = slice(lo - lf_lo, hi - lf_lo)

        @pl.when(j == jj)
        def _(cols=cols, lfc=lfc):
            o_ref[:, cols] = _log2_forget(z[:, cols], la_ref[:, lfc], lbl_ref[:, lfc]).astype(
                o_ref.dtype)


def _inproj(xa, mods, norm_g, w, la, lbl, *, layer, ctx_len, lf_lo):
    bsz, t, d = xa.shape
    nw = w.shape[-1]
    tm, tn = IN_TM, IN_TN
    lf_w = la.shape[-1]
    ctx_row = bsz
    kern = functools.partial(_inproj_kernel, tm=tm, tn=tn, ctx_len=ctx_len,
                             lf_lo=lf_lo, lf_hi=lf_lo + lf_w)
    return pl.pallas_call(
        kern,
        grid=(bsz, t // tm, nw // tn),
        in_specs=[
            pl.BlockSpec((None, tm, d), lambda b, i, j: (b, i, 0)),
            pl.BlockSpec((None, None, N_MOD, d), lambda b, i, j: (layer, b, 0, 0)),
            pl.BlockSpec((None, None, N_MOD, d), lambda b, i, j: (layer, ctx_row, 0, 0)),
            pl.BlockSpec((None, None, 1, d), lambda b, i, j: (layer, 1, 0, 0)),
            pl.BlockSpec((None, d, tn), lambda b, i, j: (layer, 0, j)),
            pl.BlockSpec((None, 1, lf_w), lambda b, i, j: (layer, 0, 0)),
            pl.BlockSpec((None, 1, lf_w), lambda b, i, j: (layer, 0, 0)),
        ],
        out_specs=pl.BlockSpec((None, tm, tn), lambda b, i, j: (b, i, j)),
        out_shape=jax.ShapeDtypeStruct((bsz, t, nw), BF16),
        scratch_shapes=[pltpu.VMEM((tm, d), BF16), pltpu.VMEM((2, 2, ROW_BLOCK, d), F32)],
        compiler_params=_params("arbitrary", "arbitrary", "arbitrary"),
        name="inproj",
    )(xa, mods, mods, norm_g, w, la, lbl)


def _hgrn_tables(c):
    levels = int(math.log2(c))
    t = np.arange(c)[:, None]
    r = np.arange(c)[None, :]
    sums = np.zeros((2, 2 + levels, c, c), np.float32)
    masks = np.zeros((2, levels + 1, c, c), np.float32)
    masks[:, levels] = np.eye(c)
    sums[0, 0] = r <= t
    sums[0, 1] = r > t
    sums[1, 0] = r >= t
    sums[1, 1] = r < t
    for l in range(levels):
        h = 1 << l
        mid = (t // (2 * h)) * 2 * h + h - 1
        later = (t & h) != 0
        sums[0, 2 + l] = np.where(later, (r > mid) & (r <= t), (r > t) & (r <= mid))
        sums[1, 2 + l] = np.where(later, (r > mid) & (r < t), (r >= t) & (r <= mid))
        same = (t // (2 * h)) == (r // (2 * h))
        masks[0, l] = same & later & ((r & h) == 0)
        masks[1, l] = same & (~later) & ((r & h) != 0)
    return sums.reshape(2, (2 + levels) * c, c), masks


_NT = (((1,), (1,)), ((), ()))
_TN = (((0,), (0,)), ((), ()))


def _hgrn_kernel(q_ref, v_ref, lff_ref, lfb_ref, ga_ref, sums_ref, masks_ref, gn_ref, y_ref,
                 of_sc, ob_sc, e_sc, a_sc, qd_sc, kd_sc, g_sc, *, n_ctx_chunks, n_chunks, scale):
    c = HGRN_CHUNK
    levels = masks_ref.shape[1] - 1
    lf_refs = (lff_ref, lfb_ref)
    o_scs = (of_sc, ob_sc)

    def rows_of(d, p):
        if d == 0:
            ci = p
        else:
            ci = jnp.where(p < n_ctx_chunks, n_ctx_chunks - 1 - p, n_chunks - 1 + n_ctx_chunks - p)
        return pl.ds(pl.multiple_of(ci * c, c), c)

    def stage_sums(p, slot):
        for d in range(2):
            lf = lf_refs[d][rows_of(d, p), :]
            e_sc[d, slot] = jnp.dot(sums_ref[d], lf, preferred_element_type=F32)

    def stage_scores(p, slot):
        for d in range(2):
            rows = rows_of(d, p)
            qb = q_ref[rows, :]
            kb = (1.0 - jnp.exp2(lf_refs[d][rows, :].astype(F32))).astype(BF16)
            qd_sc[d, slot] = qb * jnp.exp2(e_sc[d, slot, 0:c, :]).astype(BF16)
            kd_sc[d, slot] = kb * jnp.exp2(e_sc[d, slot, c:2 * c, :]).astype(BF16)
            kbt = kb.T
            a = jnp.dot(qb, kbt, preferred_element_type=F32).astype(BF16)
            a = a * masks_ref[d, levels]
            for l in range(levels):
                w = jnp.exp2(e_sc[d, slot, (2 + l) * c:(3 + l) * c, :]).astype(BF16)
                p_l = jnp.dot(qb * w, kbt * w.T, preferred_element_type=F32)
                a = a + p_l.astype(BF16) * masks_ref[d, l]
            a_sc[d, slot] = a
            total = (c - 1) if d == 0 else 0
            g_sc[d, slot] = jnp.exp2(e_sc[d, slot, total:total + 1, :])

    def stage_state(p, slot, states):
        new = []
        for d in range(2):
            rows = rows_of(d, p)
            v = v_ref[rows, :]
            st = states[d]
            o = lax.dot_general(qd_sc[d, slot], st.astype(BF16), _NT, preferred_element_type=F32)
            o_scs[d][rows, :] = o + jnp.dot(a_sc[d, slot], v, preferred_element_type=F32)
            new.append(st * g_sc[d, slot] + lax.dot_general(v, kd_sc[d, slot], _TN,
                                                             preferred_element_type=F32))
        return tuple(new)

    def step(j, par, states):
        stage_sums(j, par)
        states = stage_state(j - 2, par, states)
        stage_scores(j - 1, 1 - par)
        return states

    def body(m, states):
        j = 2 + 2 * m
        return step(j + 1, 1, step(j, 0, states))

    assert n_chunks % 2 == 0
    last = n_chunks - 1
    zero = jnp.zeros((A_DV, A_DK), F32)
    stage_sums(0, 0)
    stage_scores(0, 0)
    stage_sums(1, 1)
    states = lax.fori_loop(0, (n_chunks - 2) // 2, body, (zero, zero))
    states = stage_state(last - 1, (last - 1) % 2, states)
    stage_scores(last, last % 2)
    stage_state(last, last % 2, states)

    def readout(j, carry):
        rows = pl.ds(pl.multiple_of(j * c, c), c)
        o = (of_sc[rows, :] + ob_sc[rows, :]) * scale
        ms = jnp.mean(o * o, axis=-1, keepdims=True)
        ga = ga_ref[rows, :].astype(F32)
        y = (o * lax.rsqrt(ms + EPS) * gn_ref[...]) * (ga * _sigmoid(ga))
        y_ref[rows, :] = y.astype(y_ref.dtype)
        return carry

    lax.fori_loop(0, n_chunks, readout, 0, unroll=n_chunks // 2)


def _hgrn(z, sums, masks, gn, *, layer, ctx_len, col_q, col_v, col_ff, col_fb, col_ga):
    bsz, t, _ = z.shape
    c = HGRN_CHUNK
    dk = A_DK
    assert t // c >= 2
    kern = functools.partial(_hgrn_kernel, n_ctx_chunks=ctx_len // c, n_chunks=t // c,
                             scale=A_DK ** -0.5)

    def col(start):
        return pl.BlockSpec((None, t, dk), lambda b, h: (b, 0, start // dk + h))

    return pl.pallas_call(
        kern,
        grid=(bsz, A_HEADS),
        in_specs=[
            col(col_q), col(col_v), col(col_ff), col(col_fb), col(col_ga),
            pl.BlockSpec(sums.shape, lambda b, h: (0, 0, 0)),
            pl.BlockSpec(masks.shape, lambda b, h: (0, 0, 0, 0)),
            pl.BlockSpec((None, 1, A_DV), lambda b, h: (layer, 0, 0)),
        ],
        out_specs=pl.BlockSpec((None, t, A_DV), lambda b, h: (b, 0, h)),
        out_shape=jax.ShapeDtypeStruct((bsz, t, A_HEADS * A_DV), BF16),
        scratch_shapes=[
            pltpu.VMEM((t, A_DV), F32), pltpu.VMEM((t, A_DV), F32),
            pltpu.VMEM((2, 2) + sums.shape[1:], F32),
            pltpu.VMEM((2, 2, c, c), BF16),
            pltpu.VMEM((2, 2, c, dk), BF16),
            pltpu.VMEM((2, 2, c, dk), BF16),
            pltpu.VMEM((2, 2, 1, dk), F32),
        ],
        compiler_params=_params("arbitrary", "arbitrary"),
        name="hgrn",
    )(z, z, z, z, z, sums, masks, gn)


def _mix_kernel(x_ref, ya_ref, *refs, tm, ctx_len, n_piece):
    m_refs = refs[:3 * n_piece]
    (u_ref, v_ref, hc_ref, gb_ref, gc_ref, mx_ref, mc_ref, lng_ref, lnb_ref, ws_ref, bs_ref,
     cw_ref, wpa_ref, wpb_ref, wpc_ref, wo_ref, o_ref, yb_sc, mg_sc) = refs[3 * n_piece:]
    i = pl.program_id(1)
    tile_is_ctx = i * tm < ctx_len

    vg = _gelu_tanh(v_ref[...].astype(F32))
    mu = jnp.mean(vg, axis=-1, keepdims=True)
    dv = vg - mu
    var = jnp.mean(dv * dv, axis=-1, keepdims=True)
    vn = (dv * lax.rsqrt(var + EPS) * lng_ref[...] + lnb_ref[...]).astype(BF16)
    for n in range(tm // B_CHUNK):
        rs = slice(n * B_CHUNK, (n + 1) * B_CHUNK)
        for g in range(B_GROUPS):
            cs = slice(g * B_GROUP_CH, (g + 1) * B_GROUP_CH)
            mixed = jnp.dot(ws_ref[g], vn[rs, cs], preferred_element_type=F32) + bs_ref[g]
            yb_sc[rs, cs] = (_gelu_tanh(u_ref[rs, cs].astype(F32)) * mixed).astype(BF16)

    period = jnp.where(tile_is_ctx, ctx_len, GRID_W)
    pos = (i * tm + lax.broadcasted_iota(jnp.int32, (tm, 1), 0)) & (period - 1)
    tt = gc_ref[...].astype(F32) * hc_ref[...].astype(F32)
    prev = jnp.where(pos == 0, 0.0, pltpu.roll(tt, 1, axis=0))
    nxt = jnp.where(pos == period - 1, 0.0, pltpu.roll(tt, tm - 1, axis=0))
    conv = cw_ref[0:1, :] * prev + cw_ref[1:2, :] * tt + cw_ref[2:3, :] * nxt
    yc = (gb_ref[...].astype(F32) * conv).astype(BF16)

    d = x_ref.shape[-1]
    pw = d // n_piece
    branches = ((ya_ref[...], wpa_ref), (yb_sc[...], wpb_ref), (yc, wpc_ref))
    for k in range(n_piece):
        cs = slice(k * pw, (k + 1) * pw)
        merged2 = None
        for g, (y, w_ref) in enumerate(branches):
            p = jnp.dot(y, w_ref[:, cs], preferred_element_type=F32)
            p = p + jnp.tanh((0.5 * m_refs[g * n_piece + k][...]).astype(F32)) * p
            merged2 = p if merged2 is None else merged2 + p
        mg_sc[:, cs] = merged2.astype(BF16)
    mg = mg_sc[...]
    for k in range(n_piece):
        cs = slice(k * pw, (k + 1) * pw)
        out2 = jnp.dot(mg, wo_ref[:, cs], preferred_element_type=F32)
        gate = jnp.where(tile_is_ctx, mc_ref[5:6, cs], mx_ref[5:6, cs])
        o_ref[:, cs] = x_ref[:, cs] + (0.5 * gate) * out2


def _mix(xa, ya, z, mods, ln_g, ln_b, ws, bs, cw, wpa, wpb, wpc, wo, *, layer, ctx_len, col_m,
         col_s):
    bsz, t, d = xa.shape
    tm = MIX_TM
    assert ctx_len % tm == 0 and tm % B_CHUNK == 0 and tm % GRID_W == 0
    assert ctx_len & (ctx_len - 1) == 0 and GRID_W & (GRID_W - 1) == 0
    bw = B_GROUPS * B_GROUP_CH
    ctx_row = bsz
    pw = min(MIX_PIECE, d)
    n_piece = d // pw
    kern = functools.partial(_mix_kernel, tm=tm, ctx_len=ctx_len, n_piece=n_piece)

    def zcol(start, width):
        assert start % width == 0
        return pl.BlockSpec((None, tm, width), lambda b, i: (b, i, start // width))

    def per_layer(arr):
        zeros = (0,) * (arr.ndim - 1)
        return pl.BlockSpec((None,) + arr.shape[1:], lambda b, i: (layer,) + zeros,
                            pipeline_mode=pl.Buffered(1))

    return pl.pallas_call(
        kern,
        grid=(bsz, t // tm),
        in_specs=[
            pl.BlockSpec((None, tm, d), lambda b, i: (b, i, 0)),
            pl.BlockSpec((None, tm, ya.shape[2]), lambda b, i: (b, i, 0)),
            *[zcol(col_m + g * d + k * pw, pw) for g in range(3) for k in range(n_piece)],
            zcol(col_s, bw), zcol(col_s + bw, bw), zcol(col_s + 2 * bw, bw),
            zcol(col_s + 3 * bw, bw), zcol(col_s + 4 * bw, bw),
            pl.BlockSpec((None, None, N_MOD, d), lambda b, i: (layer, b, 0, 0)),
            pl.BlockSpec((None, None, N_MOD, d), lambda b, i: (layer, ctx_row, 0, 0)),
            per_layer(ln_g), per_layer(ln_b), per_layer(ws), per_layer(bs), per_layer(cw),
            per_layer(wpa), per_layer(wpb), per_layer(wpc), per_layer(wo),
        ],
        out_specs=pl.BlockSpec((None, tm, d), lambda b, i: (b, i, 0)),
        out_shape=jax.ShapeDtypeStruct(xa.shape, F32),
        scratch_shapes=[pltpu.VMEM((tm, bw), BF16), pltpu.VMEM((tm, d), BF16)],
        compiler_params=_params("arbitrary", "arbitrary"),
        name="mix",
    )(xa, ya, *([z] * (3 * n_piece + 5)), mods, mods, ln_g, ln_b, ws, bs, cw, wpa, wpb, wpc, wo)


def _final_norm_kernel(x_ref, g_ref, o_ref):
    x = x_ref[...]
    ms = jnp.mean(x * x, axis=-1, keepdims=True)
    o_ref[...] = x * lax.rsqrt(ms + EPS) * g_ref[...]


def _final_norm(xa, g, *, ctx_len):
    bsz, t, d = xa.shape
    tm = NORM_TM
    assert ctx_len % tm == 0
    skip = ctx_len // tm
    return pl.pallas_call(
        _final_norm_kernel,
        grid=(bsz, (t - ctx_len) // tm),
        in_specs=[pl.BlockSpec((None, tm, d), lambda b, i: (b, i + skip, 0)),
                  pl.BlockSpec((1, d), lambda b, i: (0, 0))],
        out_specs=pl.BlockSpec((None, tm, d), lambda b, i: (b, i, 0)),
        out_shape=jax.ShapeDtypeStruct((bsz, t - ctx_len, d), F32),
        compiler_params=_params("arbitrary", "arbitrary"),
        name="final_norm",
    )(xa, g.reshape(1, d))


def _in_columns():
    wk = A_HEADS * A_DK
    wv = A_HEADS * A_DV
    bw = B_GROUPS * B_GROUP_CH
    s = 3 * wk + 2 * wv
    assert C_WIDTH == bw
    return dict(q=0, v=wk, ff=wk + wv, fb=2 * wk + wv, ga=3 * wk + wv, s=s, m=s + 5 * bw)


def kernel(x, c, ctx, c_ctx, w_mod, b_mod, norm_g, final_norm_g, ffn_w_gate, ffn_w_up,
           ffn_w_down, w_in, hgrn_lb_logits, hgrn_out_norm_g, gmlp_ln_g, gmlp_ln_b,
           gmlp_w_s, gmlp_b_s, conv_w, w_proj_a, w_proj_b, w_proj_c, w_out):
    bsz, _, d = x.shape
    depth = w_mod.shape[0]
    ctx_len = ctx.shape[1]
    assert ctx_len % ROW_BLOCK == 0
    xa = jnp.concatenate([ctx, x], axis=1)

    mod_rows = 8 * (-(-(bsz + 1) // 8))
    cvec = jnp.zeros((mod_rows, d), F32).at[:bsz].set(c).at[bsz].set(c_ctx)
    mods = _modulation(cvec, w_mod, b_mod).reshape(depth, mod_rows, N_MOD, d)

    lb = jnp.cumsum(jax.nn.softmax(hgrn_lb_logits.astype(F32), axis=0), axis=0)
    lb = (lb - lb[:1]).reshape(depth, 1, -1)
    la = jnp.log1p(-lb)
    lbl = jnp.log(lb)

    sums_np, masks_np = _hgrn_tables(HGRN_CHUNK)
    sums = jnp.asarray(sums_np, BF16)
    masks = jnp.asarray(masks_np, BF16)

    wg, wu, wd = (w.astype(BF16) for w in (ffn_w_gate, ffn_w_up, ffn_w_down))
    w_in_p, cols = w_in.astype(BF16), _in_columns()
    ws, wpa, wpb, wpc, wo = (w.astype(BF16) for w in (gmlp_w_s, w_proj_a, w_proj_b, w_proj_c,
                                                      w_out))
    norm_g4 = norm_g.reshape(depth, -1, 1, d)
    gn = hgrn_out_norm_g.reshape(depth, 1, -1)
    ln_g = gmlp_ln_g.reshape(depth, 1, -1)
    ln_b = gmlp_ln_b.reshape(depth, 1, -1)
    bs = gmlp_b_s[..., None]

    for l in range(depth):
        xa = _ffn(xa, mods, norm_g4, wg, wu, wd, layer=l, slot=0, which=0, ctx_len=ctx_len)
        z = _inproj(xa, mods, norm_g4, w_in_p, la, lbl, layer=l, ctx_len=ctx_len,
                    lf_lo=cols["ff"])
        ya = _hgrn(z, sums, masks, gn, layer=l, ctx_len=ctx_len, col_q=cols["q"],
                   col_v=cols["v"], col_ff=cols["ff"], col_fb=cols["fb"], col_ga=cols["ga"])
        xa = _mix(xa, ya, z, mods, ln_g, ln_b, ws, bs, conv_w, wpa, wpb, wpc, wo, layer=l,
                  ctx_len=ctx_len, col_m=cols["m"], col_s=cols["s"])
        xa = _ffn(xa, mods, norm_g4, wg, wu, wd, layer=l, slot=2, which=1, ctx_len=ctx_len)
    return _final_norm(xa, final_norm_g, ctx_len=ctx_len)
```

```python
import functools
import math

import numpy as np
import jax
import jax.numpy as jnp
from jax import lax
from jax.experimental import pallas as pl
from jax.experimental.pallas import tpu as pltpu

F32 = jnp.float32
BF16 = jnp.bfloat16

EPS = 1e-6
N_MOD = 9
GRID_W = 64
A_HEADS = 8
A_DK = 128
A_DV = 128
B_GROUPS = 4
B_GROUP_CH = 128
B_CHUNK = 128
C_WIDTH = 512

VMEM_LIMIT_BYTES = 60 * 1024 * 1024
MXU_COLS = 256

HGRN_CHUNK = 128
ROW_BLOCK = 16
FFN_TM = 544
FFN_TF = 1408
FFN_PIECE = 512
IN_TM = 1088
IN_TN = 1536
MIX_TM = 256
MIX_PIECE = 512
MOD_TN = 1024
NORM_TM = 256


def _params(*sem):
    return pltpu.CompilerParams(dimension_semantics=sem, vmem_limit_bytes=VMEM_LIMIT_BYTES)


def _sigmoid(a):
    return 0.5 * (jnp.tanh(0.5 * a) + 1.0)


def _gelu_tanh(a):
    return 0.5 * a * (1.0 + jnp.tanh(math.sqrt(2.0 / math.pi) * (a + 0.044715 * (a * a * a))))


def _row_blocks(tile, tm, ctx_len, fn, *, trips):
    def body(r, carry):
        start = pl.multiple_of(r * ROW_BLOCK, ROW_BLOCK)
        fn(pl.ds(start, ROW_BLOCK), tile * tm + start < ctx_len)
        return carry

    n = tm // ROW_BLOCK
    lax.fori_loop(0, n, body, 0, unroll=n // trips if n % trips == 0 else 1)


def _adaln_rows(x_ref, h_ref, vec_sc, g_ref, mx_ref, mc_ref, k_shift, tile, tm, ctx_len,
                zero_ref=None):
    d = x_ref.shape[-1]
    for s, m_ref in enumerate((mx_ref, mc_ref)):
        gain = g_ref[...] * (1.0 + m_ref[k_shift + 1:k_shift + 2, :])
        vec_sc[s, 0] = jnp.broadcast_to(gain, (ROW_BLOCK, d))
        vec_sc[s, 1] = jnp.broadcast_to(m_ref[k_shift:k_shift + 1, :], (ROW_BLOCK, d))

    def block(rows, is_ctx):
        s = is_ctx.astype(jnp.int32)
        x = x_ref[rows, :]
        ms = jnp.mean(x * x, axis=-1, keepdims=True)
        h = (x * lax.rsqrt(ms + EPS)) * vec_sc[s, 0] + vec_sc[s, 1]
        h_ref[rows, :] = h.astype(h_ref.dtype)
        if zero_ref is not None:
            zero_ref[rows, :] = jnp.zeros((ROW_BLOCK, d), zero_ref.dtype)

    _row_blocks(tile, tm, ctx_len, block, trips=2)


def _mod_kernel(c_ref, w_ref, b_ref, o_ref):
    c = c_ref[...]
    s = (c * _sigmoid(c)).astype(BF16)
    o_ref[...] = jnp.dot(s, w_ref[...].astype(BF16), preferred_element_type=F32) + b_ref[...]


def _modulation(cvec, w_mod, b_mod):
    depth, d, nw = w_mod.shape
    rows = cvec.shape[0]
    tn = MOD_TN
    return pl.pallas_call(
        _mod_kernel,
        grid=(depth, nw // tn),
        in_specs=[
            pl.BlockSpec((rows, d), lambda l, j: (0, 0)),
            pl.BlockSpec((None, d, tn), lambda l, j: (l, 0, j)),
            pl.BlockSpec((None, 1, tn), lambda l, j: (l, 0, j)),
        ],
        out_specs=pl.BlockSpec((None, rows, tn), lambda l, j: (l, 0, j)),
        out_shape=jax.ShapeDtypeStruct((depth, rows, nw), F32),
        compiler_params=_params("arbitrary", "arbitrary"),
        name="modulation",
    )(cvec, w_mod, b_mod.reshape(depth, 1, nw))


def _col_pieces(width, piece):
    return [slice(lo, min(lo + piece, width)) for lo in range(0, width, piece)]


def _ffn_up_kernel(x_ref, mx_ref, mc_ref, g_ref, wg_ref, wu_ref, act_ref, h_sc, vec_sc,
                   *, slot, tm, ctx_len):
    i = pl.program_id(1)

    @pl.when(pl.program_id(2) == 0)
    def _():
        _adaln_rows(x_ref, h_sc, vec_sc, g_ref, mx_ref, mc_ref, 3 * slot, i, tm, ctx_len)

    h = h_sc[...]
    tf = act_ref.shape[-1]
    paired = tf - tf % MXU_COLS
    for cs in _col_pieces(paired, FFN_PIECE):
        a = jnp.dot(h, wg_ref[:, cs], preferred_element_type=F32)
        u = jnp.dot(h, wu_ref[:, cs], preferred_element_type=F32)
        act_ref[:, cs] = ((a * _sigmoid(a)) * u).astype(act_ref.dtype)
    if paired < tf:
        half = tf - paired
        w_tail = jnp.concatenate([wg_ref[:, paired:], wu_ref[:, paired:]], axis=1)
        au = jnp.dot(h, w_tail, preferred_element_type=F32)
        a, u = au[:, :half], au[:, half:]
        act_ref[:, paired:] = ((a * _sigmoid(a)) * u).astype(act_ref.dtype)


def _ffn_down_kernel(act_ref, x_ref, mx_ref, mc_ref, wd_ref, o_ref, *, slot, tm, ctx_len):
    is_ctx = (pl.program_id(1) * tm + lax.broadcasted_iota(jnp.int32, (tm, 1), 0)) < ctx_len
    act = act_ref[...]
    k_gate = 3 * slot + 2
    for cs in _col_pieces(o_ref.shape[-1], FFN_PIECE):
        y = jnp.dot(act, wd_ref[:, cs], preferred_element_type=F32)
        gate = jnp.where(is_ctx, mc_ref[k_gate:k_gate + 1, cs], mx_ref[k_gate:k_gate + 1, cs])
        o_ref[:, cs] = x_ref[:, cs] + (0.5 * gate) * y


def _ffn(xa, mods, norm_g, wg, wu, wd, *, layer, slot, which, ctx_len):
    bsz, t, d = xa.shape
    f = wg.shape[-1]
    tm, tf = FFN_TM, FFN_TF
    ctx_row = bsz
    mods_x = pl.BlockSpec((None, None, N_MOD, d), lambda b, i, *_: (layer, b, 0, 0))
    mods_c = pl.BlockSpec((None, None, N_MOD, d), lambda b, i, *_: (layer, ctx_row, 0, 0))
    act = pl.pallas_call(
        functools.partial(_ffn_up_kernel, slot=slot, tm=tm, ctx_len=ctx_len),
        grid=(bsz, t // tm, f // tf),
        in_specs=[
            pl.BlockSpec((None, tm, d), lambda b, i, j: (b, i, 0)),
            mods_x, mods_c,
            pl.BlockSpec((None, None, 1, d), lambda b, i, j: (layer, slot, 0, 0)),
            pl.BlockSpec((None, None, d, tf), lambda b, i, j: (layer, which, 0, j)),
            pl.BlockSpec((None, None, d, tf), lambda b, i, j: (layer, which, 0, j)),
        ],
        out_specs=pl.BlockSpec((None, tm, tf), lambda b, i, j: (b, i, j)),
        out_shape=jax.ShapeDtypeStruct((bsz, t, f), BF16),
        scratch_shapes=[pltpu.VMEM((tm, d), BF16), pltpu.VMEM((2, 2, ROW_BLOCK, d), F32)],
        compiler_params=_params("arbitrary", "arbitrary", "arbitrary"),
        name=f"ffn{slot}_up",
    )(xa, mods, mods, norm_g, wg, wu)
    return pl.pallas_call(
        functools.partial(_ffn_down_kernel, slot=slot, tm=tm, ctx_len=ctx_len),
        grid=(bsz, t // tm),
        in_specs=[
            pl.BlockSpec((None, tm, f), lambda b, i: (b, i, 0)),
            pl.BlockSpec((None, tm, d), lambda b, i: (b, i, 0)),
            mods_x, mods_c,
            pl.BlockSpec((None, None, f, d), lambda b, i: (layer, which, 0, 0),
                         pipeline_mode=pl.Buffered(1)),
        ],
        out_specs=pl.BlockSpec((None, tm, d), lambda b, i: (b, i, 0)),
        out_shape=jax.ShapeDtypeStruct(xa.shape, F32),
        compiler_params=_params("arbitrary", "arbitrary"),
        name=f"ffn{slot}_down",
    )(act, xa, mods, mods, wd)


LOG2_E = 1.4426950408889634


def _log2_forget(z, la, lbl):
    ls = jnp.minimum(z, 0.0) - jnp.log(1.0 + jnp.exp(-jnp.abs(z)))
    a = la + ls
    hi = jnp.maximum(a, lbl)
    return (hi + jnp.log(1.0 + jnp.exp(-jnp.abs(a - lbl)))) * LOG2_E


def _inproj_kernel(x_ref, mx_ref, mc_ref, g_ref, w_ref, la_ref, lbl_ref, o_ref, h_sc, vec_sc,
                   *, tm, tn, ctx_len, lf_lo, lf_hi):
    i = pl.program_id(1)
    j = pl.program_id(2)

    @pl.when(j == 0)
    def _():
        _adaln_rows(x_ref, h_sc, vec_sc, g_ref, mx_ref, mc_ref, 3, i, tm, ctx_len)

    z = jnp.dot(h_sc[...], w_ref[...], preferred_element_type=F32)
    o_ref[...] = z.astype(o_ref.dtype)

    for jj in range(lf_lo // tn, pl.cdiv(lf_hi, tn)):
        lo = max(lf_lo, jj * tn)
        hi = min(lf_hi, (jj + 1) * tn)
        cols = slice(lo - jj * tn, hi - jj * tn)
        lfc = slice(lo - lf_lo, hi - lf_lo)

        @pl.when(j == jj)
        def _(cols=cols, lfc=lfc):
            o_ref[:, cols] = _log2_forget(z[:, cols], la_ref[:, lfc], lbl_ref[:, lfc]).astype(
                o_ref.dtype)


def _inproj(xa, mods, norm_g, w, la, lbl, *, layer, ctx_len, lf_lo):
    bsz, t, d = xa.shape
    nw = w.shape[-1]
    tm, tn = IN_TM, IN_TN
    lf_w = la.shape[-1]
    ctx_row = bsz
    kern = functools.partial(_inproj_kernel, tm=tm, tn=tn, ctx_len=ctx_len,
                             lf_lo=lf_lo, lf_hi=lf_lo + lf_w)
    return pl.pallas_call(
        kern,
        grid=(bsz, t // tm, nw // tn),
        in_specs=[
            pl.BlockSpec((None, tm, d), lambda b, i, j: (b, i, 0)),
            pl.BlockSpec((None, None, N_MOD, d), lambda b, i, j: (layer, b, 0, 0)),
            pl.BlockSpec((None, None, N_MOD, d), lambda b, i, j: (layer, ctx_row, 0, 0)),
            pl.BlockSpec((None, None, 1, d), lambda b, i, j: (layer, 1, 0, 0)),
            pl.BlockSpec((None, d, tn), lambda b, i, j: (layer, 0, j)),
            pl.BlockSpec((None, 1, lf_w), lambda b, i, j: (layer, 0, 0)),
            pl.BlockSpec((None, 1, lf_w), lambda b, i, j: (layer, 0, 0)),
        ],
        out_specs=pl.BlockSpec((None, tm, tn), lambda b, i, j: (b, i, j)),
        out_shape=jax.ShapeDtypeStruct((bsz, t, nw), BF16),
        scratch_shapes=[pltpu.VMEM((tm, d), BF16), pltpu.VMEM((2, 2, ROW_BLOCK, d), F32)],
        compiler_params=_params("arbitrary", "arbitrary", "arbitrary"),
        name="inproj",
    )(xa, mods, mods, norm_g, w, la, lbl)


def _hgrn_tables(c):
    levels = int(math.log2(c))
    t = np.arange(c)[:, None]
    r = np.arange(c)[None, :]
    sums = np.zeros((2, 2 + levels, c, c), np.float32)
    masks = np.zeros((2, levels + 1, c, c), np.float32)
    masks[:, levels] = np.eye(c)
    sums[0, 0] = r <= t
    sums[0, 1] = r > t
    sums[1, 0] = r >= t
    sums[1, 1] = r < t
    for l in range(levels):
        h = 1 << l
        mid = (t // (2 * h)) * 2 * h + h - 1
        later = (t & h) != 0
        sums[0, 2 + l] = np.where(later, (r > mid) & (r <= t), (r > t) & (r <= mid))
        sums[1, 2 + l] = np.where(later, (r > mid) & (r < t), (r >= t) & (r <= mid))
        same = (t // (2 * h)) == (r // (2 * h))
        masks[0, l] = same & later & ((r & h) == 0)
        masks[1, l] = same & (~later) & ((r & h) != 0)
    return sums.reshape(2, (2 + levels) * c, c), masks


_NT = (((1,), (1,)), ((), ()))
_TN = (((0,), (0,)), ((), ()))


def _hgrn_kernel(q_ref, v_ref, lff_ref, lfb_ref, ga_ref, sums_ref, masks_ref, gn_ref, y_ref,
                 of_sc, ob_sc, e_sc, a_sc, qd_sc, kd_sc, g_sc, *, n_ctx_chunks, n_chunks, scale):
    c = HGRN_CHUNK
    levels = masks_ref.shape[1] - 1
    lf_refs = (lff_ref, lfb_ref)
    o_scs = (of_sc, ob_sc)

    def rows_of(d, p):
        if d == 0:
            ci = p
        else:
            ci = jnp.where(p < n_ctx_chunks, n_ctx_chunks - 1 - p, n_chunks - 1 + n_ctx_chunks - p)
        return pl.ds(pl.multiple_of(ci * c, c), c)

    def stage_sums(p, slot):
        for d in range(2):
            lf = lf_refs[d][rows_of(d, p), :]
            e_sc[d, slot] = jnp.dot(sums_ref[d], lf, preferred_element_type=F32)

    def stage_scores(p, slot):
        for d in range(2):
            rows = rows_of(d, p)
            qb = q_ref[rows, :]
            kb = (1.0 - jnp.exp2(lf_refs[d][rows, :].astype(F32))).astype(BF16)
            qd_sc[d, slot] = qb * jnp.exp2(e_sc[d, slot, 0:c, :]).astype(BF16)
            kd_sc[d, slot] = kb * jnp.exp2(e_sc[d, slot, c:2 * c, :]).astype(BF16)
            kbt = kb.T
            a = jnp.dot(qb, kbt, preferred_element_type=F32).astype(BF16)
            a = a * masks_ref[d, levels]
            for l in range(levels):
                w = jnp.exp2(e_sc[d, slot, (2 + l) * c:(3 + l) * c, :]).astype(BF16)
                p_l = jnp.dot(qb * w, kbt * w.T, preferred_element_type=F32)
                a = a + p_l.astype(BF16) * masks_ref[d, l]
            a_sc[d, slot] = a
            total = (c - 1) if d == 0 else 0
            g_sc[d, slot] = jnp.exp2(e_sc[d, slot, total:total + 1, :])

    def stage_state(p, slot, states):
        new = []
        for d in range(2):
            rows = rows_of(d, p)
            v = v_ref[rows, :]
            st = states[d]
            o = lax.dot_general(qd_sc[d, slot], st.astype(BF16), _NT, preferred_element_type=F32)
            o_scs[d][rows, :] = o + jnp.dot(a_sc[d, slot], v, preferred_element_type=F32)
            new.append(st * g_sc[d, slot] + lax.dot_general(v, kd_sc[d, slot], _TN,
                                                             preferred_element_type=F32))
        return tuple(new)

    def step(j, par, states):
        stage_sums(j, par)
        states = stage_state(j - 2, par, states)
        stage_scores(j - 1, 1 - par)
        return states

    def body(m, states):
        j = 2 + 2 * m
        return step(j + 1, 1, step(j, 0, states))

    assert n_chunks % 2 == 0
    last = n_chunks - 1
    zero = jnp.zeros((A_DV, A_DK), F32)
    stage_sums(0, 0)
    stage_scores(0, 0)
    stage_sums(1, 1)
    states = lax.fori_loop(0, (n_chunks - 2) // 2, body, (zero, zero))
    states = stage_state(last - 1, (last - 1) % 2, states)
    stage_scores(last, last % 2)
    stage_state(last, last % 2, states)

    def readout(j, carry):
        rows = pl.ds(pl.multiple_of(j * c, c), c)
        o = (of_sc[rows, :] + ob_sc[rows, :]) * scale
        ms = jnp.mean(o * o, axis=-1, keepdims=True)
        ga = ga_ref[rows, :].astype(F32)
        y = (o * lax.rsqrt(ms + EPS) * gn_ref[...]) * (ga * _sigmoid(ga))
        y_ref[rows, :] = y.astype(y_ref.dtype)
        return carry

    lax.fori_loop(0, n_chunks, readout, 0, unroll=n_chunks // 2)


def _hgrn(z, sums, masks, gn, *, layer, ctx_len, col_q, col_v, col_ff, col_fb, col_ga):
    bsz, t, _ = z.shape
    c = HGRN_CHUNK
    dk = A_DK
    assert t // c >= 2
    kern = functools.partial(_hgrn_kernel, n_ctx_chunks=ctx_len // c, n_chunks=t // c,
                             scale=A_DK ** -0.5)

    def col(start):
        return pl.BlockSpec((None, t, dk), lambda b, h: (b, 0, start // dk + h))

    return pl.pallas_call(
        kern,
        grid=(bsz, A_HEADS),
        in_specs=[
            col(col_q), col(col_v), col(col_ff), col(col_fb), col(col_ga),
            pl.BlockSpec(sums.shape, lambda b, h: (0, 0, 0)),
            pl.BlockSpec(masks.shape, lambda b, h: (0, 0, 0, 0)),
            pl.BlockSpec((None, 1, A_DV), lambda b, h: (layer, 0, 0)),
        ],
        out_specs=pl.BlockSpec((None, t, A_DV), lambda b, h: (b, 0, h)),
        out_shape=jax.ShapeDtypeStruct((bsz, t, A_HEADS * A_DV), BF16),
        scratch_shapes=[
            pltpu.VMEM((t, A_DV), F32), pltpu.VMEM((t, A_DV), F32),
            pltpu.VMEM((2, 2) + sums.shape[1:], F32),
            pltpu.VMEM((2, 2, c, c), BF16),
            pltpu.VMEM((2, 2, c, dk), BF16),
            pltpu.VMEM((2, 2, c, dk), BF16),
            pltpu.VMEM((2, 2, 1, dk), F32),
        ],
        compiler_params=_params("arbitrary", "arbitrary"),
        name="hgrn",
    )(z, z, z, z, z, sums, masks, gn)


def _mix_kernel(x_ref, ya_ref, *refs, tm, ctx_len, n_piece):
    m_refs = refs[:3 * n_piece]
    (u_ref, v_ref, hc_ref, gb_ref, gc_ref, mx_ref, mc_ref, lng_ref, lnb_ref, ws_ref, bs_ref,
     cw_ref, wpa_ref, wpb_ref, wpc_ref, wo_ref, o_ref, yb_sc, mg_sc) = refs[3 * n_piece:]
    i = pl.program_id(1)
    tile_is_ctx = i * tm < ctx_len

    vg = _gelu_tanh(v_ref[...].astype(F32))
    mu = jnp.mean(vg, axis=-1, keepdims=True)
    dv = vg - mu
    var = jnp.mean(dv * dv, axis=-1, keepdims=True)
    vn = (dv * lax.rsqrt(var + EPS) * lng_ref[...] + lnb_ref[...]).astype(BF16)
    for n in range(tm // B_CHUNK):
        rs = slice(n * B_CHUNK, (n + 1) * B_CHUNK)
        for g in range(B_GROUPS):
            cs = slice(g * B_GROUP_CH, (g + 1) * B_GROUP_CH)
            mixed = jnp.dot(ws_ref[g], vn[rs, cs], preferred_element_type=F32) + bs_ref[g]
            yb_sc[rs, cs] = (_gelu_tanh(u_ref[rs, cs].astype(F32)) * mixed).astype(BF16)

    period = jnp.where(tile_is_ctx, ctx_len, GRID_W)
    pos = (i * tm + lax.broadcasted_iota(jnp.int32, (tm, 1), 0)) & (period - 1)
    tt = gc_ref[...].astype(F32) * hc_ref[...].astype(F32)
    prev = jnp.where(pos == 0, 0.0, pltpu.roll(tt, 1, axis=0))
    nxt = jnp.where(pos == period - 1, 0.0, pltpu.roll(tt, tm - 1, axis=0))
    conv = cw_ref[0:1, :] * prev + cw_ref[1:2, :] * tt + cw_ref[2:3, :] * nxt
    yc = (gb_ref[...].astype(F32) * conv).astype(BF16)

    d = x_ref.shape[-1]
    pw = d // n_piece
    branches = ((ya_ref[...], wpa_ref), (yb_sc[...], wpb_ref), (yc, wpc_ref))
    for k in range(n_piece):
        cs = slice(k * pw, (k + 1) * pw)
        merged2 = None
        for g, (y, w_ref) in enumerate(branches):
            p = jnp.dot(y, w_ref[:, cs], preferred_element_type=F32)
            p = p + jnp.tanh((0.5 * m_refs[g * n_piece + k][...]).astype(F32)) * p
            merged2 = p if merged2 is None else merged2 + p
        mg_sc[:, cs] = merged2.astype(BF16)
    mg = mg_sc[...]
    for k in range(n_piece):
        cs = slice(k * pw, (k + 1) * pw)
        out2 = jnp.dot(mg, wo_ref[:, cs], preferred_element_type=F32)
        gate = jnp.where(tile_is_ctx, mc_ref[5:6, cs], mx_ref[5:6, cs])
        o_ref[:, cs] = x_ref[:, cs] + (0.5 * gate) * out2


def _mix(xa, ya, z, mods, ln_g, ln_b, ws, bs, cw, wpa, wpb, wpc, wo, *, layer, ctx_len, col_m,
         col_s):
    bsz, t, d = xa.shape
    tm = MIX_TM
    assert ctx_len % tm == 0 and tm % B_CHUNK == 0 and tm % GRID_W == 0
    assert ctx_len & (ctx_len - 1) == 0 and GRID_W & (GRID_W - 1) == 0
    bw = B_GROUPS * B_GROUP_CH
    ctx_row = bsz
    pw = min(MIX_PIECE, d)
    n_piece = d // pw
    kern = functools.partial(_mix_kernel, tm=tm, ctx_len=ctx_len, n_piece=n_piece)

    def zcol(start, width):
        assert start % width == 0
        return pl.BlockSpec((None, tm, width), lambda b, i: (b, i, start // width))

    def per_layer(arr):
        zeros = (0,) * (arr.ndim - 1)
        return pl.BlockSpec((None,) + arr.shape[1:], lambda b, i: (layer,) + zeros,
                            pipeline_mode=pl.Buffered(1))

    return pl.pallas_call(
        kern,
        grid=(bsz, t // tm),
        in_specs=[
            pl.BlockSpec((None, tm, d), lambda b, i: (b, i, 0)),
            pl.BlockSpec((None, tm, ya.shape[2]), lambda b, i: (b, i, 0)),
            *[zcol(col_m + g * d + k * pw, pw) for g in range(3) for k in range(n_piece)],
            zcol(col_s, bw), zcol(col_s + bw, bw), zcol(col_s + 2 * bw, bw),
            zcol(col_s + 3 * bw, bw), zcol(col_s + 4 * bw, bw),
            pl.BlockSpec((None, None, N_MOD, d), lambda b, i: (layer, b, 0, 0)),
            pl.BlockSpec((None, None, N_MOD, d), lambda b, i: (layer, ctx_row, 0, 0)),
            per_layer(ln_g), per_layer(ln_b), per_layer(ws), per_layer(bs), per_layer(cw),
            per_layer(wpa), per_layer(wpb), per_layer(wpc), per_layer(wo),
        ],
        out_specs=pl.BlockSpec((None, tm, d), lambda b, i: (b, i, 0)),
        out_shape=jax.ShapeDtypeStruct(xa.shape, F32),
        scratch_shapes=[pltpu.VMEM((tm, bw), BF16), pltpu.VMEM((tm, d), BF16)],
        compiler_params=_params("arbitrary", "arbitrary"),
        name="mix",
    )(xa, ya, *([z] * (3 * n_piece + 5)), mods, mods, ln_g, ln_b, ws, bs, cw, wpa, wpb, wpc, wo)


def _final_norm_kernel(x_ref, g_ref, o_ref):
    x = x_ref[...]
    ms = jnp.mean(x * x, axis=-1, keepdims=True)
    o_ref[...] = x * lax.rsqrt(ms + EPS) * g_ref[...]


def _final_norm(xa, g, *, ctx_len):
    bsz, t, d = xa.shape
    tm = NORM_TM
    assert ctx_len % tm == 0
    skip = ctx_len // tm
    return pl.pallas_call(
        _final_norm_kernel,
        grid=(bsz, (t - ctx_len) // tm),
        in_specs=[pl.BlockSpec((None, tm, d), lambda b, i: (b, i + skip, 0)),
                  pl.BlockSpec((1, d), lambda b, i: (0, 0))],
        out_specs=pl.BlockSpec((None, tm, d), lambda b, i: (b, i, 0)),
        out_shape=jax.ShapeDtypeStruct((bsz, t - ctx_len, d), F32),
        compiler_params=_params("arbitrary", "arbitrary"),
        name="final_norm",
    )(xa, g.reshape(1, d))


def _in_columns():
    wk = A_HEADS * A_DK
    wv = A_HEADS * A_DV
    bw = B_GROUPS * B_GROUP_CH
    s = 3 * wk + 2 * wv
    assert C_WIDTH == bw
    return dict(q=0, v=wk, ff=wk + wv, fb=2 * wk + wv, ga=3 * wk + wv, s=s, m=s + 5 * bw)


def kernel(x, c, ctx, c_ctx, w_mod, b_mod, norm_g, final_norm_g, ffn_w_gate, ffn_w_up,
           ffn_w_down, w_in, hgrn_lb_logits, hgrn_out_norm_g, gmlp_ln_g, gmlp_ln_b,
           gmlp_w_s, gmlp_b_s, conv_w, w_proj_a, w_proj_b, w_proj_c, w_out):
    bsz, _, d = x.shape
    depth = w_mod.shape[0]
    ctx_len = ctx.shape[1]
    assert ctx_len % ROW_BLOCK == 0
    xa = jnp.concatenate([ctx, x], axis=1)

    mod_rows = 8 * (-(-(bsz + 1) // 8))
    cvec = jnp.zeros((mod_rows, d), F32).at[:bsz].set(c).at[bsz].set(c_ctx)
    mods = _modulation(cvec, w_mod, b_mod).reshape(depth, mod_rows, N_MOD, d)

    lb = jnp.cumsum(jax.nn.softmax(hgrn_lb_logits.astype(F32), axis=0), axis=0)
    lb = (lb - lb[:1]).reshape(depth, 1, -1)
    la = jnp.log1p(-lb)
    lbl = jnp.log(lb)

    sums_np, masks_np = _hgrn_tables(HGRN_CHUNK)
    sums = jnp.asarray(sums_np, BF16)
    masks = jnp.asarray(masks_np, BF16)

    wg, wu, wd = (w.astype(BF16) for w in (ffn_w_gate, ffn_w_up, ffn_w_down))
    w_in_p, cols = w_in.astype(BF16), _in_columns()
    ws, wpa, wpb, wpc, wo = (w.astype(BF16) for w in (gmlp_w_s, w_proj_a, w_proj_b, w_proj_c,
                                                      w_out))
    norm_g4 = norm_g.reshape(depth, -1, 1, d)
    gn = hgrn_out_norm_g.reshape(depth, 1, -1)
    ln_g = gmlp_ln_g.reshape(depth, 1, -1)
    ln_b = gmlp_ln_b.reshape(depth, 1, -1)
    bs = gmlp_b_s[..., None]

    for l in range(depth):
        xa = _ffn(xa, mods, norm_g4, wg, wu, wd, layer=l, slot=0, which=0, ctx_len=ctx_len)
        z = _inproj(xa, mods, norm_g4, w_in_p, la, lbl, layer=l, ctx_len=ctx_len,
                    lf_lo=cols["ff"])
        ya = _hgrn(z, sums, masks, gn, layer=l, ctx_len=ctx_len, col_q=cols["q"],
                   col_v=cols["v"], col_ff=cols["ff"], col_fb=cols["fb"], col_ga=cols["ga"])
        xa = _mix(xa, ya, z, mods, ln_g, ln_b, ws, bs, conv_w, wpa, wpb, wpc, wo, layer=l,
                  ctx_len=ctx_len, col_m=cols["m"], col_s=cols["s"])
        xa = _ffn(xa, mods, norm_g4, wg, wu, wd, layer=l, slot=2, which=1, ctx_len=ctx_len)
    return _final_norm(xa, final_norm_g, ctx_len=ctx_len)
```

```python
import functools
import math

import numpy as np
import jax
import jax.numpy as jnp
from jax import lax
from jax.experimental import pallas as pl
from jax.experimental.pallas import tpu as pltpu

F32 = jnp.float32
BF16 = jnp.bfloat16

EPS = 1e-6
N_MOD = 9
GRID_W = 64
A_HEADS = 8
A_DK = 128
A_DV = 128
B_GROUPS = 4
B_GROUP_CH = 128
B_CHUNK = 128
C_WIDTH = 512

VMEM_LIMIT_BYTES = 60 * 1024 * 1024
MXU_COLS = 256

HGRN_CHUNK = 128
ROW_BLOCK = 16
FFN_TM = 544
FFN_TF = 1408
FFN_PIECE = 512
CAST_ROWS = 16
IN_TM = 1088
IN_TN = 1536
IN_PIECE = 512
MIX_TM = 256
MIX_PIECE = 512
MOD_TN = 1024
FINAL_TM = 256


def _params(*sem):
    return pltpu.CompilerParams(dimension_semantics=sem, vmem_limit_bytes=VMEM_LIMIT_BYTES)


def _sigmoid(a):
    return 0.5 * (jnp.tanh(0.5 * a) + 1.0)


def _gelu_tanh(a):
    return 0.5 * a * (1.0 + jnp.tanh(math.sqrt(2.0 / math.pi) * (a + 0.044715 * (a * a * a))))


def _row_blocks(tile, tm, ctx_len, fn, *, trips):
    def body(r, carry):
        start = pl.multiple_of(r * ROW_BLOCK, ROW_BLOCK)
        fn(pl.ds(start, ROW_BLOCK), tile * tm + start < ctx_len)
        return carry

    n = tm // ROW_BLOCK
    lax.fori_loop(0, n, body, 0, unroll=n // trips if n % trips == 0 else 1)


def _adaln_rows(x_ref, h_ref, vec_sc, g_ref, mx_ref, mc_ref, k_shift, tile, tm, ctx_len,
                zero_ref=None):
    d = x_ref.shape[-1]
    for s, m_ref in enumerate((mx_ref, mc_ref)):
        gain = g_ref[...] * (1.0 + m_ref[k_shift + 1:k_shift + 2, :])
        vec_sc[s, 0] = jnp.broadcast_to(gain, (ROW_BLOCK, d))
        vec_sc[s, 1] = jnp.broadcast_to(m_ref[k_shift:k_shift + 1, :], (ROW_BLOCK, d))

    def block(rows, is_ctx):
        s = is_ctx.astype(jnp.int32)
        x = x_ref[rows, :]
        ms = jnp.mean(x * x, axis=-1, keepdims=True)
        h = (x * lax.rsqrt(ms + EPS)) * vec_sc[s, 0] + vec_sc[s, 1]
        h_ref[rows, :] = h.astype(h_ref.dtype)
        if zero_ref is not None:
            zero_ref[rows, :] = jnp.zeros((ROW_BLOCK, d), zero_ref.dtype)

    _row_blocks(tile, tm, ctx_len, block, trips=2)


def _mod_kernel(c_ref, w_ref, b_ref, o_ref):
    c = c_ref[...]
    s = (c * _sigmoid(c)).astype(BF16)
    o_ref[...] = jnp.dot(s, w_ref[...].astype(BF16), preferred_element_type=F32) + b_ref[...]


def _modulation(cvec, w_mod, b_mod):
    depth, d, nw = w_mod.shape
    rows = cvec.shape[0]
    tn = MOD_TN
    return pl.pallas_call(
        _mod_kernel,
        grid=(depth, nw // tn),
        in_specs=[
            pl.BlockSpec((rows, d), lambda l, j: (0, 0)),
            pl.BlockSpec((None, d, tn), lambda l, j: (l, 0, j)),
            pl.BlockSpec((None, 1, tn), lambda l, j: (l, 0, j)),
        ],
        out_specs=pl.BlockSpec((None, rows, tn), lambda l, j: (l, 0, j)),
        out_shape=jax.ShapeDtypeStruct((depth, rows, nw), F32),
        compiler_params=_params("arbitrary", "arbitrary"),
        name="modulation",
    )(cvec, w_mod, b_mod.reshape(depth, 1, nw))


def _col_pieces(width, piece):
    return [slice(lo, min(lo + piece, width)) for lo in range(0, width, piece)]


def _ffn_up_kernel(x_ref, mx_ref, mc_ref, g_ref, wg_ref, wu_ref, *rest, slot, tm, ctx_len, n_cast):
    cast_src = rest[:n_cast]
    act_ref = rest[n_cast]
    cast_dst = rest[n_cast + 1:2 * n_cast + 1]
    h_sc, vec_sc = rest[2 * n_cast + 1:]
    for src, dst in zip(cast_src, cast_dst):
        dst[...] = src[...].astype(dst.dtype)
    i = pl.program_id(1)

    @pl.when(pl.program_id(2) == 0)
    def _():
        _adaln_rows(x_ref, h_sc, vec_sc, g_ref, mx_ref, mc_ref, 3 * slot, i, tm, ctx_len)

    h = h_sc[...]
    tf = act_ref.shape[-1]
    paired = tf - tf % MXU_COLS
    for cs in _col_pieces(paired, FFN_PIECE):
        a = jnp.dot(h, wg_ref[:, cs], preferred_element_type=F32)
        u = jnp.dot(h, wu_ref[:, cs], preferred_element_type=F32)
        act_ref[:, cs] = ((a * _sigmoid(a)) * u).astype(act_ref.dtype)
    if paired < tf:
        half = tf - paired
        w_tail = jnp.concatenate([wg_ref[:, paired:], wu_ref[:, paired:]], axis=1)
        au = jnp.dot(h, w_tail, preferred_element_type=F32)
        a, u = au[:, :half], au[:, half:]
        act_ref[:, paired:] = ((a * _sigmoid(a)) * u).astype(act_ref.dtype)


def _ffn_down_kernel(act_ref, x_ref, mx_ref, mc_ref, wd_ref, *rest, slot, tm, ctx_len, row0):
    o_ref = rest[-1]
    row = row0 + pl.program_id(1) * tm + lax.broadcasted_iota(jnp.int32, (tm, 1), 0)
    is_ctx = row < ctx_len
    act = act_ref[...]
    k_gate = 3 * slot + 2
    for cs in _col_pieces(o_ref.shape[-1], FFN_PIECE):
        y = jnp.dot(act, wd_ref[:, cs], preferred_element_type=F32)
        gate = jnp.where(is_ctx, mc_ref[k_gate:k_gate + 1, cs], mx_ref[k_gate:k_gate + 1, cs])
        o_ref[:, cs] = x_ref[:, cs] + (0.5 * gate) * y
    if len(rest) == 2:
        y = o_ref[...]
        ms = jnp.mean(y * y, axis=-1, keepdims=True)
        o_ref[...] = y * lax.rsqrt(ms + EPS) * rest[0][...]


def _ffn_steps(bsz, t, f):
    return bsz * (t // FFN_TM) * (f // FFN_TF)


def _ffn(xa, mods, norm_g, wg, wu, wd, *, layer, slot, which, ctx_len, casts=(), final_g=None):
    bsz, t, d = xa.shape
    f = wg.shape[-1]
    tm, tf = FFN_TM, FFN_TF
    nt, nj = t // tm, f // tf
    steps = _ffn_steps(bsz, t, f)
    ctx_row = bsz
    mods_x = pl.BlockSpec((None, None, N_MOD, d), lambda b, i, *_: (layer, b, 0, 0))
    mods_c = pl.BlockSpec((None, None, N_MOD, d), lambda b, i, *_: (layer, ctx_row, 0, 0))

    def step_block(arr, first):
        return pl.BlockSpec((None,) + arr.shape[1:],
                            lambda b, i, j: (first + (b * nt + i) * nj + j, 0, 0))

    cast_in = [step_block(src, lsrc * steps) for src, lsrc in casts]
    cast_out = [step_block(src, 0) for src, _ in casts]
    act, *cast = pl.pallas_call(
        functools.partial(_ffn_up_kernel, slot=slot, tm=tm, ctx_len=ctx_len, n_cast=len(casts)),
        grid=(bsz, nt, nj),
        in_specs=[
            pl.BlockSpec((None, tm, d), lambda b, i, j: (b, i, 0)),
            mods_x, mods_c,
            pl.BlockSpec((None, None, 1, d), lambda b, i, j: (layer, slot, 0, 0)),
            pl.BlockSpec((None, None, d, tf), lambda b, i, j: (0, which, 0, j)),
            pl.BlockSpec((None, None, d, tf), lambda b, i, j: (0, which, 0, j)),
            *cast_in,
        ],
        out_specs=[pl.BlockSpec((None, tm, tf), lambda b, i, j: (b, i, j)), *cast_out],
        out_shape=[jax.ShapeDtypeStruct((bsz, t, f), BF16),
                   *[jax.ShapeDtypeStruct((steps,) + src.shape[1:], BF16) for src, _ in casts]],
        scratch_shapes=[pltpu.VMEM((tm, d), BF16), pltpu.VMEM((2, 2, ROW_BLOCK, d), F32)],
        compiler_params=_params("arbitrary", "arbitrary", "arbitrary"),
        name=f"ffn{slot}_up",
    )(xa, mods, mods, norm_g, wg, wu, *[src for src, _ in casts])
    if final_g is None:
        tmd, skip, rows_out, extra, extra_specs = tm, 0, t, (), []
    else:
        tmd = FINAL_TM
        assert ctx_len % tmd == 0 and (t - ctx_len) % tmd == 0
        skip, rows_out, extra = ctx_len // tmd, t - ctx_len, (final_g.reshape(1, d),)
        extra_specs = [pl.BlockSpec((1, d), lambda b, i: (0, 0))]
    out = pl.pallas_call(
        functools.partial(_ffn_down_kernel, slot=slot, tm=tmd, ctx_len=ctx_len, row0=skip * tmd),
        grid=(bsz, rows_out // tmd),
        in_specs=[
            pl.BlockSpec((None, tmd, f), lambda b, i: (b, i + skip, 0)),
            pl.BlockSpec((None, tmd, d), lambda b, i: (b, i + skip, 0)),
            mods_x, mods_c,
            pl.BlockSpec((None, None, f, d), lambda b, i: (0, which, 0, 0),
                         pipeline_mode=pl.Buffered(1)),
            *extra_specs,
        ],
        out_specs=pl.BlockSpec((None, tmd, d), lambda b, i: (b, i, 0)),
        out_shape=jax.ShapeDtypeStruct((bsz, rows_out, d), F32),
        compiler_params=_params("arbitrary", "arbitrary"),
        name=f"ffn{slot}_down",
    )(act, xa, mods, mods, wd, *extra)
    return out, cast


LOG2_E = 1.4426950408889634


def _log2_forget(z, la, lbl):
    ls = jnp.minimum(z, 0.0) - jnp.log(1.0 + jnp.exp(-jnp.abs(z)))
    a = la + ls
    hi = jnp.maximum(a, lbl)
    return (hi + jnp.log(1.0 + jnp.exp(-jnp.abs(a - lbl)))) * LOG2_E


def _inproj_kernel(x_ref, mx_ref, mc_ref, g_ref, w_ref, la_ref, lbl_ref, o_ref, h_sc, vec_sc,
                   *, tm, tn, ctx_len, lf_lo, lf_hi):
    i = pl.program_id(1)
    j = pl.program_id(2)

    @pl.when(j == 0)
    def _():
        _adaln_rows(x_ref, h_sc, vec_sc, g_ref, mx_ref, mc_ref, 3, i, tm, ctx_len)

    h = h_sc[...]

    def project(first_col):
        for cs in _col_pieces(tn, IN_PIECE):
            z = jnp.dot(h, w_ref[:, cs], preferred_element_type=F32)
            lo = None if first_col is None else first_col + cs.start
            if lo is not None and lf_lo <= lo < lf_hi:
                lfc = slice(lo - lf_lo, lo - lf_lo + (cs.stop - cs.start))
                z = _log2_forget(z, la_ref[:, lfc], lbl_ref[:, lfc])
            o_ref[:, cs] = z.astype(o_ref.dtype)

    lf_tiles = range(lf_lo // tn, pl.cdiv(lf_hi, tn))
    for jj in lf_tiles:
        pl.when(j == jj)(functools.partial(project, jj * tn))
    pl.when(jnp.logical_or(j < lf_tiles.start, j >= lf_tiles.stop))(
        functools.partial(project, None))


def _inproj(xa, mods, norm_g, w, la, lbl, *, layer, ctx_len, lf_lo):
    bsz, t, d = xa.shape
    nw = w.shape[-1]
    tm, tn = IN_TM, IN_TN
    lf_w = la.shape[-1]
    ctx_row = bsz
    piece = min(IN_PIECE, tn)
    assert tn % piece == 0 and lf_lo % piece == 0 and lf_w % piece == 0
    kern = functools.partial(_inproj_kernel, tm=tm, tn=tn, ctx_len=ctx_len,
                             lf_lo=lf_lo, lf_hi=lf_lo + lf_w)
    return pl.pallas_call(
        kern,
        grid=(bsz, t // tm, nw // tn),
        in_specs=[
            pl.BlockSpec((None, tm, d), lambda b, i, j: (b, i, 0)),
            pl.BlockSpec((None, None, N_MOD, d), lambda b, i, j: (layer, b, 0, 0)),
            pl.BlockSpec((None, None, N_MOD, d), lambda b, i, j: (layer, ctx_row, 0, 0)),
            pl.BlockSpec((None, None, 1, d), lambda b, i, j: (layer, 1, 0, 0)),
            pl.BlockSpec((None, d, tn), lambda b, i, j: (0, 0, j)),
            pl.BlockSpec((None, 1, lf_w), lambda b, i, j: (layer, 0, 0)),
            pl.BlockSpec((None, 1, lf_w), lambda b, i, j: (layer, 0, 0)),
        ],
        out_specs=pl.BlockSpec((None, tm, tn), lambda b, i, j: (b, i, j)),
        out_shape=jax.ShapeDtypeStruct((bsz, t, nw), BF16),
        scratch_shapes=[pltpu.VMEM((tm, d), BF16), pltpu.VMEM((2, 2, ROW_BLOCK, d), F32)],
        compiler_params=_params("arbitrary", "arbitrary", "arbitrary"),
        name="inproj",
    )(xa, mods, mods, norm_g, w, la, lbl)


def _hgrn_tables(c):
    levels = int(math.log2(c))
    t = np.arange(c)[:, None]
    r = np.arange(c)[None, :]
    sums = np.zeros((2, 2 + levels, c, c), np.float32)
    masks = np.zeros((2, levels + 1, c, c), np.float32)
    masks[:, levels] = np.eye(c)
    sums[0, 0] = r <= t
    sums[0, 1] = r > t
    sums[1, 0] = r >= t
    sums[1, 1] = r < t
    for l in range(levels):
        h = 1 << l
        mid = (t // (2 * h)) * 2 * h + h - 1
        later = (t & h) != 0
        sums[0, 2 + l] = np.where(later, (r > mid) & (r <= t), (r > t) & (r <= mid))
        sums[1, 2 + l] = np.where(later, (r > mid) & (r < t), (r >= t) & (r <= mid))
        same = (t // (2 * h)) == (r // (2 * h))
        masks[0, l] = same & later & ((r & h) == 0)
        masks[1, l] = same & (~later) & ((r & h) != 0)
    return sums.reshape(2, (2 + levels) * c, c), masks


_NT = (((1,), (1,)), ((), ()))
_TN = (((0,), (0,)), ((), ()))


def _hgrn_kernel(q_ref, v_ref, lff_ref, lfb_ref, ga_ref, sums_ref, masks_ref, gn_ref, y_ref,
                 of_sc, ob_sc, e_sc, a_sc, qd_sc, kd_sc, g_sc, *, n_ctx_chunks, n_chunks, scale):
    c = HGRN_CHUNK
    levels = masks_ref.shape[1] - 1
    lf_refs = (lff_ref, lfb_ref)
    o_scs = (of_sc, ob_sc)

    def rows_of(d, p):
        if d == 0:
            ci = p
        else:
            ci = jnp.where(p < n_ctx_chunks, n_ctx_chunks - 1 - p, n_chunks - 1 + n_ctx_chunks - p)
        return pl.ds(pl.multiple_of(ci * c, c), c)

    def stage_sums(p, slot):
        for d in range(2):
            lf = lf_refs[d][rows_of(d, p), :]
            e_sc[d, slot] = jnp.dot(sums_ref[d], lf, preferred_element_type=F32)

    def stage_scores(p, slot):
        for d in range(2):
            rows = rows_of(d, p)
            qb = q_ref[rows, :]
            kb = (1.0 - jnp.exp2(lf_refs[d][rows, :].astype(F32))).astype(BF16)
            qd_sc[d, slot] = qb * jnp.exp2(e_sc[d, slot, 0:c, :]).astype(BF16)
            kd_sc[d, slot] = kb * jnp.exp2(e_sc[d, slot, c:2 * c, :]).astype(BF16)
            kbt = kb.T
            a = jnp.dot(qb, kbt, preferred_element_type=F32).astype(BF16)
            a = a * masks_ref[d, levels]
            for l in range(levels):
                w = jnp.exp2(e_sc[d, slot, (2 + l) * c:(3 + l) * c, :]).astype(BF16)
                p_l = jnp.dot(qb * w, kbt * w.T, preferred_element_type=F32)
                a = a + p_l.astype(BF16) * masks_ref[d, l]
            a_sc[d, slot] = a
            total = (c - 1) if d == 0 else 0
            g_sc[d, slot] = jnp.exp2(e_sc[d, slot, total:total + 1, :])

    def stage_state(p, slot, states):
        new = []
        for d in range(2):
            rows = rows_of(d, p)
            v = v_ref[rows, :]
            st = states[d]
            o = lax.dot_general(qd_sc[d, slot], st.astype(BF16), _NT, preferred_element_type=F32)
            o_scs[d][rows, :] = o + jnp.dot(a_sc[d, slot], v, preferred_element_type=F32)
            new.append(st * g_sc[d, slot] + lax.dot_general(v, kd_sc[d, slot], _TN,
                                                             preferred_element_type=F32))
        return tuple(new)

    def step(j, par, states):
        stage_sums(j, par)
        states = stage_state(j - 2, par, states)
        stage_scores(j - 1, 1 - par)
        return states

    def body(m, states):
        j = 2 + 2 * m
        return step(j + 1, 1, step(j, 0, states))

    assert n_chunks % 2 == 0
    last = n_chunks - 1
    zero = jnp.zeros((A_DV, A_DK), F32)
    stage_sums(0, 0)
    stage_scores(0, 0)
    stage_sums(1, 1)
    states = lax.fori_loop(0, (n_chunks - 2) // 2, body, (zero, zero))
    states = stage_state(last - 1, (last - 1) % 2, states)
    stage_scores(last, last % 2)
    stage_state(last, last % 2, states)

    def readout(j, carry):
        rows = pl.ds(pl.multiple_of(j * c, c), c)
        o = (of_sc[rows, :] + ob_sc[rows, :]) * scale
        ms = jnp.mean(o * o, axis=-1, keepdims=True)
        ga = ga_ref[rows, :].astype(F32)
        y = (o * lax.rsqrt(ms + EPS) * gn_ref[...]) * (ga * _sigmoid(ga))
        y_ref[rows, :] = y.astype(y_ref.dtype)
        return carry

    lax.fori_loop(0, n_chunks, readout, 0, unroll=n_chunks // 2)


def _hgrn(z, sums, masks, gn, *, layer, ctx_len, col_q, col_v, col_ff, col_fb, col_ga):
    bsz, t, _ = z.shape
    c = HGRN_CHUNK
    dk = A_DK
    assert t // c >= 2
    kern = functools.partial(_hgrn_kernel, n_ctx_chunks=ctx_len // c, n_chunks=t // c,
                             scale=A_DK ** -0.5)

    def col(start):
        return pl.BlockSpec((None, t, dk), lambda b, h: (b, 0, start // dk + h))

    return pl.pallas_call(
        kern,
        grid=(bsz, A_HEADS),
        in_specs=[
            col(col_q), col(col_v), col(col_ff), col(col_fb), col(col_ga),
            pl.BlockSpec(sums.shape, lambda b, h: (0, 0, 0)),
            pl.BlockSpec(masks.shape, lambda b, h: (0, 0, 0, 0)),
            pl.BlockSpec((None, 1, A_DV), lambda b, h: (layer, 0, 0)),
        ],
        out_specs=pl.BlockSpec((None, t, A_DV), lambda b, h: (b, 0, h)),
        out_shape=jax.ShapeDtypeStruct((bsz, t, A_HEADS * A_DV), BF16),
        scratch_shapes=[
            pltpu.VMEM((t, A_DV), F32), pltpu.VMEM((t, A_DV), F32),
            pltpu.VMEM((2, 2) + sums.shape[1:], F32),
            pltpu.VMEM((2, 2, c, c), BF16),
            pltpu.VMEM((2, 2, c, dk), BF16),
            pltpu.VMEM((2, 2, c, dk), BF16),
            pltpu.VMEM((2, 2, 1, dk), F32),
        ],
        compiler_params=_params("arbitrary", "arbitrary"),
        name="hgrn",
    )(z, z, z, z, z, sums, masks, gn)


def _mix_kernel(x_ref, ya_ref, *refs, tm, ctx_len, n_piece):
    m_refs = refs[:3 * n_piece]
    (u_ref, v_ref, hc_ref, gb_ref, gc_ref, mx_ref, mc_ref, lng_ref, lnb_ref, ws_ref, bs_ref,
     cw_ref, wpa_ref, wpb_ref, wpc_ref, wo_ref, o_ref, yb_sc, mg_sc) = refs[3 * n_piece:]
    i = pl.program_id(1)
    tile_is_ctx = i * tm < ctx_len

    vg = _gelu_tanh(v_ref[...].astype(F32))
    mu = jnp.mean(vg, axis=-1, keepdims=True)
    dv = vg - mu
    var = jnp.mean(dv * dv, axis=-1, keepdims=True)
    vn = (dv * lax.rsqrt(var + EPS) * lng_ref[...] + lnb_ref[...]).astype(BF16)
    for n in range(tm // B_CHUNK):
        rs = slice(n * B_CHUNK, (n + 1) * B_CHUNK)
        for g in range(B_GROUPS):
            cs = slice(g * B_GROUP_CH, (g + 1) * B_GROUP_CH)
            mixed = jnp.dot(ws_ref[g], vn[rs, cs], preferred_element_type=F32) + bs_ref[g]
            yb_sc[rs, cs] = (_gelu_tanh(u_ref[rs, cs].astype(F32)) * mixed).astype(BF16)

    period = jnp.where(tile_is_ctx, ctx_len, GRID_W)
    pos = (i * tm + lax.broadcasted_iota(jnp.int32, (tm, 1), 0)) & (period - 1)
    tt = gc_ref[...].astype(F32) * hc_ref[...].astype(F32)
    prev = jnp.where(pos == 0, 0.0, pltpu.roll(tt, 1, axis=0))
    nxt = jnp.where(pos == period - 1, 0.0, pltpu.roll(tt, tm - 1, axis=0))
    conv = cw_ref[0:1, :] * prev + cw_ref[1:2, :] * tt + cw_ref[2:3, :] * nxt
    yc = (gb_ref[...].astype(F32) * conv).astype(BF16)

    d = x_ref.shape[-1]
    pw = d // n_piece
    branches = ((ya_ref[...], wpa_ref), (yb_sc[...], wpb_ref), (yc, wpc_ref))
    for k in range(n_piece):
        cs = slice(k * pw, (k + 1) * pw)
        merged2 = None
        for g, (y, w_ref) in enumerate(branches):
            p = jnp.dot(y, w_ref[:, cs], preferred_element_type=F32)
            p = p + jnp.tanh((0.5 * m_refs[g * n_piece + k][...]).astype(F32)) * p
            merged2 = p if merged2 is None else merged2 + p
        mg_sc[:, cs] = merged2.astype(BF16)
    mg = mg_sc[...]
    for k in range(n_piece):
        cs = slice(k * pw, (k + 1) * pw)
        out2 = jnp.dot(mg, wo_ref[:, cs], preferred_element_type=F32)
        gate = jnp.where(tile_is_ctx, mc_ref[5:6, cs], mx_ref[5:6, cs])
        o_ref[:, cs] = x_ref[:, cs] + (0.5 * gate) * out2


def _mix(xa, ya, z, mods, ln_g, ln_b, ws, bs, cw, wpa, wpb, wpc, wo, *, layer, ctx_len, col_m,
         col_s):
    bsz, t, d = xa.shape
    tm = MIX_TM
    assert ctx_len % tm == 0 and tm % B_CHUNK == 0 and tm % GRID_W == 0
    assert ctx_len & (ctx_len - 1) == 0 and GRID_W & (GRID_W - 1) == 0
    bw = B_GROUPS * B_GROUP_CH
    ctx_row = bsz
    pw = min(MIX_PIECE, d)
    n_piece = d // pw
    kern = functools.partial(_mix_kernel, tm=tm, ctx_len=ctx_len, n_piece=n_piece)

    def zcol(start, width):
        assert start % width == 0
        return pl.BlockSpec((None, tm, width), lambda b, i: (b, i, start // width))

    def per_layer(arr, index=layer):
        zeros = (0,) * (arr.ndim - 1)
        return pl.BlockSpec((None,) + arr.shape[1:], lambda b, i: (index,) + zeros,
                            pipeline_mode=pl.Buffered(1))

    return pl.pallas_call(
        kern,
        grid=(bsz, t // tm),
        in_specs=[
            pl.BlockSpec((None, tm, d), lambda b, i: (b, i, 0)),
            pl.BlockSpec((None, tm, ya.shape[2]), lambda b, i: (b, i, 0)),
            *[zcol(col_m + g * d + k * pw, pw) for g in range(3) for k in range(n_piece)],
            zcol(col_s, bw), zcol(col_s + bw, bw), zcol(col_s + 2 * bw, bw),
            zcol(col_s + 3 * bw, bw), zcol(col_s + 4 * bw, bw),
            pl.BlockSpec((None, None, N_MOD, d), lambda b, i: (layer, b, 0, 0)),
            pl.BlockSpec((None, None, N_MOD, d), lambda b, i: (layer, ctx_row, 0, 0)),
            per_layer(ln_g), per_layer(ln_b), per_layer(ws), per_layer(bs), per_layer(cw),
            per_layer(wpa, 0), per_layer(wpb, 0), per_layer(wpc, 0), per_layer(wo, 0),
        ],
        out_specs=pl.BlockSpec((None, tm, d), lambda b, i: (b, i, 0)),
        out_shape=jax.ShapeDtypeStruct(xa.shape, F32),
        scratch_shapes=[pltpu.VMEM((tm, bw), BF16), pltpu.VMEM((tm, d), BF16)],
        compiler_params=_params("arbitrary", "arbitrary"),
        name="mix",
    )(xa, ya, *([z] * (3 * n_piece + 5)), mods, mods, ln_g, ln_b, ws, bs, cw, wpa, wpb, wpc, wo)


def _in_columns():
    wk = A_HEADS * A_DK
    wv = A_HEADS * A_DV
    bw = B_GROUPS * B_GROUP_CH
    s = 3 * wk + 2 * wv
    assert C_WIDTH == bw
    return dict(q=0, v=wk, ff=wk + wv, fb=2 * wk + wv, ga=3 * wk + wv, s=s, m=s + 5 * bw)


def kernel(x, c, ctx, c_ctx, w_mod, b_mod, norm_g, final_norm_g, ffn_w_gate, ffn_w_up,
           ffn_w_down, w_in, hgrn_lb_logits, hgrn_out_norm_g, gmlp_ln_g, gmlp_ln_b,
           gmlp_w_s, gmlp_b_s, conv_w, w_proj_a, w_proj_b, w_proj_c, w_out):
    bsz, _, d = x.shape
    depth = w_mod.shape[0]
    ctx_len = ctx.shape[1]
    assert ctx_len % ROW_BLOCK == 0
    xa = jnp.concatenate([ctx, x], axis=1)

    mod_rows = 8 * (-(-(bsz + 1) // 8))
    cvec = jnp.zeros((mod_rows, d), F32).at[:bsz].set(c).at[bsz].set(c_ctx)
    mods = _modulation(cvec, w_mod, b_mod).reshape(depth, mod_rows, N_MOD, d)

    lb = jnp.cumsum(jax.nn.softmax(hgrn_lb_logits.astype(F32), axis=0), axis=0)
    lb = (lb - lb[:1]).reshape(depth, 1, -1)
    la = jnp.log1p(-lb)
    lbl = jnp.log(lb)

    sums_np, masks_np = _hgrn_tables(HGRN_CHUNK)
    sums = jnp.asarray(sums_np, BF16)
    masks = jnp.asarray(masks_np, BF16)

    cols = _in_columns()
    ws = gmlp_w_s.astype(BF16)
    ffn_set = (ffn_w_gate, ffn_w_up, ffn_w_down)
    mix_set = (w_in, w_proj_a, w_proj_b, w_proj_c, w_out)
    steps = _ffn_steps(bsz, xa.shape[1], ffn_w_gate.shape[-1])
    ffn_views = [w.reshape(depth * steps, CAST_ROWS, -1) for w in ffn_set]
    mix_views = [w.reshape(depth * steps, CAST_ROWS, -1) for w in mix_set]

    def uncast(w, arr):
        return arr.reshape((1,) + w.shape[1:])

    ffn_w = [w[0:1].astype(BF16) for w in ffn_set]
    mix_w = [w[0:1].astype(BF16) for w in mix_set]
    norm_g4 = norm_g.reshape(depth, -1, 1, d)
    gn = hgrn_out_norm_g.reshape(depth, 1, -1)
    ln_g = gmlp_ln_g.reshape(depth, 1, -1)
    ln_b = gmlp_ln_b.reshape(depth, 1, -1)
    bs = gmlp_b_s[..., None]

    for l in range(depth):
        more = l + 1 < depth
        xa, cast = _ffn(xa, mods, norm_g4, *ffn_w, layer=l, slot=0, which=0, ctx_len=ctx_len,
                        casts=[(v, l + 1) for v in ffn_views] if more else ())
        next_ffn_w = [uncast(w, a) for w, a in zip(ffn_set, cast)]
        w_in_l, wpa, wpb, wpc, wo = mix_w
        z = _inproj(xa, mods, norm_g4, w_in_l, la, lbl, layer=l, ctx_len=ctx_len,
                    lf_lo=cols["ff"])
        ya = _hgrn(z, sums, masks, gn, layer=l, ctx_len=ctx_len, col_q=cols["q"],
                   col_v=cols["v"], col_ff=cols["ff"], col_fb=cols["fb"], col_ga=cols["ga"])
        xa = _mix(xa, ya, z, mods, ln_g, ln_b, ws, bs, conv_w, wpa, wpb, wpc, wo, layer=l,
                  ctx_len=ctx_len, col_m=cols["m"], col_s=cols["s"])
        xa, cast = _ffn(xa, mods, norm_g4, *ffn_w, layer=l, slot=2, which=1, ctx_len=ctx_len,
                        casts=[(v, l + 1) for v in mix_views] if more else (),
                        final_g=None if more else final_norm_g)
        if more:
            ffn_w = next_ffn_w
            mix_w = [uncast(w, a) for w, a in zip(mix_set, cast)]
    return xa
```

```python
import functools
import math

import numpy as np
import jax
import jax.numpy as jnp
from jax import lax
from jax.experimental import pallas as pl
from jax.experimental.pallas import tpu as pltpu

F32 = jnp.float32
BF16 = jnp.bfloat16

EPS = 1e-6
N_MOD = 9
GRID_W = 64
A_HEADS = 8
A_DK = 128
A_DV = 128
B_GROUPS = 4
B_GROUP_CH = 128
B_CHUNK = 128
C_WIDTH = 512

VMEM_LIMIT_BYTES = 60 * 1024 * 1024
MXU_COLS = 256

HGRN_CHUNK = 128
ROW_BLOCK = 16
FFN_TM = 544
FFN_TF = 1408
FFN_PIECE = 512
CAST_ROWS = 16
IN_TM = 1088
IN_TN = 1536
MIX_TM = 256
MIX_PIECE = 512
MOD_TN = 1024
FINAL_TM = 256


def _params(*sem):
    return pltpu.CompilerParams(dimension_semantics=sem, vmem_limit_bytes=VMEM_LIMIT_BYTES)


def _sigmoid(a):
    return 0.5 * (jnp.tanh(0.5 * a) + 1.0)


def _gelu_tanh(a):
    return 0.5 * a * (1.0 + jnp.tanh(math.sqrt(2.0 / math.pi) * (a + 0.044715 * (a * a * a))))


def _row_blocks(tile, tm, ctx_len, fn, *, trips):
    def body(r, carry):
        start = pl.multiple_of(r * ROW_BLOCK, ROW_BLOCK)
        fn(pl.ds(start, ROW_BLOCK), tile * tm + start < ctx_len)
        return carry

    n = tm // ROW_BLOCK
    lax.fori_loop(0, n, body, 0, unroll=n // trips if n % trips == 0 else 1)


def _adaln_rows(x_ref, h_ref, vec_sc, g_ref, mx_ref, mc_ref, k_shift, tile, tm, ctx_len,
                zero_ref=None):
    d = x_ref.shape[-1]
    for s, m_ref in enumerate((mx_ref, mc_ref)):
        gain = g_ref[...] * (1.0 + m_ref[k_shift + 1:k_shift + 2, :])
        vec_sc[s, 0] = jnp.broadcast_to(gain, (ROW_BLOCK, d))
        vec_sc[s, 1] = jnp.broadcast_to(m_ref[k_shift:k_shift + 1, :], (ROW_BLOCK, d))

    def block(rows, is_ctx):
        s = is_ctx.astype(jnp.int32)
        x = x_ref[rows, :]
        ms = jnp.mean(x * x, axis=-1, keepdims=True)
        h = (x * lax.rsqrt(ms + EPS)) * vec_sc[s, 0] + vec_sc[s, 1]
        h_ref[rows, :] = h.astype(h_ref.dtype)
        if zero_ref is not None:
            zero_ref[rows, :] = jnp.zeros((ROW_BLOCK, d), zero_ref.dtype)

    _row_blocks(tile, tm, ctx_len, block, trips=2)


def _mod_kernel(c_ref, w_ref, b_ref, o_ref):
    c = c_ref[...]
    s = (c * _sigmoid(c)).astype(BF16)
    o_ref[...] = jnp.dot(s, w_ref[...].astype(BF16), preferred_element_type=F32) + b_ref[...]


def _modulation(cvec, w_mod, b_mod):
    depth, d, nw = w_mod.shape
    rows = cvec.shape[0]
    tn = MOD_TN
    return pl.pallas_call(
        _mod_kernel,
        grid=(depth, nw // tn),
        in_specs=[
            pl.BlockSpec((rows, d), lambda l, j: (0, 0)),
            pl.BlockSpec((None, d, tn), lambda l, j: (l, 0, j)),
            pl.BlockSpec((None, 1, tn), lambda l, j: (l, 0, j)),
        ],
        out_specs=pl.BlockSpec((None, rows, tn), lambda l, j: (l, 0, j)),
        out_shape=jax.ShapeDtypeStruct((depth, rows, nw), F32),
        compiler_params=_params("arbitrary", "arbitrary"),
        name="modulation",
    )(cvec, w_mod, b_mod.reshape(depth, 1, nw))


def _col_pieces(width, piece):
    return [slice(lo, min(lo + piece, width)) for lo in range(0, width, piece)]


def _ffn_up_kernel(x_ref, mx_ref, mc_ref, g_ref, wg_ref, wu_ref, *rest, slot, tm, ctx_len, n_cast):
    cast_src = rest[:n_cast]
    act_ref = rest[n_cast]
    cast_dst = rest[n_cast + 1:2 * n_cast + 1]
    h_sc, vec_sc = rest[2 * n_cast + 1:]
    for src, dst in zip(cast_src, cast_dst):
        dst[...] = src[...].astype(dst.dtype)
    i = pl.program_id(1)

    @pl.when(pl.program_id(2) == 0)
    def _():
        _adaln_rows(x_ref, h_sc, vec_sc, g_ref, mx_ref, mc_ref, 3 * slot, i, tm, ctx_len)

    h = h_sc[...]
    tf = act_ref.shape[-1]
    paired = tf - tf % MXU_COLS
    for cs in _col_pieces(paired, FFN_PIECE):
        a = jnp.dot(h, wg_ref[:, cs], preferred_element_type=F32)
        u = jnp.dot(h, wu_ref[:, cs], preferred_element_type=F32)
        act_ref[:, cs] = ((a * _sigmoid(a)) * u).astype(act_ref.dtype)
    if paired < tf:
        half = tf - paired
        w_tail = jnp.concatenate([wg_ref[:, paired:], wu_ref[:, paired:]], axis=1)
        au = jnp.dot(h, w_tail, preferred_element_type=F32)
        a, u = au[:, :half], au[:, half:]
        act_ref[:, paired:] = ((a * _sigmoid(a)) * u).astype(act_ref.dtype)


def _ffn_down_kernel(act_ref, x_ref, mx_ref, mc_ref, wd_ref, *rest, slot, tm, ctx_len, row0):
    o_ref = rest[-1]
    row = row0 + pl.program_id(1) * tm + lax.broadcasted_iota(jnp.int32, (tm, 1), 0)
    is_ctx = row < ctx_len
    act = act_ref[...]
    k_gate = 3 * slot + 2
    for cs in _col_pieces(o_ref.shape[-1], FFN_PIECE):
        y = jnp.dot(act, wd_ref[:, cs], preferred_element_type=F32)
        gate = jnp.where(is_ctx, mc_ref[k_gate:k_gate + 1, cs], mx_ref[k_gate:k_gate + 1, cs])
        o_ref[:, cs] = x_ref[:, cs] + (0.5 * gate) * y
    if len(rest) == 2:
        y = o_ref[...]
        ms = jnp.mean(y * y, axis=-1, keepdims=True)
        o_ref[...] = y * lax.rsqrt(ms + EPS) * rest[0][...]


def _ffn_steps(bsz, t, f):
    return bsz * (t // FFN_TM) * (f // FFN_TF)


def _ffn(xa, mods, norm_g, wg, wu, wd, *, layer, slot, which, ctx_len, casts=(), final_g=None):
    bsz, t, d = xa.shape
    f = wg.shape[-1]
    tm, tf = FFN_TM, FFN_TF
    nt, nj = t // tm, f // tf
    steps = _ffn_steps(bsz, t, f)
    ctx_row = bsz
    mods_x = pl.BlockSpec((None, None, N_MOD, d), lambda b, i, *_: (layer, b, 0, 0))
    mods_c = pl.BlockSpec((None, None, N_MOD, d), lambda b, i, *_: (layer, ctx_row, 0, 0))

    def cast_specs(src, lsrc):
        lead, rows = src.shape[1:-2], src.shape[-2]
        n_lead = math.prod(lead)
        rb = next(r for r in range(CAST_ROWS, rows + 1, CAST_ROWS)
                  if rows % r == 0 and n_lead * (rows // r) <= steps)
        per_lead = rows // rb
        n_blocks = n_lead * per_lead

        def index(layer_index):
            def index_map(b, i, j):
                blk = (((b * nt + i) * nj + j) * n_blocks) // steps
                where = (blk // per_lead, blk % per_lead) if lead else (blk,)
                return (layer_index,) + where + (0,)
            return index_map

        block = (None,) + (None,) * len(lead) + (rb, src.shape[-1])
        return (pl.BlockSpec(block, index(lsrc)), pl.BlockSpec(block, index(0)),
                jax.ShapeDtypeStruct((1,) + src.shape[1:], BF16))

    cast_in, cast_out, cast_shapes = zip(*[cast_specs(s, l) for s, l in casts]) if casts else (
        (), (), ())
    act, *cast = pl.pallas_call(
        functools.partial(_ffn_up_kernel, slot=slot, tm=tm, ctx_len=ctx_len, n_cast=len(casts)),
        grid=(bsz, nt, nj),
        in_specs=[
            pl.BlockSpec((None, tm, d), lambda b, i, j: (b, i, 0)),
            mods_x, mods_c,
            pl.BlockSpec((None, None, 1, d), lambda b, i, j: (layer, slot, 0, 0)),
            pl.BlockSpec((None, None, d, tf), lambda b, i, j: (0, which, 0, j)),
            pl.BlockSpec((None, None, d, tf), lambda b, i, j: (0, which, 0, j)),
            *cast_in,
        ],
        out_specs=[pl.BlockSpec((None, tm, tf), lambda b, i, j: (b, i, j)), *cast_out],
        out_shape=[jax.ShapeDtypeStruct((bsz, t, f), BF16),
                   *cast_shapes],
        scratch_shapes=[pltpu.VMEM((tm, d), BF16), pltpu.VMEM((2, 2, ROW_BLOCK, d), F32)],
        compiler_params=_params("arbitrary", "arbitrary", "arbitrary"),
        name=f"ffn{slot}_up",
    )(xa, mods, mods, norm_g, wg, wu, *[src for src, _ in casts])
    if final_g is None:
        tmd, skip, rows_out, extra, extra_specs = tm, 0, t, (), []
    else:
        tmd = FINAL_TM
        assert ctx_len % tmd == 0 and (t - ctx_len) % tmd == 0
        skip, rows_out, extra = ctx_len // tmd, t - ctx_len, (final_g.reshape(1, d),)
        extra_specs = [pl.BlockSpec((1, d), lambda b, i: (0, 0))]
    out = pl.pallas_call(
        functools.partial(_ffn_down_kernel, slot=slot, tm=tmd, ctx_len=ctx_len, row0=skip * tmd),
        grid=(bsz, rows_out // tmd),
        in_specs=[
            pl.BlockSpec((None, tmd, f), lambda b, i: (b, i + skip, 0)),
            pl.BlockSpec((None, tmd, d), lambda b, i: (b, i + skip, 0)),
            mods_x, mods_c,
            pl.BlockSpec((None, None, f, d), lambda b, i: (0, which, 0, 0),
                         pipeline_mode=pl.Buffered(1)),
            *extra_specs,
        ],
        out_specs=pl.BlockSpec((None, tmd, d), lambda b, i: (b, i, 0)),
        out_shape=jax.ShapeDtypeStruct((bsz, rows_out, d), F32),
        compiler_params=_params("arbitrary", "arbitrary"),
        name=f"ffn{slot}_down",
    )(act, xa, mods, mods, wd, *extra)
    return out, cast


LOG2_E = 1.4426950408889634


def _log2_forget(z, la, lbl):
    ls = jnp.minimum(z, 0.0) - jnp.log(1.0 + jnp.exp(-jnp.abs(z)))
    a = la + ls
    hi = jnp.maximum(a, lbl)
    return (hi + jnp.log(1.0 + jnp.exp(-jnp.abs(a - lbl)))) * LOG2_E


def _inproj_kernel(x_ref, mx_ref, mc_ref, g_ref, w_ref, la_ref, lbl_ref, o_ref, h_sc, vec_sc,
                   *, tm, tn, ctx_len, lf_lo, lf_hi):
    i = pl.program_id(1)
    j = pl.program_id(2)

    @pl.when(j == 0)
    def _():
        _adaln_rows(x_ref, h_sc, vec_sc, g_ref, mx_ref, mc_ref, 3, i, tm, ctx_len)

    z = jnp.dot(h_sc[...], w_ref[...], preferred_element_type=F32)
    o_ref[...] = z.astype(o_ref.dtype)

    for jj in range(lf_lo // tn, pl.cdiv(lf_hi, tn)):
        lo = max(lf_lo, jj * tn)
        hi = min(lf_hi, (jj + 1) * tn)
        cols = slice(lo - jj * tn, hi - jj * tn)
        lfc = slice(lo - lf_lo, hi - lf_lo)

        @pl.when(j == jj)
        def _(cols=cols, lfc=lfc):
            o_ref[:, cols] = _log2_forget(z[:, cols], la_ref[:, lfc], lbl_ref[:, lfc]).astype(
                o_ref.dtype)


def _inproj(xa, mods, norm_g, w, la, lbl, *, layer, ctx_len, lf_lo):
    bsz, t, d = xa.shape
    nw = w.shape[-1]
    tm, tn = IN_TM, IN_TN
    lf_w = la.shape[-1]
    ctx_row = bsz
    kern = functools.partial(_inproj_kernel, tm=tm, tn=tn, ctx_len=ctx_len,
                             lf_lo=lf_lo, lf_hi=lf_lo + lf_w)
    return pl.pallas_call(
        kern,
        grid=(bsz, t // tm, nw // tn),
        in_specs=[
            pl.BlockSpec((None, tm, d), lambda b, i, j: (b, i, 0)),
            pl.BlockSpec((None, None, N_MOD, d), lambda b, i, j: (layer, b, 0, 0)),
            pl.BlockSpec((None, None, N_MOD, d), lambda b, i, j: (layer, ctx_row, 0, 0)),
            pl.BlockSpec((None, None, 1, d), lambda b, i, j: (layer, 1, 0, 0)),
            pl.BlockSpec((None, d, tn), lambda b, i, j: (0, 0, j)),
            pl.BlockSpec((None, 1, lf_w), lambda b, i, j: (layer, 0, 0)),
            pl.BlockSpec((None, 1, lf_w), lambda b, i, j: (layer, 0, 0)),
        ],
        out_specs=pl.BlockSpec((None, tm, tn), lambda b, i, j: (b, i, j)),
        out_shape=jax.ShapeDtypeStruct((bsz, t, nw), BF16),
        scratch_shapes=[pltpu.VMEM((tm, d), BF16), pltpu.VMEM((2, 2, ROW_BLOCK, d), F32)],
        compiler_params=_params("arbitrary", "arbitrary", "arbitrary"),
        name="inproj",
    )(xa, mods, mods, norm_g, w, la, lbl)


def _hgrn_tables(c):
    levels = int(math.log2(c))
    t = np.arange(c)[:, None]
    r = np.arange(c)[None, :]
    sums = np.zeros((2, 2 + levels, c, c), np.float32)
    masks = np.zeros((2, levels + 1, c, c), np.float32)
    masks[:, levels] = np.eye(c)
    sums[0, 0] = r <= t
    sums[0, 1] = r > t
    sums[1, 0] = r >= t
    sums[1, 1] = r < t
    for l in range(levels):
        h = 1 << l
        mid = (t // (2 * h)) * 2 * h + h - 1
        later = (t & h) != 0
        sums[0, 2 + l] = np.where(later, (r > mid) & (r <= t), (r > t) & (r <= mid))
        sums[1, 2 + l] = np.where(later, (r > mid) & (r < t), (r >= t) & (r <= mid))
        same = (t // (2 * h)) == (r // (2 * h))
        masks[0, l] = same & later & ((r & h) == 0)
        masks[1, l] = same & (~later) & ((r & h) != 0)
    return sums.reshape(2, (2 + levels) * c, c), masks


_NT = (((1,), (1,)), ((), ()))
_TN = (((0,), (0,)), ((), ()))


def _hgrn_kernel(q_ref, v_ref, lff_ref, lfb_ref, ga_ref, sums_ref, masks_ref, gn_ref, y_ref,
                 of_sc, ob_sc, e_sc, a_sc, qd_sc, kd_sc, g_sc, *, n_ctx_chunks, n_chunks, scale):
    c = HGRN_CHUNK
    levels = masks_ref.shape[1] - 1
    lf_refs = (lff_ref, lfb_ref)
    o_scs = (of_sc, ob_sc)

    def rows_of(d, p):
        if d == 0:
            ci = p
        else:
            ci = jnp.where(p < n_ctx_chunks, n_ctx_chunks - 1 - p, n_chunks - 1 + n_ctx_chunks - p)
        return pl.ds(pl.multiple_of(ci * c, c), c)

    def stage_sums(p, slot):
        for d in range(2):
            lf = lf_refs[d][rows_of(d, p), :]
            e_sc[d, slot] = jnp.dot(sums_ref[d], lf, preferred_element_type=F32)

    def stage_scores(p, slot):
        for d in range(2):
            rows = rows_of(d, p)
            qb = q_ref[rows, :]
            kb = (1.0 - jnp.exp2(lf_refs[d][rows, :].astype(F32))).astype(BF16)
            qd_sc[d, slot] = qb * jnp.exp2(e_sc[d, slot, 0:c, :]).astype(BF16)
            kd_sc[d, slot] = kb * jnp.exp2(e_sc[d, slot, c:2 * c, :]).astype(BF16)
            kbt = kb.T
            a = jnp.dot(qb, kbt, preferred_element_type=F32).astype(BF16)
            a = a * masks_ref[d, levels]
            for l in range(levels):
                w = jnp.exp2(e_sc[d, slot, (2 + l) * c:(3 + l) * c, :]).astype(BF16)
                p_l = jnp.dot(qb * w, kbt * w.T, preferred_element_type=F32)
                a = a + p_l.astype(BF16) * masks_ref[d, l]
            a_sc[d, slot] = a
            total = (c - 1) if d == 0 else 0
            g_sc[d, slot] = jnp.exp2(e_sc[d, slot, total:total + 1, :])

    def stage_state(p, slot, states):
        new = []
        for d in range(2):
            rows = rows_of(d, p)
            v = v_ref[rows, :]
            st = states[d]
            o = lax.dot_general(qd_sc[d, slot], st.astype(BF16), _NT, preferred_element_type=F32)
            o_scs[d][rows, :] = o + jnp.dot(a_sc[d, slot], v, preferred_element_type=F32)
            new.append(st * g_sc[d, slot] + lax.dot_general(v, kd_sc[d, slot], _TN,
                                                             preferred_element_type=F32))
        return tuple(new)

    def step(j, par, states):
        stage_sums(j, par)
        states = stage_state(j - 2, par, states)
        stage_scores(j - 1, 1 - par)
        return states

    def body(m, states):
        j = 2 + 2 * m
        return step(j + 1, 1, step(j, 0, states))

    assert n_chunks % 2 == 0
    last = n_chunks - 1
    zero = jnp.zeros((A_DV, A_DK), F32)
    stage_sums(0, 0)
    stage_scores(0, 0)
    stage_sums(1, 1)
    states = lax.fori_loop(0, (n_chunks - 2) // 2, body, (zero, zero))
    states = stage_state(last - 1, (last - 1) % 2, states)
    stage_scores(last, last % 2)
    stage_state(last, last % 2, states)

    def readout(j, carry):
        rows = pl.ds(pl.multiple_of(j * c, c), c)
        o = (of_sc[rows, :] + ob_sc[rows, :]) * scale
        ms = jnp.mean(o * o, axis=-1, keepdims=True)
        ga = ga_ref[rows, :].astype(F32)
        y = (o * lax.rsqrt(ms + EPS) * gn_ref[...]) * (ga * _sigmoid(ga))
        y_ref[rows, :] = y.astype(y_ref.dtype)
        return carry

    lax.fori_loop(0, n_chunks, readout, 0, unroll=n_chunks // 2)


def _hgrn(z, sums, masks, gn, *, layer, ctx_len, col_q, col_v, col_ff, col_fb, col_ga):
    bsz, t, _ = z.shape
    c = HGRN_CHUNK
    dk = A_DK
    assert t // c >= 2
    kern = functools.partial(_hgrn_kernel, n_ctx_chunks=ctx_len // c, n_chunks=t // c,
                             scale=A_DK ** -0.5)

    def col(start):
        return pl.BlockSpec((None, t, dk), lambda b, h: (b, 0, start // dk + h))

    return pl.pallas_call(
        kern,
        grid=(bsz, A_HEADS),
        in_specs=[
            col(col_q), col(col_v), col(col_ff), col(col_fb), col(col_ga),
            pl.BlockSpec(sums.shape, lambda b, h: (0, 0, 0)),
            pl.BlockSpec(masks.shape, lambda b, h: (0, 0, 0, 0)),
            pl.BlockSpec((None, 1, A_DV), lambda b, h: (layer, 0, 0)),
        ],
        out_specs=pl.BlockSpec((None, t, A_DV), lambda b, h: (b, 0, h)),
        out_shape=jax.ShapeDtypeStruct((bsz, t, A_HEADS * A_DV), BF16),
        scratch_shapes=[
            pltpu.VMEM((t, A_DV), F32), pltpu.VMEM((t, A_DV), F32),
            pltpu.VMEM((2, 2) + sums.shape[1:], F32),
            pltpu.VMEM((2, 2, c, c), BF16),
            pltpu.VMEM((2, 2, c, dk), BF16),
            pltpu.VMEM((2, 2, c, dk), BF16),
            pltpu.VMEM((2, 2, 1, dk), F32),
        ],
        compiler_params=_params("arbitrary", "arbitrary"),
        name="hgrn",
    )(z, z, z, z, z, sums, masks, gn)


def _mix_kernel(x_ref, ya_ref, *refs, tm, ctx_len, n_piece):
    m_refs = refs[:3 * n_piece]
    (u_ref, v_ref, hc_ref, gb_ref, gc_ref, mx_ref, mc_ref, lng_ref, lnb_ref, ws_ref, bs_ref,
     cw_ref, wpa_ref, wpb_ref, wpc_ref, wo_ref, o_ref, yb_sc, mg_sc) = refs[3 * n_piece:]
    i = pl.program_id(1)
    tile_is_ctx = i * tm < ctx_len

    vg = _gelu_tanh(v_ref[...].astype(F32))
    mu = jnp.mean(vg, axis=-1, keepdims=True)
    dv = vg - mu
    var = jnp.mean(dv * dv, axis=-1, keepdims=True)
    vn = (dv * lax.rsqrt(var + EPS) * lng_ref[...] + lnb_ref[...]).astype(BF16)
    for n in range(tm // B_CHUNK):
        rs = slice(n * B_CHUNK, (n + 1) * B_CHUNK)
        for g in range(B_GROUPS):
            cs = slice(g * B_GROUP_CH, (g + 1) * B_GROUP_CH)
            mixed = jnp.dot(ws_ref[g], vn[rs, cs], preferred_element_type=F32) + bs_ref[g]
            yb_sc[rs, cs] = (_gelu_tanh(u_ref[rs, cs].astype(F32)) * mixed).astype(BF16)

    period = jnp.where(tile_is_ctx, ctx_len, GRID_W)
    pos = (i * tm + lax.broadcasted_iota(jnp.int32, (tm, 1), 0)) & (period - 1)
    tt = gc_ref[...].astype(F32) * hc_ref[...].astype(F32)
    prev = jnp.where(pos == 0, 0.0, pltpu.roll(tt, 1, axis=0))
    nxt = jnp.where(pos == period - 1, 0.0, pltpu.roll(tt, tm - 1, axis=0))
    conv = cw_ref[0:1, :] * prev + cw_ref[1:2, :] * tt + cw_ref[2:3, :] * nxt
    yc = (gb_ref[...].astype(F32) * conv).astype(BF16)

    d = x_ref.shape[-1]
    pw = d // n_piece
    branches = ((ya_ref[...], wpa_ref), (yb_sc[...], wpb_ref), (yc, wpc_ref))
    for k in range(n_piece):
        cs = slice(k * pw, (k + 1) * pw)
        merged2 = None
        for g, (y, w_ref) in enumerate(branches):
            p = jnp.dot(y, w_ref[:, cs], preferred_element_type=F32)
            p = p + jnp.tanh((0.5 * m_refs[g * n_piece + k][...]).astype(F32)) * p
            merged2 = p if merged2 is None else merged2 + p
        mg_sc[:, cs] = merged2.astype(BF16)
    mg = mg_sc[...]
    for k in range(n_piece):
        cs = slice(k * pw, (k + 1) * pw)
        out2 = jnp.dot(mg, wo_ref[:, cs], preferred_element_type=F32)
        gate = jnp.where(tile_is_ctx, mc_ref[5:6, cs], mx_ref[5:6, cs])
        o_ref[:, cs] = x_ref[:, cs] + (0.5 * gate) * out2


def _mix(xa, ya, z, mods, ln_g, ln_b, ws, bs, cw, wpa, wpb, wpc, wo, *, layer, ctx_len, col_m,
         col_s):
    bsz, t, d = xa.shape
    tm = MIX_TM
    assert ctx_len % tm == 0 and tm % B_CHUNK == 0 and tm % GRID_W == 0
    assert ctx_len & (ctx_len - 1) == 0 and GRID_W & (GRID_W - 1) == 0
    bw = B_GROUPS * B_GROUP_CH
    ctx_row = bsz
    pw = min(MIX_PIECE, d)
    n_piece = d // pw
    kern = functools.partial(_mix_kernel, tm=tm, ctx_len=ctx_len, n_piece=n_piece)

    def zcol(start, width):
        assert start % width == 0
        return pl.BlockSpec((None, tm, width), lambda b, i: (b, i, start // width))

    def per_layer(arr, index=layer):
        zeros = (0,) * (arr.ndim - 1)
        return pl.BlockSpec((None,) + arr.shape[1:], lambda b, i: (index,) + zeros,
                            pipeline_mode=pl.Buffered(1))

    return pl.pallas_call(
        kern,
        grid=(bsz, t // tm),
        in_specs=[
            pl.BlockSpec((None, tm, d), lambda b, i: (b, i, 0)),
            pl.BlockSpec((None, tm, ya.shape[2]), lambda b, i: (b, i, 0)),
            *[zcol(col_m + g * d + k * pw, pw) for g in range(3) for k in range(n_piece)],
            zcol(col_s, bw), zcol(col_s + bw, bw), zcol(col_s + 2 * bw, bw),
            zcol(col_s + 3 * bw, bw), zcol(col_s + 4 * bw, bw),
            pl.BlockSpec((None, None, N_MOD, d), lambda b, i: (layer, b, 0, 0)),
            pl.BlockSpec((None, None, N_MOD, d), lambda b, i: (layer, ctx_row, 0, 0)),
            per_layer(ln_g), per_layer(ln_b), per_layer(ws), per_layer(bs), per_layer(cw),
            per_layer(wpa, 0), per_layer(wpb, 0), per_layer(wpc, 0), per_layer(wo, 0),
        ],
        out_specs=pl.BlockSpec((None, tm, d), lambda b, i: (b, i, 0)),
        out_shape=jax.ShapeDtypeStruct(xa.shape, F32),
        scratch_shapes=[pltpu.VMEM((tm, bw), BF16), pltpu.VMEM((tm, d), BF16)],
        compiler_params=_params("arbitrary", "arbitrary"),
        name="mix",
    )(xa, ya, *([z] * (3 * n_piece + 5)), mods, mods, ln_g, ln_b, ws, bs, cw, wpa, wpb, wpc, wo)


def _in_columns():
    wk = A_HEADS * A_DK
    wv = A_HEADS * A_DV
    bw = B_GROUPS * B_GROUP_CH
    s = 3 * wk + 2 * wv
    assert C_WIDTH == bw
    return dict(q=0, v=wk, ff=wk + wv, fb=2 * wk + wv, ga=3 * wk + wv, s=s, m=s + 5 * bw)


def kernel(x, c, ctx, c_ctx, w_mod, b_mod, norm_g, final_norm_g, ffn_w_gate, ffn_w_up,
           ffn_w_down, w_in, hgrn_lb_logits, hgrn_out_norm_g, gmlp_ln_g, gmlp_ln_b,
           gmlp_w_s, gmlp_b_s, conv_w, w_proj_a, w_proj_b, w_proj_c, w_out):
    bsz, _, d = x.shape
    depth = w_mod.shape[0]
    ctx_len = ctx.shape[1]
    assert ctx_len % ROW_BLOCK == 0
    xa = jnp.concatenate([ctx, x], axis=1)

    mod_rows = 8 * (-(-(bsz + 1) // 8))
    cvec = jnp.zeros((mod_rows, d), F32).at[:bsz].set(c).at[bsz].set(c_ctx)
    mods = _modulation(cvec, w_mod, b_mod).reshape(depth, mod_rows, N_MOD, d)

    lb = jnp.cumsum(jax.nn.softmax(hgrn_lb_logits.astype(F32), axis=0), axis=0)
    lb = (lb - lb[:1]).reshape(depth, 1, -1)
    la = jnp.log1p(-lb)
    lbl = jnp.log(lb)

    sums_np, masks_np = _hgrn_tables(HGRN_CHUNK)
    sums = jnp.asarray(sums_np, BF16)
    masks = jnp.asarray(masks_np, BF16)

    cols = _in_columns()
    ws = gmlp_w_s.astype(BF16)
    ffn_set = (ffn_w_gate, ffn_w_up, ffn_w_down)
    mix_set = (w_in, w_proj_a, w_proj_b, w_proj_c, w_out)
    ffn_w = [w[0:1].astype(BF16) for w in ffn_set]
    mix_w = [w[0:1].astype(BF16) for w in mix_set]
    norm_g4 = norm_g.reshape(depth, -1, 1, d)
    gn = hgrn_out_norm_g.reshape(depth, 1, -1)
    ln_g = gmlp_ln_g.reshape(depth, 1, -1)
    ln_b = gmlp_ln_b.reshape(depth, 1, -1)
    bs = gmlp_b_s[..., None]

    for l in range(depth):
        more = l + 1 < depth
        xa, next_ffn_w = _ffn(xa, mods, norm_g4, *ffn_w, layer=l, slot=0, which=0,
                              ctx_len=ctx_len,
                              casts=[(w, l + 1) for w in ffn_set] if more else ())
        w_in_l, wpa, wpb, wpc, wo = mix_w
        z = _inproj(xa, mods, norm_g4, w_in_l, la, lbl, layer=l, ctx_len=ctx_len,
                    lf_lo=cols["ff"])
        ya = _hgrn(z, sums, masks, gn, layer=l, ctx_len=ctx_len, col_q=cols["q"],
                   col_v=cols["v"], col_ff=cols["ff"], col_fb=cols["fb"], col_ga=cols["ga"])
        xa = _mix(xa, ya, z, mods, ln_g, ln_b, ws, bs, conv_w, wpa, wpb, wpc, wo, layer=l,
                  ctx_len=ctx_len, col_m=cols["m"], col_s=cols["s"])
        xa, next_mix_w = _ffn(xa, mods, norm_g4, *ffn_w, layer=l, slot=2, which=1,
                              ctx_len=ctx_len,
                              casts=[(w, l + 1) for w in mix_set] if more else (),
                              final_g=None if more else final_norm_g)
        if more:
            ffn_w, mix_w = next_ffn_w, next_mix_w
    return xa
```

```python
import functools
import math

import numpy as np
import jax
import jax.numpy as jnp
from jax import lax
from jax.experimental import pallas as pl
from jax.experimental.pallas import tpu as pltpu

F32 = jnp.float32
BF16 = jnp.bfloat16

EPS = 1e-6
N_MOD = 9
GRID_W = 64
A_HEADS = 8
A_DK = 128
A_DV = 128
B_GROUPS = 4
B_GROUP_CH = 128
B_CHUNK = 128
C_WIDTH = 512

VMEM_LIMIT_BYTES = 60 * 1024 * 1024
MXU_COLS = 256

HGRN_CHUNK = 128
HGRN_MXU_LEVELS = 3
ROW_BLOCK = 16
FFN_TM = 544
FFN_TF = 1408
FFN_PIECE = 512
CAST_ROWS = 16
IN_TM = 1088
IN_TN = 1536
MIX_TM = 256
MIX_PIECE = 512
MOD_TN = 1024
FINAL_TM = 256


def _params(*sem):
    return pltpu.CompilerParams(dimension_semantics=sem, vmem_limit_bytes=VMEM_LIMIT_BYTES)


def _sigmoid(a):
    return 0.5 * (jnp.tanh(0.5 * a) + 1.0)


def _gelu_tanh(a):
    return 0.5 * a * (1.0 + jnp.tanh(math.sqrt(2.0 / math.pi) * (a + 0.044715 * (a * a * a))))


def _row_blocks(tile, tm, ctx_len, fn, *, trips):
    def body(r, carry):
        start = pl.multiple_of(r * ROW_BLOCK, ROW_BLOCK)
        fn(pl.ds(start, ROW_BLOCK), tile * tm + start < ctx_len)
        return carry

    n = tm // ROW_BLOCK
    lax.fori_loop(0, n, body, 0, unroll=n // trips if n % trips == 0 else 1)


def _adaln_rows(x_ref, h_ref, vec_sc, g_ref, mx_ref, mc_ref, k_shift, tile, tm, ctx_len,
                zero_ref=None):
    d = x_ref.shape[-1]
    for s, m_ref in enumerate((mx_ref, mc_ref)):
        gain = g_ref[...] * (1.0 + m_ref[k_shift + 1:k_shift + 2, :])
        vec_sc[s, 0] = jnp.broadcast_to(gain, (ROW_BLOCK, d))
        vec_sc[s, 1] = jnp.broadcast_to(m_ref[k_shift:k_shift + 1, :], (ROW_BLOCK, d))

    def block(rows, is_ctx):
        s = is_ctx.astype(jnp.int32)
        x = x_ref[rows, :]
        ms = jnp.mean(x * x, axis=-1, keepdims=True)
        h = (x * lax.rsqrt(ms + EPS)) * vec_sc[s, 0] + vec_sc[s, 1]
        h_ref[rows, :] = h.astype(h_ref.dtype)
        if zero_ref is not None:
            zero_ref[rows, :] = jnp.zeros((ROW_BLOCK, d), zero_ref.dtype)

    _row_blocks(tile, tm, ctx_len, block, trips=2)


def _mod_kernel(c_ref, w_ref, b_ref, o_ref):
    c = c_ref[...]
    s = (c * _sigmoid(c)).astype(BF16)
    o_ref[...] = jnp.dot(s, w_ref[...].astype(BF16), preferred_element_type=F32) + b_ref[...]


def _modulation(cvec, w_mod, b_mod):
    depth, d, nw = w_mod.shape
    rows = cvec.shape[0]
    tn = MOD_TN
    return pl.pallas_call(
        _mod_kernel,
        grid=(depth, nw // tn),
        in_specs=[
            pl.BlockSpec((rows, d), lambda l, j: (0, 0)),
            pl.BlockSpec((None, d, tn), lambda l, j: (l, 0, j)),
            pl.BlockSpec((None, 1, tn), lambda l, j: (l, 0, j)),
        ],
        out_specs=pl.BlockSpec((None, rows, tn), lambda l, j: (l, 0, j)),
        out_shape=jax.ShapeDtypeStruct((depth, rows, nw), F32),
        compiler_params=_params("arbitrary", "arbitrary"),
        name="modulation",
    )(cvec, w_mod, b_mod.reshape(depth, 1, nw))


def _col_pieces(width, piece):
    return [slice(lo, min(lo + piece, width)) for lo in range(0, width, piece)]


def _ffn_up_kernel(x_ref, mx_ref, mc_ref, g_ref, wg_ref, wu_ref, *rest, slot, tm, ctx_len, n_cast):
    cast_src = rest[:n_cast]
    act_ref = rest[n_cast]
    cast_dst = rest[n_cast + 1:2 * n_cast + 1]
    h_sc, vec_sc = rest[2 * n_cast + 1:]
    i = pl.program_id(1)

    @pl.when(pl.program_id(2) == 0)
    def _():
        _adaln_rows(x_ref, h_sc, vec_sc, g_ref, mx_ref, mc_ref, 3 * slot, i, tm, ctx_len)

    h = h_sc[...]
    tf = act_ref.shape[-1]
    paired = tf - tf % MXU_COLS
    for cs in _col_pieces(paired, FFN_PIECE):
        a = jnp.dot(h, wg_ref[:, cs], preferred_element_type=F32)
        u = jnp.dot(h, wu_ref[:, cs], preferred_element_type=F32)
        act_ref[:, cs] = ((a * _sigmoid(a)) * u).astype(act_ref.dtype)
    if paired < tf:
        half = tf - paired
        w_tail = jnp.concatenate([wg_ref[:, paired:], wu_ref[:, paired:]], axis=1)
        au = jnp.dot(h, w_tail, preferred_element_type=F32)
        a, u = au[:, :half], au[:, half:]
        act_ref[:, paired:] = ((a * _sigmoid(a)) * u).astype(act_ref.dtype)
    for src, dst in zip(cast_src, cast_dst):
        dst[...] = src[...].astype(dst.dtype)


def _ffn_down_kernel(act_ref, x_ref, mx_ref, mc_ref, wd_ref, *rest, slot, tm, ctx_len, row0):
    o_ref = rest[-1]
    row = row0 + pl.program_id(1) * tm + lax.broadcasted_iota(jnp.int32, (tm, 1), 0)
    is_ctx = row < ctx_len
    act = act_ref[...]
    k_gate = 3 * slot + 2
    for cs in _col_pieces(o_ref.shape[-1], FFN_PIECE):
        y = jnp.dot(act, wd_ref[:, cs], preferred_element_type=F32)
        gate = jnp.where(is_ctx, mc_ref[k_gate:k_gate + 1, cs], mx_ref[k_gate:k_gate + 1, cs])
        o_ref[:, cs] = x_ref[:, cs] + (0.5 * gate) * y
    if len(rest) == 2:
        y = o_ref[...]
        ms = jnp.mean(y * y, axis=-1, keepdims=True)
        o_ref[...] = y * lax.rsqrt(ms + EPS) * rest[0][...]


def _ffn_steps(bsz, t, f):
    return bsz * (t // FFN_TM) * (f // FFN_TF)


def _ffn(xa, mods, norm_g, wg, wu, wd, *, layer, slot, which, ctx_len, casts=(), final_g=None):
    bsz, t, d = xa.shape
    f = wg.shape[-1]
    tm, tf = FFN_TM, FFN_TF
    nt, nj = t // tm, f // tf
    steps = _ffn_steps(bsz, t, f)
    ctx_row = bsz
    mods_x = pl.BlockSpec((None, None, N_MOD, d), lambda b, i, *_: (layer, b, 0, 0))
    mods_c = pl.BlockSpec((None, None, N_MOD, d), lambda b, i, *_: (layer, ctx_row, 0, 0))

    def cast_specs(src, lsrc):
        lead, rows = src.shape[1:-2], src.shape[-2]
        n_lead = math.prod(lead)
        rb = next(r for r in range(CAST_ROWS, rows + 1, CAST_ROWS)
                  if rows % r == 0 and n_lead * (rows // r) <= steps)
        per_lead = rows // rb
        n_blocks = n_lead * per_lead

        def index(layer_index):
            def index_map(b, i, j):
                blk = (((b * nt + i) * nj + j) * n_blocks) // steps
                where = (blk // per_lead, blk % per_lead) if lead else (blk,)
                return (layer_index,) + where + (0,)
            return index_map

        block = (None,) + (None,) * len(lead) + (rb, src.shape[-1])
        return (pl.BlockSpec(block, index(lsrc)), pl.BlockSpec(block, index(0)),
                jax.ShapeDtypeStruct((1,) + src.shape[1:], BF16))

    cast_in, cast_out, cast_shapes = zip(*[cast_specs(s, l) for s, l in casts]) if casts else (
        (), (), ())
    act, *cast = pl.pallas_call(
        functools.partial(_ffn_up_kernel, slot=slot, tm=tm, ctx_len=ctx_len, n_cast=len(casts)),
        grid=(bsz, nt, nj),
        in_specs=[
            pl.BlockSpec((None, tm, d), lambda b, i, j: (b, i, 0)),
            mods_x, mods_c,
            pl.BlockSpec((None, None, 1, d), lambda b, i, j: (layer, slot, 0, 0)),
            pl.BlockSpec((None, None, d, tf), lambda b, i, j: (0, which, 0, j)),
            pl.BlockSpec((None, None, d, tf), lambda b, i, j: (0, which, 0, j)),
            *cast_in,
        ],
        out_specs=[pl.BlockSpec((None, tm, tf), lambda b, i, j: (b, i, j)), *cast_out],
        out_shape=[jax.ShapeDtypeStruct((bsz, t, f), BF16),
                   *cast_shapes],
        scratch_shapes=[pltpu.VMEM((tm, d), BF16), pltpu.VMEM((2, 2, ROW_BLOCK, d), F32)],
        compiler_params=_params("arbitrary", "arbitrary", "arbitrary"),
        name=f"ffn{slot}_up",
    )(xa, mods, mods, norm_g, wg, wu, *[src for src, _ in casts])
    if final_g is None:
        tmd, skip, rows_out, extra, extra_specs = tm, 0, t, (), []
    else:
        tmd = FINAL_TM
        assert ctx_len % tmd == 0 and (t - ctx_len) % tmd == 0
        skip, rows_out, extra = ctx_len // tmd, t - ctx_len, (final_g.reshape(1, d),)
        extra_specs = [pl.BlockSpec((1, d), lambda b, i: (0, 0))]
    out = pl.pallas_call(
        functools.partial(_ffn_down_kernel, slot=slot, tm=tmd, ctx_len=ctx_len, row0=skip * tmd),
        grid=(bsz, rows_out // tmd),
        in_specs=[
            pl.BlockSpec((None, tmd, f), lambda b, i: (b, i + skip, 0)),
            pl.BlockSpec((None, tmd, d), lambda b, i: (b, i + skip, 0)),
            mods_x, mods_c,
            pl.BlockSpec((None, None, f, d), lambda b, i: (0, which, 0, 0),
                         pipeline_mode=pl.Buffered(1)),
            *extra_specs,
        ],
        out_specs=pl.BlockSpec((None, tmd, d), lambda b, i: (b, i, 0)),
        out_shape=jax.ShapeDtypeStruct((bsz, rows_out, d), F32),
        compiler_params=_params("arbitrary", "arbitrary"),
        name=f"ffn{slot}_down",
    )(act, xa, mods, mods, wd, *extra)
    return out, cast


LOG2_E = 1.4426950408889634


def _log2_forget(z, la, lbl):
    ls = jnp.minimum(z, 0.0) - jnp.log(1.0 + jnp.exp(-jnp.abs(z)))
    a = la + ls
    hi = jnp.maximum(a, lbl)
    return (hi + jnp.log(1.0 + jnp.exp(-jnp.abs(a - lbl)))) * LOG2_E


def _inproj_kernel(x_ref, mx_ref, mc_ref, g_ref, w_ref, la_ref, lbl_ref, o_ref, h_sc, vec_sc,
                   *, tm, tn, ctx_len, lf_lo, lf_hi):
    i = pl.program_id(1)
    j = pl.program_id(2)

    @pl.when(j == 0)
    def _():
        _adaln_rows(x_ref, h_sc, vec_sc, g_ref, mx_ref, mc_ref, 3, i, tm, ctx_len)

    z = jnp.dot(h_sc[...], w_ref[...], preferred_element_type=F32)
    o_ref[...] = z.astype(o_ref.dtype)

    for jj in range(lf_lo // tn, pl.cdiv(lf_hi, tn)):
        lo = max(lf_lo, jj * tn)
        hi = min(lf_hi, (jj + 1) * tn)
        cols = slice(lo - jj * tn, hi - jj * tn)
        lfc = slice(lo - lf_lo, hi - lf_lo)

        @pl.when(j == jj)
        def _(cols=cols, lfc=lfc):
            o_ref[:, cols] = _log2_forget(z[:, cols], la_ref[:, lfc], lbl_ref[:, lfc]).astype(
                o_ref.dtype)


def _inproj(xa, mods, norm_g, w, la, lbl, *, layer, ctx_len, lf_lo):
    bsz, t, d = xa.shape
    nw = w.shape[-1]
    tm, tn = IN_TM, IN_TN
    lf_w = la.shape[-1]
    ctx_row = bsz
    kern = functools.partial(_inproj_kernel, tm=tm, tn=tn, ctx_len=ctx_len,
                             lf_lo=lf_lo, lf_hi=lf_lo + lf_w)
    return pl.pallas_call(
        kern,
        grid=(bsz, t // tm, nw // tn),
        in_specs=[
            pl.BlockSpec((None, tm, d), lambda b, i, j: (b, i, 0)),
            pl.BlockSpec((None, None, N_MOD, d), lambda b, i, j: (layer, b, 0, 0)),
            pl.BlockSpec((None, None, N_MOD, d), lambda b, i, j: (layer, ctx_row, 0, 0)),
            pl.BlockSpec((None, None, 1, d), lambda b, i, j: (layer, 1, 0, 0)),
            pl.BlockSpec((None, d, tn), lambda b, i, j: (0, 0, j)),
            pl.BlockSpec((None, 1, lf_w), lambda b, i, j: (layer, 0, 0)),
            pl.BlockSpec((None, 1, lf_w), lambda b, i, j: (layer, 0, 0)),
        ],
        out_specs=pl.BlockSpec((None, tm, tn), lambda b, i, j: (b, i, j)),
        out_shape=jax.ShapeDtypeStruct((bsz, t, nw), BF16),
        scratch_shapes=[pltpu.VMEM((tm, d), BF16), pltpu.VMEM((2, 2, ROW_BLOCK, d), F32)],
        compiler_params=_params("arbitrary", "arbitrary", "arbitrary"),
        name="inproj",
    )(xa, mods, mods, norm_g, w, la, lbl)


def _hgrn_tables(c):
    levels = int(math.log2(c))
    t = np.arange(c)[:, None]
    r = np.arange(c)[None, :]
    masks = np.zeros((2, levels + 1, c, c), np.float32)
    masks[:, levels] = np.eye(c)
    sums = np.zeros((2, 1 + HGRN_MXU_LEVELS, c, c), np.float32)
    sums[0, 0] = r <= t
    sums[1, 0] = r >= t
    for l in range(levels):
        h = 1 << l
        mid = (t // (2 * h)) * 2 * h + h - 1
        later = (t & h) != 0
        if l < HGRN_MXU_LEVELS:
            sums[0, 1 + l] = np.where(later, (r > mid) & (r <= t), (r > t) & (r <= mid))
            sums[1, 1 + l] = np.where(later, (r > mid) & (r < t), (r >= t) & (r <= mid))
        same = (t // (2 * h)) == (r // (2 * h))
        masks[0, l] = same & later & ((r & h) == 0)
        masks[1, l] = same & (~later) & ((r & h) != 0)
    return sums.reshape(2, (1 + HGRN_MXU_LEVELS) * c, c), masks


def _level_weight(l, d, g):
    c, dk = g.shape
    h = 1 << l
    assert h % 8 == 0
    mid = h - 1 if d == 0 else h
    g3 = g.reshape(c // (2 * h), 2 * h, dk)
    gm = g3[:, mid:mid + 1, :]
    ge, gl = g3[:, :h, :], g3[:, h:, :]
    early, late = (gm - ge, gl - gm) if d == 0 else (ge - gm, gm - gl)
    return jnp.exp2(jnp.concatenate([early, late], axis=1)).reshape(c, dk)


_NT = (((1,), (1,)), ((), ()))
_TN = (((0,), (0,)), ((), ()))


def _hgrn_kernel(q_ref, v_ref, lff_ref, lfb_ref, ga_ref, sums_ref, masks_ref, gn_ref, y_ref,
                 of_sc, ob_sc, e_sc, a_sc, qd_sc, kd_sc, g_sc, *, n_ctx_chunks, n_chunks, scale):
    c = HGRN_CHUNK
    levels = masks_ref.shape[1] - 1
    lf_refs = (lff_ref, lfb_ref)
    o_scs = (of_sc, ob_sc)

    def rows_of(d, p):
        if d == 0:
            ci = p
        else:
            ci = jnp.where(p < n_ctx_chunks, n_ctx_chunks - 1 - p, n_chunks - 1 + n_ctx_chunks - p)
        return pl.ds(pl.multiple_of(ci * c, c), c)

    def stage_sums(p, slot):
        for d in range(2):
            lf = lf_refs[d][rows_of(d, p), :]
            e_sc[d, slot] = jnp.dot(sums_ref[d], lf, preferred_element_type=F32)

    def stage_scores(p, slot):
        for d in range(2):
            rows = rows_of(d, p)
            qb = q_ref[rows, :]
            kb = (1.0 - jnp.exp2(lf_refs[d][rows, :].astype(F32))).astype(BF16)
            g = e_sc[d, slot, 0:c, :]
            total = (c - 1) if d == 0 else 0
            g_tot = g[total:total + 1, :]
            qd_sc[d, slot] = qb * jnp.exp2(g).astype(BF16)
            kd_sc[d, slot] = kb * jnp.exp2(g_tot - g).astype(BF16)
            g_sc[d, slot] = jnp.exp2(g_tot)
            kbt = kb.T
            a = jnp.dot(qb, kbt, preferred_element_type=F32).astype(BF16)
            a = a * masks_ref[d, levels]
            for l in range(levels):
                if l < HGRN_MXU_LEVELS:
                    w = jnp.exp2(e_sc[d, slot, (1 + l) * c:(2 + l) * c, :])
                else:
                    w = _level_weight(l, d, g)
                w = w.astype(BF16)
                p_l = jnp.dot(qb * w, kbt * w.T, preferred_element_type=F32)
                a = a + p_l.astype(BF16) * masks_ref[d, l]
            a_sc[d, slot] = a

    def stage_state(p, slot, states):
        new = []
        for d in range(2):
            rows = rows_of(d, p)
            v = v_ref[rows, :]
            st = states[d]
            o = lax.dot_general(qd_sc[d, slot], st.astype(BF16), _NT, preferred_element_type=F32)
            o_scs[d][rows, :] = o + jnp.dot(a_sc[d, slot], v, preferred_element_type=F32)
            new.append(st * g_sc[d, slot] + lax.dot_general(v, kd_sc[d, slot], _TN,
                                                             preferred_element_type=F32))
        return tuple(new)

    def step(j, par, states):
        stage_sums(j, par)
        states = stage_state(j - 2, par, states)
        stage_scores(j - 1, 1 - par)
        return states

    def body(m, states):
        j = 2 + 2 * m
        return step(j + 1, 1, step(j, 0, states))

    assert n_chunks % 2 == 0
    last = n_chunks - 1
    zero = jnp.zeros((A_DV, A_DK), F32)
    stage_sums(0, 0)
    stage_scores(0, 0)
    stage_sums(1, 1)
    states = lax.fori_loop(0, (n_chunks - 2) // 2, body, (zero, zero))
    states = stage_state(last - 1, (last - 1) % 2, states)
    stage_scores(last, last % 2)
    stage_state(last, last % 2, states)

    def readout(j, carry):
        rows = pl.ds(pl.multiple_of(j * c, c), c)
        o = (of_sc[rows, :] + ob_sc[rows, :]) * scale
        ms = jnp.mean(o * o, axis=-1, keepdims=True)
        ga = ga_ref[rows, :].astype(F32)
        y = (o * lax.rsqrt(ms + EPS) * gn_ref[...]) * (ga * _sigmoid(ga))
        y_ref[rows, :] = y.astype(y_ref.dtype)
        return carry

    lax.fori_loop(0, n_chunks, readout, 0, unroll=n_chunks // 2)


def _hgrn(z, sums, masks, gn, *, layer, ctx_len, col_q, col_v, col_ff, col_fb, col_ga):
    bsz, t, _ = z.shape
    c = HGRN_CHUNK
    dk = A_DK
    assert t // c >= 2
    kern = functools.partial(_hgrn_kernel, n_ctx_chunks=ctx_len // c, n_chunks=t // c,
                             scale=A_DK ** -0.5)

    def col(start):
        return pl.BlockSpec((None, t, dk), lambda b, h: (b, 0, start // dk + h))

    return pl.pallas_call(
        kern,
        grid=(bsz, A_HEADS),
        in_specs=[
            col(col_q), col(col_v), col(col_ff), col(col_fb), col(col_ga),
            pl.BlockSpec(sums.shape, lambda b, h: (0, 0, 0)),
            pl.BlockSpec(masks.shape, lambda b, h: (0, 0, 0, 0)),
            pl.BlockSpec((None, 1, A_DV), lambda b, h: (layer, 0, 0)),
        ],
        out_specs=pl.BlockSpec((None, t, A_DV), lambda b, h: (b, 0, h)),
        out_shape=jax.ShapeDtypeStruct((bsz, t, A_HEADS * A_DV), BF16),
        scratch_shapes=[
            pltpu.VMEM((t, A_DV), F32), pltpu.VMEM((t, A_DV), F32),
            pltpu.VMEM((2, 2) + sums.shape[1:], F32),
            pltpu.VMEM((2, 2, c, c), BF16),
            pltpu.VMEM((2, 2, c, dk), BF16),
            pltpu.VMEM((2, 2, c, dk), BF16),
            pltpu.VMEM((2, 2, 1, dk), F32),
        ],
        compiler_params=_params("arbitrary", "arbitrary"),
        name="hgrn",
    )(z, z, z, z, z, sums, masks, gn)


def _mix_kernel(x_ref, ya_ref, *refs, tm, ctx_len, n_piece):
    m_refs = refs[:3 * n_piece]
    (u_ref, v_ref, hc_ref, gb_ref, gc_ref, mx_ref, mc_ref, lng_ref, lnb_ref, ws_ref, bs_ref,
     cw_ref, wpa_ref, wpb_ref, wpc_ref, wo_ref, o_ref, yb_sc, mg_sc) = refs[3 * n_piece:]
    i = pl.program_id(1)
    tile_is_ctx = i * tm < ctx_len

    vg = _gelu_tanh(v_ref[...].astype(F32))
    mu = jnp.mean(vg, axis=-1, keepdims=True)
    dv = vg - mu
    var = jnp.mean(dv * dv, axis=-1, keepdims=True)
    vn = (dv * lax.rsqrt(var + EPS) * lng_ref[...] + lnb_ref[...]).astype(BF16)
    for n in range(tm // B_CHUNK):
        rs = slice(n * B_CHUNK, (n + 1) * B_CHUNK)
        for g in range(B_GROUPS):
            cs = slice(g * B_GROUP_CH, (g + 1) * B_GROUP_CH)
            mixed = jnp.dot(ws_ref[g], vn[rs, cs], preferred_element_type=F32) + bs_ref[g]
            yb_sc[rs, cs] = (_gelu_tanh(u_ref[rs, cs].astype(F32)) * mixed).astype(BF16)

    period = jnp.where(tile_is_ctx, ctx_len, GRID_W)
    pos = (i * tm + lax.broadcasted_iota(jnp.int32, (tm, 1), 0)) & (period - 1)
    tt = gc_ref[...].astype(F32) * hc_ref[...].astype(F32)
    prev = jnp.where(pos == 0, 0.0, pltpu.roll(tt, 1, axis=0))
    nxt = jnp.where(pos == period - 1, 0.0, pltpu.roll(tt, tm - 1, axis=0))
    conv = cw_ref[0:1, :] * prev + cw_ref[1:2, :] * tt + cw_ref[2:3, :] * nxt
    yc = (gb_ref[...].astype(F32) * conv).astype(BF16)

    d = x_ref.shape[-1]
    pw = d // n_piece
    branches = ((ya_ref[...], wpa_ref), (yb_sc[...], wpb_ref), (yc, wpc_ref))
    for k in range(n_piece):
        cs = slice(k * pw, (k + 1) * pw)
        merged2 = None
        for g, (y, w_ref) in enumerate(branches):
            p = jnp.dot(y, w_ref[:, cs], preferred_element_type=F32)
            p = p + jnp.tanh((0.5 * m_refs[g * n_piece + k][...]).astype(F32)) * p
            merged2 = p if merged2 is None else merged2 + p
        mg_sc[:, cs] = merged2.astype(BF16)
    mg = mg_sc[...]
    for k in range(n_piece):
        cs = slice(k * pw, (k + 1) * pw)
        out2 = jnp.dot(mg, wo_ref[:, cs], preferred_element_type=F32)
        gate = jnp.where(tile_is_ctx, mc_ref[5:6, cs], mx_ref[5:6, cs])
        o_ref[:, cs] = x_ref[:, cs] + (0.5 * gate) * out2


def _mix(xa, ya, z, mods, ln_g, ln_b, ws, bs, cw, wpa, wpb, wpc, wo, *, layer, ctx_len, col_m,
         col_s):
    bsz, t, d = xa.shape
    tm = MIX_TM
    assert ctx_len % tm == 0 and tm % B_CHUNK == 0 and tm % GRID_W == 0
    assert ctx_len & (ctx_len - 1) == 0 and GRID_W & (GRID_W - 1) == 0
    bw = B_GROUPS * B_GROUP_CH
    ctx_row = bsz
    pw = min(MIX_PIECE, d)
    n_piece = d // pw
    kern = functools.partial(_mix_kernel, tm=tm, ctx_len=ctx_len, n_piece=n_piece)

    def zcol(start, width):
        assert start % width == 0
        return pl.BlockSpec((None, tm, width), lambda b, i: (b, i, start // width))

    def per_layer(arr, index=layer):
        zeros = (0,) * (arr.ndim - 1)
        return pl.BlockSpec((None,) + arr.shape[1:], lambda b, i: (index,) + zeros,
                            pipeline_mode=pl.Buffered(1))

    return pl.pallas_call(
        kern,
        grid=(bsz, t // tm),
        in_specs=[
            pl.BlockSpec((None, tm, d), lambda b, i: (b, i, 0)),
            pl.BlockSpec((None, tm, ya.shape[2]), lambda b, i: (b, i, 0)),
            *[zcol(col_m + g * d + k * pw, pw) for g in range(3) for k in range(n_piece)],
            zcol(col_s, bw), zcol(col_s + bw, bw), zcol(col_s + 2 * bw, bw),
            zcol(col_s + 3 * bw, bw), zcol(col_s + 4 * bw, bw),
            pl.BlockSpec((None, None, N_MOD, d), lambda b, i: (layer, b, 0, 0)),
            pl.BlockSpec((None, None, N_MOD, d), lambda b, i: (layer, ctx_row, 0, 0)),
            per_layer(ln_g), per_layer(ln_b), per_layer(ws), per_layer(bs), per_layer(cw),
            per_layer(wpa, 0), per_layer(wpb, 0), per_layer(wpc, 0), per_layer(wo, 0),
        ],
        out_specs=pl.BlockSpec((None, tm, d), lambda b, i: (b, i, 0)),
        out_shape=jax.ShapeDtypeStruct(xa.shape, F32),
        scratch_shapes=[pltpu.VMEM((tm, bw), BF16), pltpu.VMEM((tm, d), BF16)],
        compiler_params=_params("arbitrary", "arbitrary"),
        name="mix",
    )(xa, ya, *([z] * (3 * n_piece + 5)), mods, mods, ln_g, ln_b, ws, bs, cw, wpa, wpb, wpc, wo)


def _in_columns():
    wk = A_HEADS * A_DK
    wv = A_HEADS * A_DV
    bw = B_GROUPS * B_GROUP_CH
    s = 3 * wk + 2 * wv
    assert C_WIDTH == bw
    return dict(q=0, v=wk, ff=wk + wv, fb=2 * wk + wv, ga=3 * wk + wv, s=s, m=s + 5 * bw)


def kernel(x, c, ctx, c_ctx, w_mod, b_mod, norm_g, final_norm_g, ffn_w_gate, ffn_w_up,
           ffn_w_down, w_in, hgrn_lb_logits, hgrn_out_norm_g, gmlp_ln_g, gmlp_ln_b,
           gmlp_w_s, gmlp_b_s, conv_w, w_proj_a, w_proj_b, w_proj_c, w_out):
    bsz, _, d = x.shape
    depth = w_mod.shape[0]
    ctx_len = ctx.shape[1]
    assert ctx_len % ROW_BLOCK == 0
    xa = jnp.concatenate([ctx, x], axis=1)

    mod_rows = 8 * (-(-(bsz + 1) // 8))
    cvec = jnp.zeros((mod_rows, d), F32).at[:bsz].set(c).at[bsz].set(c_ctx)
    mods = _modulation(cvec, w_mod, b_mod).reshape(depth, mod_rows, N_MOD, d)

    lb = jnp.cumsum(jax.nn.softmax(hgrn_lb_logits.astype(F32), axis=0), axis=0)
    lb = (lb - lb[:1]).reshape(depth, 1, -1)
    la = jnp.log1p(-lb)
    lbl = jnp.log(lb)

    sums_np, masks_np = _hgrn_tables(HGRN_CHUNK)
    sums = jnp.asarray(sums_np, BF16)
    masks = jnp.asarray(masks_np, BF16)

    cols = _in_columns()
    ws = gmlp_w_s.astype(BF16)
    ffn_set = (ffn_w_gate, ffn_w_up, ffn_w_down)
    mix_set = (w_in, w_proj_a, w_proj_b, w_proj_c, w_out)
    ffn_w = [w[0:1].astype(BF16) for w in ffn_set]
    mix_w = [w[0:1].astype(BF16) for w in mix_set]
    norm_g4 = norm_g.reshape(depth, -1, 1, d)
    gn = hgrn_out_norm_g.reshape(depth, 1, -1)
    ln_g = gmlp_ln_g.reshape(depth, 1, -1)
    ln_b = gmlp_ln_b.reshape(depth, 1, -1)
    bs = gmlp_b_s[..., None]

    for l in range(depth):
        more = l + 1 < depth
        xa, next_ffn_w = _ffn(xa, mods, norm_g4, *ffn_w, layer=l, slot=0, which=0,
                              ctx_len=ctx_len,
                              casts=[(w, l + 1) for w in ffn_set] if more else ())
        w_in_l, wpa, wpb, wpc, wo = mix_w
        z = _inproj(xa, mods, norm_g4, w_in_l, la, lbl, layer=l, ctx_len=ctx_len,
                    lf_lo=cols["ff"])
        ya = _hgrn(z, sums, masks, gn, layer=l, ctx_len=ctx_len, col_q=cols["q"],
                   col_v=cols["v"], col_ff=cols["ff"], col_fb=cols["fb"], col_ga=cols["ga"])
        xa = _mix(xa, ya, z, mods, ln_g, ln_b, ws, bs, conv_w, wpa, wpb, wpc, wo, layer=l,
                  ctx_len=ctx_len, col_m=cols["m"], col_s=cols["s"])
        xa, next_mix_w = _ffn(xa, mods, norm_g4, *ffn_w, layer=l, slot=2, which=1,
                              ctx_len=ctx_len,
                              casts=[(w, l + 1) for w in mix_set] if more else (),
                              final_g=None if more else final_norm_g)
        if more:
            ffn_w, mix_w = next_ffn_w, next_mix_w
    return xa
```

```python
import functools
import math

import numpy as np
import jax
import jax.numpy as jnp
from jax import lax
from jax.experimental import pallas as pl
from jax.experimental.pallas import tpu as pltpu

F32 = jnp.float32
BF16 = jnp.bfloat16

EPS = 1e-6
N_MOD = 9
GRID_W = 64
A_HEADS = 8
A_DK = 128
A_DV = 128
B_GROUPS = 4
B_GROUP_CH = 128
B_CHUNK = 128
C_WIDTH = 512

VMEM_LIMIT_BYTES = 60 * 1024 * 1024
MXU_COLS = 256

HGRN_CHUNK = 128
HGRN_MXU_LEVELS = 3
ROW_BLOCK = 16
FFN_TM = 544
FFN_TF = 1408
FFN_PIECE = 512
CAST_ROWS = 16
IN_TM = 1088
IN_TN = 1536
MIX_TM = 256
MIX_PIECE = 512
MOD_TN = 1024
FINAL_TM = 256


def _params(*sem):
    return pltpu.CompilerParams(dimension_semantics=sem, vmem_limit_bytes=VMEM_LIMIT_BYTES)


def _sigmoid(a):
    return 0.5 * (jnp.tanh(0.5 * a) + 1.0)


def _gelu_tanh(a):
    return 0.5 * a * (1.0 + jnp.tanh(math.sqrt(2.0 / math.pi) * (a + 0.044715 * (a * a * a))))


def _row_blocks(tile, tm, ctx_len, fn, *, trips):
    def body(r, carry):
        start = pl.multiple_of(r * ROW_BLOCK, ROW_BLOCK)
        fn(pl.ds(start, ROW_BLOCK), tile * tm + start < ctx_len)
        return carry

    n = tm // ROW_BLOCK
    lax.fori_loop(0, n, body, 0, unroll=n // trips if n % trips == 0 else 1)


def _adaln_rows(x_ref, h_ref, vec_sc, g_ref, mx_ref, mc_ref, k_shift, tile, tm, ctx_len,
                zero_ref=None):
    d = x_ref.shape[-1]
    for s, m_ref in enumerate((mx_ref, mc_ref)):
        gain = g_ref[...] * (1.0 + m_ref[k_shift + 1:k_shift + 2, :])
        vec_sc[s, 0] = jnp.broadcast_to(gain, (ROW_BLOCK, d))
        vec_sc[s, 1] = jnp.broadcast_to(m_ref[k_shift:k_shift + 1, :], (ROW_BLOCK, d))

    def block(rows, is_ctx):
        s = is_ctx.astype(jnp.int32)
        x = x_ref[rows, :]
        ms = jnp.mean(x * x, axis=-1, keepdims=True)
        h = (x * lax.rsqrt(ms + EPS)) * vec_sc[s, 0] + vec_sc[s, 1]
        h_ref[rows, :] = h.astype(h_ref.dtype)
        if zero_ref is not None:
            zero_ref[rows, :] = jnp.zeros((ROW_BLOCK, d), zero_ref.dtype)

    _row_blocks(tile, tm, ctx_len, block, trips=2)


def _mod_kernel(c_ref, w_ref, b_ref, o_ref):
    c = c_ref[...]
    s = (c * _sigmoid(c)).astype(BF16)
    o_ref[...] = jnp.dot(s, w_ref[...].astype(BF16), preferred_element_type=F32) + b_ref[...]


def _modulation(cvec, w_mod, b_mod):
    depth, d, nw = w_mod.shape
    rows = cvec.shape[0]
    tn = MOD_TN
    return pl.pallas_call(
        _mod_kernel,
        grid=(depth, nw // tn),
        in_specs=[
            pl.BlockSpec((rows, d), lambda l, j: (0, 0)),
            pl.BlockSpec((None, d, tn), lambda l, j: (l, 0, j)),
            pl.BlockSpec((None, 1, tn), lambda l, j: (l, 0, j)),
        ],
        out_specs=pl.BlockSpec((None, rows, tn), lambda l, j: (l, 0, j)),
        out_shape=jax.ShapeDtypeStruct((depth, rows, nw), F32),
        compiler_params=_params("arbitrary", "arbitrary"),
        name="modulation",
    )(cvec, w_mod, b_mod.reshape(depth, 1, nw))


def _col_pieces(width, piece):
    return [slice(lo, min(lo + piece, width)) for lo in range(0, width, piece)]


def _ffn_up_kernel(x_ref, mx_ref, mc_ref, g_ref, wg_ref, wu_ref, *rest, slot, tm, ctx_len, n_cast):
    cast_src = rest[:n_cast]
    act_ref = rest[n_cast]
    cast_dst = rest[n_cast + 1:2 * n_cast + 1]
    h_sc, vec_sc = rest[2 * n_cast + 1:]
    i = pl.program_id(1)

    @pl.when(pl.program_id(2) == 0)
    def _():
        _adaln_rows(x_ref, h_sc, vec_sc, g_ref, mx_ref, mc_ref, 3 * slot, i, tm, ctx_len)

    h = h_sc[...]
    tf = act_ref.shape[-1]
    paired = tf - tf % MXU_COLS
    for cs in _col_pieces(paired, FFN_PIECE):
        a = jnp.dot(h, wg_ref[:, cs], preferred_element_type=F32)
        u = jnp.dot(h, wu_ref[:, cs], preferred_element_type=F32)
        act_ref[:, cs] = ((a * _sigmoid(a)) * u).astype(act_ref.dtype)
    if paired < tf:
        half = tf - paired
        w_tail = jnp.concatenate([wg_ref[:, paired:], wu_ref[:, paired:]], axis=1)
        au = jnp.dot(h, w_tail, preferred_element_type=F32)
        a, u = au[:, :half], au[:, half:]
        act_ref[:, paired:] = ((a * _sigmoid(a)) * u).astype(act_ref.dtype)
    for src, dst in zip(cast_src, cast_dst):
        dst[...] = src[...].astype(dst.dtype)


def _ffn_down_kernel(act_ref, x_ref, mx_ref, mc_ref, wd_ref, *rest, slot, tm, ctx_len, row0):
    o_ref = rest[-1]
    row = row0 + pl.program_id(1) * tm + lax.broadcasted_iota(jnp.int32, (tm, 1), 0)
    is_ctx = row < ctx_len
    act = act_ref[...]
    k_gate = 3 * slot + 2
    for cs in _col_pieces(o_ref.shape[-1], FFN_PIECE):
        y = jnp.dot(act, wd_ref[:, cs], preferred_element_type=F32)
        gate = jnp.where(is_ctx, mc_ref[k_gate:k_gate + 1, cs], mx_ref[k_gate:k_gate + 1, cs])
        o_ref[:, cs] = x_ref[:, cs] + (0.5 * gate) * y
    if len(rest) == 2:
        y = o_ref[...]
        ms = jnp.mean(y * y, axis=-1, keepdims=True)
        o_ref[...] = y * lax.rsqrt(ms + EPS) * rest[0][...]


def _ffn_steps(bsz, t, f):
    return bsz * (t // FFN_TM) * (f // FFN_TF)


def _ffn(xa, mods, norm_g, wg, wu, wd, *, layer, slot, which, ctx_len, casts=(), final_g=None):
    bsz, t, d = xa.shape
    f = wg.shape[-1]
    tm, tf = FFN_TM, FFN_TF
    nt, nj = t // tm, f // tf
    steps = _ffn_steps(bsz, t, f)
    ctx_row = bsz
    mods_x = pl.BlockSpec((None, None, N_MOD, d), lambda b, i, *_: (layer, b, 0, 0))
    mods_c = pl.BlockSpec((None, None, N_MOD, d), lambda b, i, *_: (layer, ctx_row, 0, 0))

    def cast_specs(src, lsrc):
        lead, rows = src.shape[1:-2], src.shape[-2]
        n_lead = math.prod(lead)
        rb = next(r for r in range(CAST_ROWS, rows + 1, CAST_ROWS)
                  if rows % r == 0 and n_lead * (rows // r) <= steps)
        per_lead = rows // rb
        n_blocks = n_lead * per_lead

        def index(layer_index):
            def index_map(b, i, j):
                blk = (((b * nt + i) * nj + j) * n_blocks) // steps
                where = (blk // per_lead, blk % per_lead) if lead else (blk,)
                return (layer_index,) + where + (0,)
            return index_map

        block = (None,) + (None,) * len(lead) + (rb, src.shape[-1])
        return (pl.BlockSpec(block, index(lsrc)), pl.BlockSpec(block, index(0)),
                jax.ShapeDtypeStruct((1,) + src.shape[1:], BF16))

    cast_in, cast_out, cast_shapes = zip(*[cast_specs(s, l) for s, l in casts]) if casts else (
        (), (), ())
    act, *cast = pl.pallas_call(
        functools.partial(_ffn_up_kernel, slot=slot, tm=tm, ctx_len=ctx_len, n_cast=len(casts)),
        grid=(bsz, nt, nj),
        in_specs=[
            pl.BlockSpec((None, tm, d), lambda b, i, j: (b, i, 0)),
            mods_x, mods_c,
            pl.BlockSpec((None, None, 1, d), lambda b, i, j: (layer, slot, 0, 0)),
            pl.BlockSpec((None, None, d, tf), lambda b, i, j: (0, which, 0, j)),
            pl.BlockSpec((None, None, d, tf), lambda b, i, j: (0, which, 0, j)),
            *cast_in,
        ],
        out_specs=[pl.BlockSpec((None, tm, tf), lambda b, i, j: (b, i, j)), *cast_out],
        out_shape=[jax.ShapeDtypeStruct((bsz, t, f), BF16),
                   *cast_shapes],
        scratch_shapes=[pltpu.VMEM((tm, d), BF16), pltpu.VMEM((2, 2, ROW_BLOCK, d), F32)],
        compiler_params=_params("arbitrary", "arbitrary", "arbitrary"),
        name=f"ffn{slot}_up",
    )(xa, mods, mods, norm_g, wg, wu, *[src for src, _ in casts])
    if final_g is None:
        tmd, skip, rows_out, extra, extra_specs = tm, 0, t, (), []
    else:
        tmd = FINAL_TM
        assert ctx_len % tmd == 0 and (t - ctx_len) % tmd == 0
        skip, rows_out, extra = ctx_len // tmd, t - ctx_len, (final_g.reshape(1, d),)
        extra_specs = [pl.BlockSpec((1, d), lambda b, i: (0, 0))]
    out = pl.pallas_call(
        functools.partial(_ffn_down_kernel, slot=slot, tm=tmd, ctx_len=ctx_len, row0=skip * tmd),
        grid=(bsz, rows_out // tmd),
        in_specs=[
            pl.BlockSpec((None, tmd, f), lambda b, i: (b, i + skip, 0)),
            pl.BlockSpec((None, tmd, d), lambda b, i: (b, i + skip, 0)),
            mods_x, mods_c,
            pl.BlockSpec((None, None, f, d), lambda b, i: (0, which, 0, 0),
                         pipeline_mode=pl.Buffered(1)),
            *extra_specs,
        ],
        out_specs=pl.BlockSpec((None, tmd, d), lambda b, i: (b, i, 0)),
        out_shape=jax.ShapeDtypeStruct((bsz, rows_out, d), F32),
        compiler_params=_params("arbitrary", "arbitrary"),
        name=f"ffn{slot}_down",
    )(act, xa, mods, mods, wd, *extra)
    return out, cast


F32_TINY = 2.0 ** -126


def _log2_forget(z, c0, c1):
    return jnp.log2(jnp.maximum(c0 + c1 * jnp.tanh(0.5 * z), F32_TINY))


def _inproj_kernel(x_ref, mx_ref, mc_ref, g_ref, w_ref, c0_ref, c1_ref, o_ref, h_sc, vec_sc,
                   *, tm, tn, ctx_len, lf_lo, lf_hi):
    i = pl.program_id(1)
    j = pl.program_id(2)

    @pl.when(j == 0)
    def _():
        _adaln_rows(x_ref, h_sc, vec_sc, g_ref, mx_ref, mc_ref, 3, i, tm, ctx_len)

    z = jnp.dot(h_sc[...], w_ref[...], preferred_element_type=F32)
    o_ref[...] = z.astype(o_ref.dtype)

    for jj in range(lf_lo // tn, pl.cdiv(lf_hi, tn)):
        lo = max(lf_lo, jj * tn)
        hi = min(lf_hi, (jj + 1) * tn)
        cols = slice(lo - jj * tn, hi - jj * tn)
        lfc = slice(lo - lf_lo, hi - lf_lo)

        @pl.when(j == jj)
        def _(cols=cols, lfc=lfc):
            o_ref[:, cols] = _log2_forget(z[:, cols], c0_ref[:, lfc], c1_ref[:, lfc]).astype(
                o_ref.dtype)


def _inproj(xa, mods, norm_g, w, c0, c1, *, layer, ctx_len, lf_lo):
    bsz, t, d = xa.shape
    nw = w.shape[-1]
    tm, tn = IN_TM, IN_TN
    lf_w = c0.shape[-1]
    ctx_row = bsz
    kern = functools.partial(_inproj_kernel, tm=tm, tn=tn, ctx_len=ctx_len,
                             lf_lo=lf_lo, lf_hi=lf_lo + lf_w)
    return pl.pallas_call(
        kern,
        grid=(bsz, t // tm, nw // tn),
        in_specs=[
            pl.BlockSpec((None, tm, d), lambda b, i, j: (b, i, 0)),
            pl.BlockSpec((None, None, N_MOD, d), lambda b, i, j: (layer, b, 0, 0)),
            pl.BlockSpec((None, None, N_MOD, d), lambda b, i, j: (layer, ctx_row, 0, 0)),
            pl.BlockSpec((None, None, 1, d), lambda b, i, j: (layer, 1, 0, 0)),
            pl.BlockSpec((None, d, tn), lambda b, i, j: (0, 0, j)),
            pl.BlockSpec((None, 1, lf_w), lambda b, i, j: (layer, 0, 0)),
            pl.BlockSpec((None, 1, lf_w), lambda b, i, j: (layer, 0, 0)),
        ],
        out_specs=pl.BlockSpec((None, tm, tn), lambda b, i, j: (b, i, j)),
        out_shape=jax.ShapeDtypeStruct((bsz, t, nw), BF16),
        scratch_shapes=[pltpu.VMEM((tm, d), BF16), pltpu.VMEM((2, 2, ROW_BLOCK, d), F32)],
        compiler_params=_params("arbitrary", "arbitrary", "arbitrary"),
        name="inproj",
    )(xa, mods, mods, norm_g, w, c0, c1)


def _hgrn_tables(c):
    levels = int(math.log2(c))
    t = np.arange(c)[:, None]
    r = np.arange(c)[None, :]
    masks = np.zeros((2, levels + 1, c, c), np.float32)
    masks[:, levels] = np.eye(c)
    sums = np.zeros((2, 1 + HGRN_MXU_LEVELS, c, c), np.float32)
    sums[0, 0] = r <= t
    sums[1, 0] = r >= t
    for l in range(levels):
        h = 1 << l
        mid = (t // (2 * h)) * 2 * h + h - 1
        later = (t & h) != 0
        if l < HGRN_MXU_LEVELS:
            sums[0, 1 + l] = np.where(later, (r > mid) & (r <= t), (r > t) & (r <= mid))
            sums[1, 1 + l] = np.where(later, (r > mid) & (r < t), (r >= t) & (r <= mid))
        same = (t // (2 * h)) == (r // (2 * h))
        masks[0, l] = same & later & ((r & h) == 0)
        masks[1, l] = same & (~later) & ((r & h) != 0)
    return sums.reshape(2, (1 + HGRN_MXU_LEVELS) * c, c), masks


def _level_weight(l, d, g):
    c, dk = g.shape
    h = 1 << l
    assert h % 8 == 0
    mid = h - 1 if d == 0 else h
    g3 = g.reshape(c // (2 * h), 2 * h, dk)
    gm = g3[:, mid:mid + 1, :]
    ge, gl = g3[:, :h, :], g3[:, h:, :]
    early, late = (gm - ge, gl - gm) if d == 0 else (ge - gm, gm - gl)
    return jnp.exp2(jnp.concatenate([early, late], axis=1)).reshape(c, dk)


_NT = (((1,), (1,)), ((), ()))
_TN = (((0,), (0,)), ((), ()))


def _hgrn_kernel(q_ref, v_ref, lff_ref, lfb_ref, ga_ref, sums_ref, masks_ref, gn_ref, y_ref,
                 of_sc, ob_sc, e_sc, a_sc, qd_sc, kd_sc, g_sc, *, n_ctx_chunks, n_chunks, scale):
    c = HGRN_CHUNK
    levels = masks_ref.shape[1] - 1
    lf_refs = (lff_ref, lfb_ref)
    o_scs = (of_sc, ob_sc)

    def rows_of(d, p):
        if d == 0:
            ci = p
        else:
            ci = jnp.where(p < n_ctx_chunks, n_ctx_chunks - 1 - p, n_chunks - 1 + n_ctx_chunks - p)
        return pl.ds(pl.multiple_of(ci * c, c), c)

    def stage_sums(p, slot):
        for d in range(2):
            lf = lf_refs[d][rows_of(d, p), :]
            e_sc[d, slot] = jnp.dot(sums_ref[d], lf, preferred_element_type=F32)

    def stage_scores(p, slot):
        for d in range(2):
            rows = rows_of(d, p)
            qb = q_ref[rows, :]
            kb = (1.0 - jnp.exp2(lf_refs[d][rows, :].astype(F32))).astype(BF16)
            g = e_sc[d, slot, 0:c, :]
            total = (c - 1) if d == 0 else 0
            g_tot = g[total:total + 1, :]
            qd_sc[d, slot] = qb * jnp.exp2(g).astype(BF16)
            kd_sc[d, slot] = kb * jnp.exp2(g_tot - g).astype(BF16)
            g_sc[d, slot] = jnp.exp2(g_tot)
            kbt = kb.T
            a = jnp.dot(qb, kbt, preferred_element_type=F32).astype(BF16)
            a = a * masks_ref[d, levels]
            for l in range(levels):
                if l < HGRN_MXU_LEVELS:
                    w = jnp.exp2(e_sc[d, slot, (1 + l) * c:(2 + l) * c, :])
                else:
                    w = _level_weight(l, d, g)
                w = w.astype(BF16)
                p_l = jnp.dot(qb * w, kbt * w.T, preferred_element_type=F32)
                a = a + p_l.astype(BF16) * masks_ref[d, l]
            a_sc[d, slot] = a

    def stage_state(p, slot, states):
        new = []
        for d in range(2):
            rows = rows_of(d, p)
            v = v_ref[rows, :]
            st = states[d]
            o = lax.dot_general(qd_sc[d, slot], st.astype(BF16), _NT, preferred_element_type=F32)
            o_scs[d][rows, :] = o + jnp.dot(a_sc[d, slot], v, preferred_element_type=F32)
            new.append(st * g_sc[d, slot] + lax.dot_general(v, kd_sc[d, slot], _TN,
                                                             preferred_element_type=F32))
        return tuple(new)

    def step(j, par, states):
        stage_sums(j, par)
        states = stage_state(j - 2, par, states)
        stage_scores(j - 1, 1 - par)
        return states

    def body(m, states):
        j = 2 + 2 * m
        return step(j + 1, 1, step(j, 0, states))

    assert n_chunks % 2 == 0
    last = n_chunks - 1
    zero = jnp.zeros((A_DV, A_DK), F32)
    stage_sums(0, 0)
    stage_scores(0, 0)
    stage_sums(1, 1)
    states = lax.fori_loop(0, (n_chunks - 2) // 2, body, (zero, zero))
    states = stage_state(last - 1, (last - 1) % 2, states)
    stage_scores(last, last % 2)
    stage_state(last, last % 2, states)

    def readout(j, carry):
        rows = pl.ds(pl.multiple_of(j * c, c), c)
        o = (of_sc[rows, :] + ob_sc[rows, :]) * scale
        ms = jnp.mean(o * o, axis=-1, keepdims=True)
        ga = ga_ref[rows, :].astype(F32)
        y = (o * lax.rsqrt(ms + EPS) * gn_ref[...]) * (ga * _sigmoid(ga))
        y_ref[rows, :] = y.astype(y_ref.dtype)
        return carry

    lax.fori_loop(0, n_chunks, readout, 0, unroll=n_chunks // 2)


def _hgrn(z, sums, masks, gn, *, layer, ctx_len, col_q, col_v, col_ff, col_fb, col_ga):
    bsz, t, _ = z.shape
    c = HGRN_CHUNK
    dk = A_DK
    assert t // c >= 2
    kern = functools.partial(_hgrn_kernel, n_ctx_chunks=ctx_len // c, n_chunks=t // c,
                             scale=A_DK ** -0.5)

    def col(start):
        return pl.BlockSpec((None, t, dk), lambda b, h: (b, 0, start // dk + h))

    return pl.pallas_call(
        kern,
        grid=(bsz, A_HEADS),
        in_specs=[
            col(col_q), col(col_v), col(col_ff), col(col_fb), col(col_ga),
            pl.BlockSpec(sums.shape, lambda b, h: (0, 0, 0)),
            pl.BlockSpec(masks.shape, lambda b, h: (0, 0, 0, 0)),
            pl.BlockSpec((None, 1, A_DV), lambda b, h: (layer, 0, 0)),
        ],
        out_specs=pl.BlockSpec((None, t, A_DV), lambda b, h: (b, 0, h)),
        out_shape=jax.ShapeDtypeStruct((bsz, t, A_HEADS * A_DV), BF16),
        scratch_shapes=[
            pltpu.VMEM((t, A_DV), F32), pltpu.VMEM((t, A_DV), F32),
            pltpu.VMEM((2, 2) + sums.shape[1:], F32),
            pltpu.VMEM((2, 2, c, c), BF16),
            pltpu.VMEM((2, 2, c, dk), BF16),
            pltpu.VMEM((2, 2, c, dk), BF16),
            pltpu.VMEM((2, 2, 1, dk), F32),
        ],
        compiler_params=_params("arbitrary", "arbitrary"),
        name="hgrn",
    )(z, z, z, z, z, sums, masks, gn)


def _mix_kernel(x_ref, ya_ref, *refs, tm, ctx_len, n_piece):
    m_refs = refs[:3 * n_piece]
    (u_ref, v_ref, hc_ref, gb_ref, gc_ref, mx_ref, mc_ref, lng_ref, lnb_ref, ws_ref, bs_ref,
     cw_ref, wpa_ref, wpb_ref, wpc_ref, wo_ref, o_ref, yb_sc, mg_sc) = refs[3 * n_piece:]
    i = pl.program_id(1)
    tile_is_ctx = i * tm < ctx_len

    vg = _gelu_tanh(v_ref[...].astype(F32))
    mu = jnp.mean(vg, axis=-1, keepdims=True)
    dv = vg - mu
    var = jnp.mean(dv * dv, axis=-1, keepdims=True)
    vn = (dv * lax.rsqrt(var + EPS) * lng_ref[...] + lnb_ref[...]).astype(BF16)
    for n in range(tm // B_CHUNK):
        rs = slice(n * B_CHUNK, (n + 1) * B_CHUNK)
        for g in range(B_GROUPS):
            cs = slice(g * B_GROUP_CH, (g + 1) * B_GROUP_CH)
            mixed = jnp.dot(ws_ref[g], vn[rs, cs], preferred_element_type=F32) + bs_ref[g]
            yb_sc[rs, cs] = (_gelu_tanh(u_ref[rs, cs].astype(F32)) * mixed).astype(BF16)

    period = jnp.where(tile_is_ctx, ctx_len, GRID_W)
    pos = (i * tm + lax.broadcasted_iota(jnp.int32, (tm, 1), 0)) & (period - 1)
    tt = gc_ref[...].astype(F32) * hc_ref[...].astype(F32)
    prev = jnp.where(pos == 0, 0.0, pltpu.roll(tt, 1, axis=0))
    nxt = jnp.where(pos == period - 1, 0.0, pltpu.roll(tt, tm - 1, axis=0))
    conv = cw_ref[0:1, :] * prev + cw_ref[1:2, :] * tt + cw_ref[2:3, :] * nxt
    yc = (gb_ref[...].astype(F32) * conv).astype(BF16)

    d = x_ref.shape[-1]
    pw = d // n_piece
    branches = ((ya_ref[...], wpa_ref), (yb_sc[...], wpb_ref), (yc, wpc_ref))
    for k in range(n_piece):
        cs = slice(k * pw, (k + 1) * pw)
        merged2 = None
        for g, (y, w_ref) in enumerate(branches):
            p = jnp.dot(y, w_ref[:, cs], preferred_element_type=F32)
            p = p + jnp.tanh((0.5 * m_refs[g * n_piece + k][...]).astype(F32)) * p
            merged2 = p if merged2 is None else merged2 + p
        mg_sc[:, cs] = merged2.astype(BF16)
    mg = mg_sc[...]
    for k in range(n_piece):
        cs = slice(k * pw, (k + 1) * pw)
        out2 = jnp.dot(mg, wo_ref[:, cs], preferred_element_type=F32)
        gate = jnp.where(tile_is_ctx, mc_ref[5:6, cs], mx_ref[5:6, cs])
        o_ref[:, cs] = x_ref[:, cs] + (0.5 * gate) * out2


def _mix(xa, ya, z, mods, ln_g, ln_b, ws, bs, cw, wpa, wpb, wpc, wo, *, layer, ctx_len, col_m,
         col_s):
    bsz, t, d = xa.shape
    tm = MIX_TM
    assert ctx_len % tm == 0 and tm % B_CHUNK == 0 and tm % GRID_W == 0
    assert ctx_len & (ctx_len - 1) == 0 and GRID_W & (GRID_W - 1) == 0
    bw = B_GROUPS * B_GROUP_CH
    ctx_row = bsz
    pw = min(MIX_PIECE, d)
    n_piece = d // pw
    kern = functools.partial(_mix_kernel, tm=tm, ctx_len=ctx_len, n_piece=n_piece)

    def zcol(start, width):
        assert start % width == 0
        return pl.BlockSpec((None, tm, width), lambda b, i: (b, i, start // width))

    def per_layer(arr, index=layer):
        zeros = (0,) * (arr.ndim - 1)
        return pl.BlockSpec((None,) + arr.shape[1:], lambda b, i: (index,) + zeros,
                            pipeline_mode=pl.Buffered(1))

    return pl.pallas_call(
        kern,
        grid=(bsz, t // tm),
        in_specs=[
            pl.BlockSpec((None, tm, d), lambda b, i: (b, i, 0)),
            pl.BlockSpec((None, tm, ya.shape[2]), lambda b, i: (b, i, 0)),
            *[zcol(col_m + g * d + k * pw, pw) for g in range(3) for k in range(n_piece)],
            zcol(col_s, bw), zcol(col_s + bw, bw), zcol(col_s + 2 * bw, bw),
            zcol(col_s + 3 * bw, bw), zcol(col_s + 4 * bw, bw),
            pl.BlockSpec((None, None, N_MOD, d), lambda b, i: (layer, b, 0, 0)),
            pl.BlockSpec((None, None, N_MOD, d), lambda b, i: (layer, ctx_row, 0, 0)),
            per_layer(ln_g), per_layer(ln_b), per_layer(ws), per_layer(bs), per_layer(cw),
            per_layer(wpa, 0), per_layer(wpb, 0), per_layer(wpc, 0), per_layer(wo, 0),
        ],
        out_specs=pl.BlockSpec((None, tm, d), lambda b, i: (b, i, 0)),
        out_shape=jax.ShapeDtypeStruct(xa.shape, F32),
        scratch_shapes=[pltpu.VMEM((tm, bw), BF16), pltpu.VMEM((tm, d), BF16)],
        compiler_params=_params("arbitrary", "arbitrary"),
        name="mix",
    )(xa, ya, *([z] * (3 * n_piece + 5)), mods, mods, ln_g, ln_b, ws, bs, cw, wpa, wpb, wpc, wo)


def _in_columns():
    wk = A_HEADS * A_DK
    wv = A_HEADS * A_DV
    bw = B_GROUPS * B_GROUP_CH
    s = 3 * wk + 2 * wv
    assert C_WIDTH == bw
    return dict(q=0, v=wk, ff=wk + wv, fb=2 * wk + wv, ga=3 * wk + wv, s=s, m=s + 5 * bw)


def kernel(x, c, ctx, c_ctx, w_mod, b_mod, norm_g, final_norm_g, ffn_w_gate, ffn_w_up,
           ffn_w_down, w_in, hgrn_lb_logits, hgrn_out_norm_g, gmlp_ln_g, gmlp_ln_b,
           gmlp_w_s, gmlp_b_s, conv_w, w_proj_a, w_proj_b, w_proj_c, w_out):
    bsz, _, d = x.shape
    depth = w_mod.shape[0]
    ctx_len = ctx.shape[1]
    assert ctx_len % ROW_BLOCK == 0
    xa = jnp.concatenate([ctx, x], axis=1)

    mod_rows = 8 * (-(-(bsz + 1) // 8))
    cvec = jnp.zeros((mod_rows, d), F32).at[:bsz].set(c).at[bsz].set(c_ctx)
    mods = _modulation(cvec, w_mod, b_mod).reshape(depth, mod_rows, N_MOD, d)

    lb = jnp.cumsum(jax.nn.softmax(hgrn_lb_logits.astype(F32), axis=0), axis=0)
    lb = (lb - lb[:1]).reshape(depth, 1, -1)
    fg_c0 = 0.5 * (1.0 + lb)
    fg_c1 = 0.5 * (1.0 - lb)

    sums_np, masks_np = _hgrn_tables(HGRN_CHUNK)
    sums = jnp.asarray(sums_np, BF16)
    masks = jnp.asarray(masks_np, BF16)

    cols = _in_columns()
    ws = gmlp_w_s.astype(BF16)
    ffn_set = (ffn_w_gate, ffn_w_up, ffn_w_down)
    mix_set = (w_in, w_proj_a, w_proj_b, w_proj_c, w_out)
    ffn_w = [w[0:1].astype(BF16) for w in ffn_set]
    mix_w = [w[0:1].astype(BF16) for w in mix_set]
    norm_g4 = norm_g.reshape(depth, -1, 1, d)
    gn = hgrn_out_norm_g.reshape(depth, 1, -1)
    ln_g = gmlp_ln_g.reshape(depth, 1, -1)
    ln_b = gmlp_ln_b.reshape(depth, 1, -1)
    bs = gmlp_b_s[..., None]

    for l in range(depth):
        more = l + 1 < depth
        xa, next_ffn_w = _ffn(xa, mods, norm_g4, *ffn_w, layer=l, slot=0, which=0,
                              ctx_len=ctx_len,
                              casts=[(w, l + 1) for w in ffn_set] if more else ())
        w_in_l, wpa, wpb, wpc, wo = mix_w
        z = _inproj(xa, mods, norm_g4, w_in_l, fg_c0, fg_c1, layer=l, ctx_len=ctx_len,
                    lf_lo=cols["ff"])
        ya = _hgrn(z, sums, masks, gn, layer=l, ctx_len=ctx_len, col_q=cols["q"],
                   col_v=cols["v"], col_ff=cols["ff"], col_fb=cols["fb"], col_ga=cols["ga"])
        xa = _mix(xa, ya, z, mods, ln_g, ln_b, ws, bs, conv_w, wpa, wpb, wpc, wo, layer=l,
                  ctx_len=ctx_len, col_m=cols["m"], col_s=cols["s"])
        xa, next_mix_w = _ffn(xa, mods, norm_g4, *ffn_w, layer=l, slot=2, which=1,
                              ctx_len=ctx_len,
                              casts=[(w, l + 1) for w in mix_set] if more else (),
                              final_g=None if more else final_norm_g)
        if more:
            ffn_w, mix_w = next_ffn_w, next_mix_w
    return xa
```

```python
import functools
import math

import numpy as np
import jax
import jax.numpy as jnp
from jax import lax
from jax.experimental import pallas as pl
from jax.experimental.pallas import tpu as pltpu

F32 = jnp.float32
BF16 = jnp.bfloat16

EPS = 1e-6
N_MOD = 9
GRID_W = 64
A_HEADS = 8
A_DK = 128
A_DV = 128
B_GROUPS = 4
B_GROUP_CH = 128
B_CHUNK = 128
C_WIDTH = 512

VMEM_LIMIT_BYTES = 60 * 1024 * 1024
MXU_COLS = 256

HGRN_CHUNK = 128
HGRN_MXU_LEVELS = 3
ROW_BLOCK = 16
FFN_TM = 544
FFN_TF = 1408
FFN_PIECE = 512
CAST_ROWS = 16
IN_TM = 1088
IN_TN = 1536
MIX_TM = 256
MIX_PIECE = 512
MOD_TN = 1024
FINAL_TM = 256


def _params(*sem):
    return pltpu.CompilerParams(dimension_semantics=sem, vmem_limit_bytes=VMEM_LIMIT_BYTES)


def _sigmoid(a):
    return 0.5 * (jnp.tanh(0.5 * a) + 1.0)


def _gelu_tanh(a):
    return 0.5 * a * (1.0 + jnp.tanh(math.sqrt(2.0 / math.pi) * (a + 0.044715 * (a * a * a))))


def _row_blocks(tile, tm, ctx_len, fn, *, trips):
    def body(r, carry):
        start = pl.multiple_of(r * ROW_BLOCK, ROW_BLOCK)
        fn(pl.ds(start, ROW_BLOCK), tile * tm + start < ctx_len)
        return carry

    n = tm // ROW_BLOCK
    lax.fori_loop(0, n, body, 0, unroll=n // trips if n % trips == 0 else 1)


def _adaln_rows(x_ref, h_ref, vec_sc, g_ref, mx_ref, mc_ref, k_shift, tile, tm, ctx_len,
                zero_ref=None):
    d = x_ref.shape[-1]
    for s, m_ref in enumerate((mx_ref, mc_ref)):
        gain = g_ref[...] * (1.0 + m_ref[k_shift + 1:k_shift + 2, :])
        vec_sc[s, 0] = jnp.broadcast_to(gain, (ROW_BLOCK, d))
        vec_sc[s, 1] = jnp.broadcast_to(m_ref[k_shift:k_shift + 1, :], (ROW_BLOCK, d))

    def block(rows, is_ctx):
        s = is_ctx.astype(jnp.int32)
        x = x_ref[rows, :]
        ms = jnp.mean(x * x, axis=-1, keepdims=True)
        h = (x * lax.rsqrt(ms + EPS)) * vec_sc[s, 0] + vec_sc[s, 1]
        h_ref[rows, :] = h.astype(h_ref.dtype)
        if zero_ref is not None:
            zero_ref[rows, :] = jnp.zeros((ROW_BLOCK, d), zero_ref.dtype)

    _row_blocks(tile, tm, ctx_len, block, trips=2)


def _mod_kernel(c_ref, w_ref, b_ref, o_ref):
    c = c_ref[...]
    s = (c * _sigmoid(c)).astype(BF16)
    o_ref[...] = jnp.dot(s, w_ref[...].astype(BF16), preferred_element_type=F32) + b_ref[...]


def _modulation(cvec, w_mod, b_mod):
    depth, d, nw = w_mod.shape
    rows = cvec.shape[0]
    tn = MOD_TN
    return pl.pallas_call(
        _mod_kernel,
        grid=(depth, nw // tn),
        in_specs=[
            pl.BlockSpec((rows, d), lambda l, j: (0, 0)),
            pl.BlockSpec((None, d, tn), lambda l, j: (l, 0, j)),
            pl.BlockSpec((None, 1, tn), lambda l, j: (l, 0, j)),
        ],
        out_specs=pl.BlockSpec((None, rows, tn), lambda l, j: (l, 0, j)),
        out_shape=jax.ShapeDtypeStruct((depth, rows, nw), F32),
        compiler_params=_params("arbitrary", "arbitrary"),
        name="modulation",
    )(cvec, w_mod, b_mod.reshape(depth, 1, nw))


def _col_pieces(width, piece):
    return [slice(lo, min(lo + piece, width)) for lo in range(0, width, piece)]


def _ffn_up_kernel(x_ref, xn_ref, mx_ref, mxn_ref, mc_ref, g_ref, wg_ref, wu_ref, *rest,
                   slot, tm, ctx_len, n_cast, nt, nj):
    cast_src = rest[:n_cast]
    act_ref = rest[n_cast]
    cast_dst = rest[n_cast + 1:2 * n_cast + 1]
    h0_sc, h1_sc, vec_sc = rest[2 * n_cast + 1:]
    i, j = pl.program_id(1), pl.program_id(2)
    tile = pl.program_id(0) * nt + i
    tile_next = jnp.minimum(tile + 1, pl.num_programs(0) * nt - 1)
    i_next = tile_next % nt
    d = x_ref.shape[-1]
    k_shift = 3 * slot

    @pl.when(jnp.logical_and(tile == 0, j == 0))
    def _():
        _adaln_rows(x_ref, h0_sc, vec_sc, g_ref, mx_ref, mc_ref, k_shift, i, tm, ctx_len)

    def step(h_ref, hn_ref):
        h = h_ref[...]
        tf = act_ref.shape[-1]
        paired = tf - tf % MXU_COLS
        for cs in _col_pieces(paired, FFN_PIECE):
            a = jnp.dot(h, wg_ref[:, cs], preferred_element_type=F32)
            u = jnp.dot(h, wu_ref[:, cs], preferred_element_type=F32)
            act_ref[:, cs] = ((a * _sigmoid(a)) * u).astype(act_ref.dtype)
        if paired < tf:
            half = tf - paired
            w_tail = jnp.concatenate([wg_ref[:, paired:], wu_ref[:, paired:]], axis=1)
            au = jnp.dot(h, w_tail, preferred_element_type=F32)
            a, u = au[:, :half], au[:, half:]
            act_ref[:, paired:] = ((a * _sigmoid(a)) * u).astype(act_ref.dtype)

        for s, m_ref in enumerate((mxn_ref, mc_ref)):
            gain = g_ref[...] * (1.0 + m_ref[k_shift + 1:k_shift + 2, :])
            vec_sc[s, 0] = jnp.broadcast_to(gain, (ROW_BLOCK, d))
            vec_sc[s, 1] = jnp.broadcast_to(m_ref[k_shift:k_shift + 1, :], (ROW_BLOCK, d))
        n_blocks = tm // ROW_BLOCK
        per_step = pl.cdiv(n_blocks, nj)
        for k in range(per_step):
            start = pl.multiple_of(jnp.minimum(j * per_step + k, n_blocks - 1) * ROW_BLOCK,
                                   ROW_BLOCK)
            rows = pl.ds(start, ROW_BLOCK)
            s = (i_next * tm + start < ctx_len).astype(jnp.int32)
            x = xn_ref[rows, :]
            ms = jnp.mean(x * x, axis=-1, keepdims=True)
            hn = (x * lax.rsqrt(ms + EPS)) * vec_sc[s, 0] + vec_sc[s, 1]
            hn_ref[rows, :] = hn.astype(hn_ref.dtype)

        for src, dst in zip(cast_src, cast_dst):
            dst[...] = src[...].astype(dst.dtype)

    pl.when(tile % 2 == 0)(functools.partial(step, h0_sc, h1_sc))
    pl.when(tile % 2 == 1)(functools.partial(step, h1_sc, h0_sc))


def _ffn_down_kernel(act_ref, x_ref, mx_ref, mc_ref, wd_ref, *rest, slot, tm, ctx_len, row0):
    o_ref = rest[-1]
    row = row0 + pl.program_id(1) * tm + lax.broadcasted_iota(jnp.int32, (tm, 1), 0)
    is_ctx = row < ctx_len
    act = act_ref[...]
    k_gate = 3 * slot + 2
    for cs in _col_pieces(o_ref.shape[-1], FFN_PIECE):
        y = jnp.dot(act, wd_ref[:, cs], preferred_element_type=F32)
        gate = jnp.where(is_ctx, mc_ref[k_gate:k_gate + 1, cs], mx_ref[k_gate:k_gate + 1, cs])
        o_ref[:, cs] = x_ref[:, cs] + (0.5 * gate) * y
    if len(rest) == 2:
        y = o_ref[...]
        ms = jnp.mean(y * y, axis=-1, keepdims=True)
        o_ref[...] = y * lax.rsqrt(ms + EPS) * rest[0][...]


def _ffn_steps(bsz, t, f):
    return bsz * (t // FFN_TM) * (f // FFN_TF)


def _ffn(xa, mods, norm_g, wg, wu, wd, *, layer, slot, which, ctx_len, casts=(), final_g=None):
    bsz, t, d = xa.shape
    f = wg.shape[-1]
    tm, tf = FFN_TM, FFN_TF
    nt, nj = t // tm, f // tf
    steps = _ffn_steps(bsz, t, f)
    ctx_row = bsz
    mods_x = pl.BlockSpec((None, None, N_MOD, d), lambda b, i, *_: (layer, b, 0, 0))
    mods_c = pl.BlockSpec((None, None, N_MOD, d), lambda b, i, *_: (layer, ctx_row, 0, 0))

    def cast_specs(src, lsrc):
        lead, rows = src.shape[1:-2], src.shape[-2]
        n_lead = math.prod(lead)
        rb = next(r for r in range(CAST_ROWS, rows + 1, CAST_ROWS)
                  if rows % r == 0 and n_lead * (rows // r) <= steps)
        per_lead = rows // rb
        n_blocks = n_lead * per_lead

        def index(layer_index):
            def index_map(b, i, j):
                blk = (((b * nt + i) * nj + j) * n_blocks) // steps
                where = (blk // per_lead, blk % per_lead) if lead else (blk,)
                return (layer_index,) + where + (0,)
            return index_map

        block = (None,) + (None,) * len(lead) + (rb, src.shape[-1])
        return (pl.BlockSpec(block, index(lsrc)), pl.BlockSpec(block, index(0)),
                jax.ShapeDtypeStruct((1,) + src.shape[1:], BF16))

    cast_in, cast_out, cast_shapes = zip(*[cast_specs(s, l) for s, l in casts]) if casts else (
        (), (), ())
    def next_tile(b, i):
        return jnp.minimum(b * nt + i + 1, bsz * nt - 1)

    act, *cast = pl.pallas_call(
        functools.partial(_ffn_up_kernel, slot=slot, tm=tm, ctx_len=ctx_len, n_cast=len(casts),
                          nt=nt, nj=nj),
        grid=(bsz, nt, nj),
        in_specs=[
            pl.BlockSpec((None, tm, d), lambda b, i, j: (b, i, 0)),
            pl.BlockSpec((None, tm, d),
                         lambda b, i, j: (next_tile(b, i) // nt, next_tile(b, i) % nt, 0)),
            mods_x,
            pl.BlockSpec((None, None, N_MOD, d),
                         lambda b, i, j: (layer, next_tile(b, i) // nt, 0, 0)),
            mods_c,
            pl.BlockSpec((None, None, 1, d), lambda b, i, j: (layer, slot, 0, 0)),
            pl.BlockSpec((None, None, d, tf), lambda b, i, j: (0, which, 0, j)),
            pl.BlockSpec((None, None, d, tf), lambda b, i, j: (0, which, 0, j)),
            *cast_in,
        ],
        out_specs=[pl.BlockSpec((None, tm, tf), lambda b, i, j: (b, i, j)), *cast_out],
        out_shape=[jax.ShapeDtypeStruct((bsz, t, f), BF16),
                   *cast_shapes],
        scratch_shapes=[pltpu.VMEM((tm, d), BF16), pltpu.VMEM((tm, d), BF16),
                        pltpu.VMEM((2, 2, ROW_BLOCK, d), F32)],
        compiler_params=_params("arbitrary", "arbitrary", "arbitrary"),
        name=f"ffn{slot}_up",
    )(xa, xa, mods, mods, mods, norm_g, wg, wu, *[src for src, _ in casts])
    if final_g is None:
        tmd, skip, rows_out, extra, extra_specs = tm, 0, t, (), []
    else:
        tmd = FINAL_TM
        assert ctx_len % tmd == 0 and (t - ctx_len) % tmd == 0
        skip, rows_out, extra = ctx_len // tmd, t - ctx_len, (final_g.reshape(1, d),)
        extra_specs = [pl.BlockSpec((1, d), lambda b, i: (0, 0))]
    out = pl.pallas_call(
        functools.partial(_ffn_down_kernel, slot=slot, tm=tmd, ctx_len=ctx_len, row0=skip * tmd),
        grid=(bsz, rows_out // tmd),
        in_specs=[
            pl.BlockSpec((None, tmd, f), lambda b, i: (b, i + skip, 0)),
            pl.BlockSpec((None, tmd, d), lambda b, i: (b, i + skip, 0)),
            mods_x, mods_c,
            pl.BlockSpec((None, None, f, d), lambda b, i: (0, which, 0, 0),
                         pipeline_mode=pl.Buffered(1)),
            *extra_specs,
        ],
        out_specs=pl.BlockSpec((None, tmd, d), lambda b, i: (b, i, 0)),
        out_shape=jax.ShapeDtypeStruct((bsz, rows_out, d), F32),
        compiler_params=_params("arbitrary", "arbitrary"),
        name=f"ffn{slot}_down",
    )(act, xa, mods, mods, wd, *extra)
    return out, cast


F32_TINY = 2.0 ** -126


def _log2_forget(z, c0, c1):
    return jnp.log2(jnp.maximum(c0 + c1 * jnp.tanh(0.5 * z), F32_TINY))


def _inproj_kernel(x_ref, mx_ref, mc_ref, g_ref, w_ref, c0_ref, c1_ref, o_ref, h_sc, vec_sc,
                   *, tm, tn, ctx_len, lf_lo, lf_hi):
    i = pl.program_id(1)
    j = pl.program_id(2)

    @pl.when(j == 0)
    def _():
        _adaln_rows(x_ref, h_sc, vec_sc, g_ref, mx_ref, mc_ref, 3, i, tm, ctx_len)

    z = jnp.dot(h_sc[...], w_ref[...], preferred_element_type=F32)
    o_ref[...] = z.astype(o_ref.dtype)

    for jj in range(lf_lo // tn, pl.cdiv(lf_hi, tn)):
        lo = max(lf_lo, jj * tn)
        hi = min(lf_hi, (jj + 1) * tn)
        cols = slice(lo - jj * tn, hi - jj * tn)
        lfc = slice(lo - lf_lo, hi - lf_lo)

        @pl.when(j == jj)
        def _(cols=cols, lfc=lfc):
            o_ref[:, cols] = _log2_forget(z[:, cols], c0_ref[:, lfc], c1_ref[:, lfc]).astype(
                o_ref.dtype)


def _inproj(xa, mods, norm_g, w, c0, c1, *, layer, ctx_len, lf_lo):
    bsz, t, d = xa.shape
    nw = w.shape[-1]
    tm, tn = IN_TM, IN_TN
    lf_w = c0.shape[-1]
    ctx_row = bsz
    kern = functools.partial(_inproj_kernel, tm=tm, tn=tn, ctx_len=ctx_len,
                             lf_lo=lf_lo, lf_hi=lf_lo + lf_w)
    return pl.pallas_call(
        kern,
        grid=(bsz, t // tm, nw // tn),
        in_specs=[
            pl.BlockSpec((None, tm, d), lambda b, i, j: (b, i, 0)),
            pl.BlockSpec((None, None, N_MOD, d), lambda b, i, j: (layer, b, 0, 0)),
            pl.BlockSpec((None, None, N_MOD, d), lambda b, i, j: (layer, ctx_row, 0, 0)),
            pl.BlockSpec((None, None, 1, d), lambda b, i, j: (layer, 1, 0, 0)),
            pl.BlockSpec((None, d, tn), lambda b, i, j: (0, 0, j)),
            pl.BlockSpec((None, 1, lf_w), lambda b, i, j: (layer, 0, 0)),
            pl.BlockSpec((None, 1, lf_w), lambda b, i, j: (layer, 0, 0)),
        ],
        out_specs=pl.BlockSpec((None, tm, tn), lambda b, i, j: (b, i, j)),
        out_shape=jax.ShapeDtypeStruct((bsz, t, nw), BF16),
        scratch_shapes=[pltpu.VMEM((tm, d), BF16), pltpu.VMEM((2, 2, ROW_BLOCK, d), F32)],
        compiler_params=_params("arbitrary", "arbitrary", "arbitrary"),
        name="inproj",
    )(xa, mods, mods, norm_g, w, c0, c1)


def _hgrn_tables(c):
    levels = int(math.log2(c))
    t = np.arange(c)[:, None]
    r = np.arange(c)[None, :]
    masks = np.zeros((2, levels + 1, c, c), np.float32)
    masks[:, levels] = np.eye(c)
    sums = np.zeros((2, 1 + HGRN_MXU_LEVELS, c, c), np.float32)
    sums[0, 0] = r <= t
    sums[1, 0] = r >= t
    for l in range(levels):
        h = 1 << l
        mid = (t // (2 * h)) * 2 * h + h - 1
        later = (t & h) != 0
        if l < HGRN_MXU_LEVELS:
            sums[0, 1 + l] = np.where(later, (r > mid) & (r <= t), (r > t) & (r <= mid))
            sums[1, 1 + l] = np.where(later, (r > mid) & (r < t), (r >= t) & (r <= mid))
        same = (t // (2 * h)) == (r // (2 * h))
        masks[0, l] = same & later & ((r & h) == 0)
        masks[1, l] = same & (~later) & ((r & h) != 0)
    return sums.reshape(2, (1 + HGRN_MXU_LEVELS) * c, c), masks


def _level_weight(l, d, g):
    c, dk = g.shape
    h = 1 << l
    assert h % 8 == 0
    mid = h - 1 if d == 0 else h
    g3 = g.reshape(c // (2 * h), 2 * h, dk)
    gm = g3[:, mid:mid + 1, :]
    ge, gl = g3[:, :h, :], g3[:, h:, :]
    early, late = (gm - ge, gl - gm) if d == 0 else (ge - gm, gm - gl)
    return jnp.exp2(jnp.concatenate([early, late], axis=1)).reshape(c, dk)


_NT = (((1,), (1,)), ((), ()))
_TN = (((0,), (0,)), ((), ()))


def _hgrn_kernel(q_ref, v_ref, lff_ref, lfb_ref, ga_ref, sums_ref, masks_ref, gn_ref, y_ref,
                 of_sc, ob_sc, e_sc, a_sc, qd_sc, kd_sc, g_sc, *, n_ctx_chunks, n_chunks, scale):
    c = HGRN_CHUNK
    levels = masks_ref.shape[1] - 1
    lf_refs = (lff_ref, lfb_ref)
    o_scs = (of_sc, ob_sc)

    def rows_of(d, p):
        if d == 0:
            ci = p
        else:
            ci = jnp.where(p < n_ctx_chunks, n_ctx_chunks - 1 - p, n_chunks - 1 + n_ctx_chunks - p)
        return pl.ds(pl.multiple_of(ci * c, c), c)

    def stage_sums(p, slot):
        for d in range(2):
            lf = lf_refs[d][rows_of(d, p), :]
            e_sc[d, slot] = jnp.dot(sums_ref[d], lf, preferred_element_type=F32)

    def stage_scores(p, slot):
        for d in range(2):
            rows = rows_of(d, p)
            qb = q_ref[rows, :]
            kb = (1.0 - jnp.exp2(lf_refs[d][rows, :].astype(F32))).astype(BF16)
            g = e_sc[d, slot, 0:c, :]
            total = (c - 1) if d == 0 else 0
            g_tot = g[total:total + 1, :]
            qd_sc[d, slot] = qb * jnp.exp2(g).astype(BF16)
            kd_sc[d, slot] = kb * jnp.exp2(g_tot - g).astype(BF16)
            g_sc[d, slot] = jnp.exp2(g_tot)
            kbt = kb.T
            a = jnp.dot(qb, kbt, preferred_element_type=F32).astype(BF16)
            a = a * masks_ref[d, levels]
            for l in range(levels):
                if l < HGRN_MXU_LEVELS:
                    w = jnp.exp2(e_sc[d, slot, (1 + l) * c:(2 + l) * c, :])
                else:
                    w = _level_weight(l, d, g)
                w = w.astype(BF16)
                p_l = jnp.dot(qb * w, kbt * w.T, preferred_element_type=F32)
                a = a + p_l.astype(BF16) * masks_ref[d, l]
            a_sc[d, slot] = a

    def stage_state(p, slot, states):
        new = []
        for d in range(2):
            rows = rows_of(d, p)
            v = v_ref[rows, :]
            st = states[d]
            o = lax.dot_general(qd_sc[d, slot], st.astype(BF16), _NT, preferred_element_type=F32)
            o_scs[d][rows, :] = o + jnp.dot(a_sc[d, slot], v, preferred_element_type=F32)
            new.append(st * g_sc[d, slot] + lax.dot_general(v, kd_sc[d, slot], _TN,
                                                             preferred_element_type=F32))
        return tuple(new)

    def step(j, par, states):
        stage_sums(j, par)
        states = stage_state(j - 2, par, states)
        stage_scores(j - 1, 1 - par)
        return states

    def body(m, states):
        j = 2 + 2 * m
        return step(j + 1, 1, step(j, 0, states))

    assert n_chunks % 2 == 0
    last = n_chunks - 1
    zero = jnp.zeros((A_DV, A_DK), F32)
    stage_sums(0, 0)
    stage_scores(0, 0)
    stage_sums(1, 1)
    states = lax.fori_loop(0, (n_chunks - 2) // 2, body, (zero, zero))
    states = stage_state(last - 1, (last - 1) % 2, states)
    stage_scores(last, last % 2)
    stage_state(last, last % 2, states)

    def readout(j, carry):
        rows = pl.ds(pl.multiple_of(j * c, c), c)
        o = (of_sc[rows, :] + ob_sc[rows, :]) * scale
        ms = jnp.mean(o * o, axis=-1, keepdims=True)
        ga = ga_ref[rows, :].astype(F32)
        y = (o * lax.rsqrt(ms + EPS) * gn_ref[...]) * (ga * _sigmoid(ga))
        y_ref[rows, :] = y.astype(y_ref.dtype)
        return carry

    lax.fori_loop(0, n_chunks, readout, 0, unroll=n_chunks // 2)


def _hgrn(z, sums, masks, gn, *, layer, ctx_len, col_q, col_v, col_ff, col_fb, col_ga):
    bsz, t, _ = z.shape
    c = HGRN_CHUNK
    dk = A_DK
    assert t // c >= 2
    kern = functools.partial(_hgrn_kernel, n_ctx_chunks=ctx_len // c, n_chunks=t // c,
                             scale=A_DK ** -0.5)

    def col(start):
        return pl.BlockSpec((None, t, dk), lambda b, h: (b, 0, start // dk + h))

    return pl.pallas_call(
        kern,
        grid=(bsz, A_HEADS),
        in_specs=[
            col(col_q), col(col_v), col(col_ff), col(col_fb), col(col_ga),
            pl.BlockSpec(sums.shape, lambda b, h: (0, 0, 0)),
            pl.BlockSpec(masks.shape, lambda b, h: (0, 0, 0, 0)),
            pl.BlockSpec((None, 1, A_DV), lambda b, h: (layer, 0, 0)),
        ],
        out_specs=pl.BlockSpec((None, t, A_DV), lambda b, h: (b, 0, h)),
        out_shape=jax.ShapeDtypeStruct((bsz, t, A_HEADS * A_DV), BF16),
        scratch_shapes=[
            pltpu.VMEM((t, A_DV), F32), pltpu.VMEM((t, A_DV), F32),
            pltpu.VMEM((2, 2) + sums.shape[1:], F32),
            pltpu.VMEM((2, 2, c, c), BF16),
            pltpu.VMEM((2, 2, c, dk), BF16),
            pltpu.VMEM((2, 2, c, dk), BF16),
            pltpu.VMEM((2, 2, 1, dk), F32),
        ],
        compiler_params=_params("arbitrary", "arbitrary"),
        name="hgrn",
    )(z, z, z, z, z, sums, masks, gn)


def _mix_kernel(x_ref, ya_ref, *refs, tm, ctx_len, n_piece):
    m_refs = refs[:3 * n_piece]
    (u_ref, v_ref, hc_ref, gb_ref, gc_ref, mx_ref, mc_ref, lng_ref, lnb_ref, ws_ref, bs_ref,
     cw_ref, wpa_ref, wpb_ref, wpc_ref, wo_ref, o_ref, yb_sc, mg_sc, acc_sc) = refs[3 * n_piece:]
    i = pl.program_id(1)
    tile_is_ctx = i * tm < ctx_len
    d = x_ref.shape[-1]
    pw = d // n_piece

    def gated2(g, k, y, w_ref):
        p = jnp.dot(y, w_ref[:, k * pw:(k + 1) * pw], preferred_element_type=F32)
        return p + jnp.tanh((0.5 * m_refs[g * n_piece + k][...]).astype(F32)) * p

    ya = ya_ref[...]
    for k in range(n_piece):
        acc_sc[:, k * pw:(k + 1) * pw] = gated2(0, k, ya, wpa_ref)

    vg = _gelu_tanh(v_ref[...].astype(F32))
    mu = jnp.mean(vg, axis=-1, keepdims=True)
    dv = vg - mu
    var = jnp.mean(dv * dv, axis=-1, keepdims=True)
    vn = (dv * lax.rsqrt(var + EPS) * lng_ref[...] + lnb_ref[...]).astype(BF16)
    for n in range(tm // B_CHUNK):
        rs = slice(n * B_CHUNK, (n + 1) * B_CHUNK)
        for g in range(B_GROUPS):
            cs = slice(g * B_GROUP_CH, (g + 1) * B_GROUP_CH)
            mixed = jnp.dot(ws_ref[g], vn[rs, cs], preferred_element_type=F32) + bs_ref[g]
            yb_sc[rs, cs] = (_gelu_tanh(u_ref[rs, cs].astype(F32)) * mixed).astype(BF16)

    period = jnp.where(tile_is_ctx, ctx_len, GRID_W)
    pos = (i * tm + lax.broadcasted_iota(jnp.int32, (tm, 1), 0)) & (period - 1)
    tt = gc_ref[...].astype(F32) * hc_ref[...].astype(F32)
    prev = jnp.where(pos == 0, 0.0, pltpu.roll(tt, 1, axis=0))
    nxt = jnp.where(pos == period - 1, 0.0, pltpu.roll(tt, tm - 1, axis=0))
    conv = cw_ref[0:1, :] * prev + cw_ref[1:2, :] * tt + cw_ref[2:3, :] * nxt
    yc = (gb_ref[...].astype(F32) * conv).astype(BF16)

    yb = yb_sc[...]
    for k in range(n_piece):
        cs = slice(k * pw, (k + 1) * pw)
        merged2 = acc_sc[:, cs] + gated2(1, k, yb, wpb_ref) + gated2(2, k, yc, wpc_ref)
        mg_sc[:, cs] = merged2.astype(BF16)
    mg = mg_sc[...]
    for k in range(n_piece):
        cs = slice(k * pw, (k + 1) * pw)
        out2 = jnp.dot(mg, wo_ref[:, cs], preferred_element_type=F32)
        gate = jnp.where(tile_is_ctx, mc_ref[5:6, cs], mx_ref[5:6, cs])
        o_ref[:, cs] = x_ref[:, cs] + (0.5 * gate) * out2


def _mix(xa, ya, z, mods, ln_g, ln_b, ws, bs, cw, wpa, wpb, wpc, wo, *, layer, ctx_len, col_m,
         col_s):
    bsz, t, d = xa.shape
    tm = MIX_TM
    assert ctx_len % tm == 0 and tm % B_CHUNK == 0 and tm % GRID_W == 0
    assert ctx_len & (ctx_len - 1) == 0 and GRID_W & (GRID_W - 1) == 0
    bw = B_GROUPS * B_GROUP_CH
    ctx_row = bsz
    pw = min(MIX_PIECE, d)
    n_piece = d // pw
    kern = functools.partial(_mix_kernel, tm=tm, ctx_len=ctx_len, n_piece=n_piece)

    def zcol(start, width):
        assert start % width == 0
        return pl.BlockSpec((None, tm, width), lambda b, i: (b, i, start // width))

    def per_layer(arr, index=layer):
        zeros = (0,) * (arr.ndim - 1)
        return pl.BlockSpec((None,) + arr.shape[1:], lambda b, i: (index,) + zeros,
                            pipeline_mode=pl.Buffered(1))

    return pl.pallas_call(
        kern,
        grid=(bsz, t // tm),
        in_specs=[
            pl.BlockSpec((None, tm, d), lambda b, i: (b, i, 0)),
            pl.BlockSpec((None, tm, ya.shape[2]), lambda b, i: (b, i, 0)),
            *[zcol(col_m + g * d + k * pw, pw) for g in range(3) for k in range(n_piece)],
            zcol(col_s, bw), zcol(col_s + bw, bw), zcol(col_s + 2 * bw, bw),
            zcol(col_s + 3 * bw, bw), zcol(col_s + 4 * bw, bw),
            pl.BlockSpec((None, None, N_MOD, d), lambda b, i: (layer, b, 0, 0)),
            pl.BlockSpec((None, None, N_MOD, d), lambda b, i: (layer, ctx_row, 0, 0)),
            per_layer(ln_g), per_layer(ln_b), per_layer(ws), per_layer(bs), per_layer(cw),
            per_layer(wpa, 0), per_layer(wpb, 0), per_layer(wpc, 0), per_layer(wo, 0),
        ],
        out_specs=pl.BlockSpec((None, tm, d), lambda b, i: (b, i, 0)),
        out_shape=jax.ShapeDtypeStruct(xa.shape, F32),
        scratch_shapes=[pltpu.VMEM((tm, bw), BF16), pltpu.VMEM((tm, d), BF16),
                        pltpu.VMEM((tm, d), F32)],
        compiler_params=_params("arbitrary", "arbitrary"),
        name="mix",
    )(xa, ya, *([z] * (3 * n_piece + 5)), mods, mods, ln_g, ln_b, ws, bs, cw, wpa, wpb, wpc, wo)


def _in_columns():
    wk = A_HEADS * A_DK
    wv = A_HEADS * A_DV
    bw = B_GROUPS * B_GROUP_CH
    s = 3 * wk + 2 * wv
    assert C_WIDTH == bw
    return dict(q=0, v=wk, ff=wk + wv, fb=2 * wk + wv, ga=3 * wk + wv, s=s, m=s + 5 * bw)


def kernel(x, c, ctx, c_ctx, w_mod, b_mod, norm_g, final_norm_g, ffn_w_gate, ffn_w_up,
           ffn_w_down, w_in, hgrn_lb_logits, hgrn_out_norm_g, gmlp_ln_g, gmlp_ln_b,
           gmlp_w_s, gmlp_b_s, conv_w, w_proj_a, w_proj_b, w_proj_c, w_out):
    bsz, _, d = x.shape
    depth = w_mod.shape[0]
    ctx_len = ctx.shape[1]
    assert ctx_len % ROW_BLOCK == 0
    xa = jnp.concatenate([ctx, x], axis=1)

    mod_rows = 8 * (-(-(bsz + 1) // 8))
    cvec = jnp.zeros((mod_rows, d), F32).at[:bsz].set(c).at[bsz].set(c_ctx)
    mods = _modulation(cvec, w_mod, b_mod).reshape(depth, mod_rows, N_MOD, d)

    lb = jnp.cumsum(jax.nn.softmax(hgrn_lb_logits.astype(F32), axis=0), axis=0)
    lb = (lb - lb[:1]).reshape(depth, 1, -1)
    fg_c0 = 0.5 * (1.0 + lb)
    fg_c1 = 0.5 * (1.0 - lb)

    sums_np, masks_np = _hgrn_tables(HGRN_CHUNK)
    sums = jnp.asarray(sums_np, BF16)
    masks = jnp.asarray(masks_np, BF16)

    cols = _in_columns()
    ws = gmlp_w_s.astype(BF16)
    ffn_set = (ffn_w_gate, ffn_w_up, ffn_w_down)
    mix_set = (w_in, w_proj_a, w_proj_b, w_proj_c, w_out)
    ffn_w = [w[0:1].astype(BF16) for w in ffn_set]
    mix_w = [w[0:1].astype(BF16) for w in mix_set]
    norm_g4 = norm_g.reshape(depth, -1, 1, d)
    gn = hgrn_out_norm_g.reshape(depth, 1, -1)
    ln_g = gmlp_ln_g.reshape(depth, 1, -1)
    ln_b = gmlp_ln_b.reshape(depth, 1, -1)
    bs = gmlp_b_s[..., None]

    for l in range(depth):
        more = l + 1 < depth
        xa, next_ffn_w = _ffn(xa, mods, norm_g4, *ffn_w, layer=l, slot=0, which=0,
                              ctx_len=ctx_len,
                              casts=[(w, l + 1) for w in ffn_set] if more else ())
        w_in_l, wpa, wpb, wpc, wo = mix_w
        z = _inproj(xa, mods, norm_g4, w_in_l, fg_c0, fg_c1, layer=l, ctx_len=ctx_len,
                    lf_lo=cols["ff"])
        ya = _hgrn(z, sums, masks, gn, layer=l, ctx_len=ctx_len, col_q=cols["q"],
                   col_v=cols["v"], col_ff=cols["ff"], col_fb=cols["fb"], col_ga=cols["ga"])
        xa = _mix(xa, ya, z, mods, ln_g, ln_b, ws, bs, conv_w, wpa, wpb, wpc, wo, layer=l,
                  ctx_len=ctx_len, col_m=cols["m"], col_s=cols["s"])
        xa, next_mix_w = _ffn(xa, mods, norm_g4, *ffn_w, layer=l, slot=2, which=1,
                              ctx_len=ctx_len,
                              casts=[(w, l + 1) for w in mix_set] if more else (),
                              final_g=None if more else final_norm_g)
        if more:
            ffn_w, mix_w = next_ffn_w, next_mix_w
    return xa
```

```python
import functools
import math

import numpy as np
import jax
import jax.numpy as jnp
from jax import lax
from jax.experimental import pallas as pl
from jax.experimental.pallas import tpu as pltpu

F32 = jnp.float32
BF16 = jnp.bfloat16

EPS = 1e-6
N_MOD = 9
GRID_W = 64
A_HEADS = 8
A_DK = 128
A_DV = 128
B_GROUPS = 4
B_GROUP_CH = 128
B_CHUNK = 128
C_WIDTH = 512

VMEM_LIMIT_BYTES = 60 * 1024 * 1024
MXU_COLS = 256

HGRN_CHUNK = 128
HGRN_MXU_LEVELS = 3
ROW_BLOCK = 16
FFN_TM = 544
FFN_TF = 1408
FFN_PIECE = 512
CAST_ROWS = 16
IN_TM = 1088
IN_TN = 1536
MIX_TM = 256
MIX_PIECE = 512
MOD_TN = 1024
FINAL_TM = 256


def _params(*sem):
    return pltpu.CompilerParams(dimension_semantics=sem, vmem_limit_bytes=VMEM_LIMIT_BYTES)


def _sigmoid(a):
    return 0.5 * (jnp.tanh(0.5 * a) + 1.0)


def _gelu_tanh(a):
    return 0.5 * a * (1.0 + jnp.tanh(math.sqrt(2.0 / math.pi) * (a + 0.044715 * (a * a * a))))


def _row_blocks(tile, tm, ctx_len, fn, *, trips):
    def body(r, carry):
        start = pl.multiple_of(r * ROW_BLOCK, ROW_BLOCK)
        fn(pl.ds(start, ROW_BLOCK), tile * tm + start < ctx_len)
        return carry

    n = tm // ROW_BLOCK
    lax.fori_loop(0, n, body, 0, unroll=n // trips if n % trips == 0 else 1)


def _adaln_rows(x_ref, h_ref, vec_sc, g_ref, mx_ref, mc_ref, k_shift, tile, tm, ctx_len):
    d = x_ref.shape[-1]
    for s, m_ref in enumerate((mx_ref, mc_ref)):
        gain = g_ref[...] * (1.0 + m_ref[k_shift + 1:k_shift + 2, :])
        vec_sc[s, 0] = jnp.broadcast_to(gain, (ROW_BLOCK, d))
        vec_sc[s, 1] = jnp.broadcast_to(m_ref[k_shift:k_shift + 1, :], (ROW_BLOCK, d))

    def block(rows, is_ctx):
        s = is_ctx.astype(jnp.int32)
        x = x_ref[rows, :]
        ms = jnp.mean(x * x, axis=-1, keepdims=True)
        h = (x * lax.rsqrt(ms + EPS)) * vec_sc[s, 0] + vec_sc[s, 1]
        h_ref[rows, :] = h.astype(h_ref.dtype)

    _row_blocks(tile, tm, ctx_len, block, trips=2)


def _mod_kernel(c_ref, w_ref, b_ref, o_ref):
    c = c_ref[...]
    s = (c * _sigmoid(c)).astype(BF16)
    o_ref[...] = jnp.dot(s, w_ref[...].astype(BF16), preferred_element_type=F32) + b_ref[...]


def _modulation(cvec, w_mod, b_mod):
    depth, d, nw = w_mod.shape
    rows = cvec.shape[0]
    tn = MOD_TN
    return pl.pallas_call(
        _mod_kernel,
        grid=(depth, nw // tn),
        in_specs=[
            pl.BlockSpec((rows, d), lambda l, j: (0, 0)),
            pl.BlockSpec((None, d, tn), lambda l, j: (l, 0, j)),
            pl.BlockSpec((None, 1, tn), lambda l, j: (l, 0, j)),
        ],
        out_specs=pl.BlockSpec((None, rows, tn), lambda l, j: (l, 0, j)),
        out_shape=jax.ShapeDtypeStruct((depth, rows, nw), F32),
        compiler_params=_params("arbitrary", "arbitrary"),
        name="modulation",
    )(cvec, w_mod, b_mod.reshape(depth, 1, nw))


def _col_pieces(width, piece):
    return [slice(lo, min(lo + piece, width)) for lo in range(0, width, piece)]


def _ffn_up_kernel(x_ref, mx_ref, mc_ref, g_ref, wg_ref, wu_ref, *rest, slot, tm, ctx_len, n_cast):
    cast_src = rest[:n_cast]
    act_ref = rest[n_cast]
    cast_dst = rest[n_cast + 1:2 * n_cast + 1]
    h_sc, vec_sc = rest[2 * n_cast + 1:]
    i = pl.program_id(1)

    @pl.when(pl.program_id(2) == 0)
    def _():
        _adaln_rows(x_ref, h_sc, vec_sc, g_ref, mx_ref, mc_ref, 3 * slot, i, tm, ctx_len)

    h = h_sc[...]
    tf = act_ref.shape[-1]
    paired = tf - tf % MXU_COLS
    for cs in _col_pieces(paired, FFN_PIECE):
        a = jnp.dot(h, wg_ref[:, cs], preferred_element_type=F32)
        u = jnp.dot(h, wu_ref[:, cs], preferred_element_type=F32)
        act_ref[:, cs] = ((a * _sigmoid(a)) * u).astype(act_ref.dtype)
    if paired < tf:
        half = tf - paired
        w_tail = jnp.concatenate([wg_ref[:, paired:], wu_ref[:, paired:]], axis=1)
        au = jnp.dot(h, w_tail, preferred_element_type=F32)
        a, u = au[:, :half], au[:, half:]
        act_ref[:, paired:] = ((a * _sigmoid(a)) * u).astype(act_ref.dtype)
    for src, dst in zip(cast_src, cast_dst):
        dst[...] = src[...].astype(dst.dtype)


def _ffn_down_kernel(act_ref, x_ref, mx_ref, mc_ref, wd_ref, *rest, slot, tm, ctx_len, row0):
    o_ref = rest[-1]
    row = row0 + pl.program_id(1) * tm + lax.broadcasted_iota(jnp.int32, (tm, 1), 0)
    is_ctx = row < ctx_len
    act = act_ref[...]
    k_gate = 3 * slot + 2
    for cs in _col_pieces(o_ref.shape[-1], FFN_PIECE):
        y = jnp.dot(act, wd_ref[:, cs], preferred_element_type=F32)
        gate = jnp.where(is_ctx, mc_ref[k_gate:k_gate + 1, cs], mx_ref[k_gate:k_gate + 1, cs])
        o_ref[:, cs] = x_ref[:, cs] + (0.5 * gate) * y
    if len(rest) == 2:
        y = o_ref[...]
        ms = jnp.mean(y * y, axis=-1, keepdims=True)
        o_ref[...] = y * lax.rsqrt(ms + EPS) * rest[0][...]


def _ffn_steps(bsz, t, f):
    return bsz * (t // FFN_TM) * (f // FFN_TF)


def _ffn(xa, mods, norm_g, wg, wu, wd, *, layer, slot, which, ctx_len, casts=(), final_g=None):
    bsz, t, d = xa.shape
    f = wg.shape[-1]
    tm, tf = FFN_TM, FFN_TF
    nt, nj = t // tm, f // tf
    steps = _ffn_steps(bsz, t, f)
    ctx_row = bsz
    mods_x = pl.BlockSpec((None, None, N_MOD, d), lambda b, i, *_: (layer, b, 0, 0))
    mods_c = pl.BlockSpec((None, None, N_MOD, d), lambda b, i, *_: (layer, ctx_row, 0, 0))

    def cast_specs(src, lsrc):
        lead, rows = src.shape[1:-2], src.shape[-2]
        n_lead = math.prod(lead)
        rb = next(r for r in range(CAST_ROWS, rows + 1, CAST_ROWS)
                  if rows % r == 0 and n_lead * (rows // r) <= steps)
        per_lead = rows // rb
        n_blocks = n_lead * per_lead

        def index(layer_index):
            def index_map(b, i, j):
                blk = (((b * nt + i) * nj + j) * n_blocks) // steps
                where = (blk // per_lead, blk % per_lead) if lead else (blk,)
                return (layer_index,) + where + (0,)
            return index_map

        block = (None,) + (None,) * len(lead) + (rb, src.shape[-1])
        return (pl.BlockSpec(block, index(lsrc)), pl.BlockSpec(block, index(0)),
                jax.ShapeDtypeStruct((1,) + src.shape[1:], BF16))

    cast_in, cast_out, cast_shapes = zip(*[cast_specs(s, l) for s, l in casts]) if casts else (
        (), (), ())
    act, *cast = pl.pallas_call(
        functools.partial(_ffn_up_kernel, slot=slot, tm=tm, ctx_len=ctx_len, n_cast=len(casts)),
        grid=(bsz, nt, nj),
        in_specs=[
            pl.BlockSpec((None, tm, d), lambda b, i, j: (b, i, 0)),
            mods_x, mods_c,
            pl.BlockSpec((None, None, 1, d), lambda b, i, j: (layer, slot, 0, 0)),
            pl.BlockSpec((None, None, d, tf), lambda b, i, j: (0, which, 0, j)),
            pl.BlockSpec((None, None, d, tf), lambda b, i, j: (0, which, 0, j)),
            *cast_in,
        ],
        out_specs=[pl.BlockSpec((None, tm, tf), lambda b, i, j: (b, i, j)), *cast_out],
        out_shape=[jax.ShapeDtypeStruct((bsz, t, f), BF16),
                   *cast_shapes],
        scratch_shapes=[pltpu.VMEM((tm, d), BF16), pltpu.VMEM((2, 2, ROW_BLOCK, d), F32)],
        compiler_params=_params("arbitrary", "arbitrary", "arbitrary"),
        name=f"ffn{slot}_up",
    )(xa, mods, mods, norm_g, wg, wu, *[src for src, _ in casts])
    if final_g is None:
        tmd, skip, rows_out, extra, extra_specs = tm, 0, t, (), []
    else:
        tmd = FINAL_TM
        assert ctx_len % tmd == 0 and (t - ctx_len) % tmd == 0
        skip, rows_out, extra = ctx_len // tmd, t - ctx_len, (final_g.reshape(1, d),)
        extra_specs = [pl.BlockSpec((1, d), lambda b, i: (0, 0))]
    out = pl.pallas_call(
        functools.partial(_ffn_down_kernel, slot=slot, tm=tmd, ctx_len=ctx_len, row0=skip * tmd),
        grid=(bsz, rows_out // tmd),
        in_specs=[
            pl.BlockSpec((None, tmd, f), lambda b, i: (b, i + skip, 0)),
            pl.BlockSpec((None, tmd, d), lambda b, i: (b, i + skip, 0)),
            mods_x, mods_c,
            pl.BlockSpec((None, None, f, d), lambda b, i: (0, which, 0, 0),
                         pipeline_mode=pl.Buffered(1)),
            *extra_specs,
        ],
        out_specs=pl.BlockSpec((None, tmd, d), lambda b, i: (b, i, 0)),
        out_shape=jax.ShapeDtypeStruct((bsz, rows_out, d), F32),
        compiler_params=_params("arbitrary", "arbitrary"),
        name=f"ffn{slot}_down",
    )(act, xa, mods, mods, wd, *extra)
    return out, cast


F32_TINY = 2.0 ** -126


def _log2_forget(z, c0, c1):
    return jnp.log2(jnp.maximum(c0 + c1 * jnp.tanh(0.5 * z), F32_TINY))


def _inproj_kernel(x_ref, mx_ref, mc_ref, g_ref, w_ref, c0_ref, c1_ref, o_ref, h_sc, vec_sc,
                   *, tm, tn, ctx_len, lf_lo, lf_hi):
    i = pl.program_id(1)
    j = pl.program_id(2)

    @pl.when(j == 0)
    def _():
        _adaln_rows(x_ref, h_sc, vec_sc, g_ref, mx_ref, mc_ref, 3, i, tm, ctx_len)

    z = jnp.dot(h_sc[...], w_ref[...], preferred_element_type=F32)
    o_ref[...] = z.astype(o_ref.dtype)

    for jj in range(lf_lo // tn, pl.cdiv(lf_hi, tn)):
        lo = max(lf_lo, jj * tn)
        hi = min(lf_hi, (jj + 1) * tn)
        cols = slice(lo - jj * tn, hi - jj * tn)
        lfc = slice(lo - lf_lo, hi - lf_lo)

        @pl.when(j == jj)
        def _(cols=cols, lfc=lfc):
            o_ref[:, cols] = _log2_forget(z[:, cols], c0_ref[:, lfc], c1_ref[:, lfc]).astype(
                o_ref.dtype)


def _inproj(xa, mods, norm_g, w, c0, c1, *, layer, ctx_len, lf_lo):
    bsz, t, d = xa.shape
    nw = w.shape[-1]
    tm, tn = IN_TM, IN_TN
    lf_w = c0.shape[-1]
    ctx_row = bsz
    kern = functools.partial(_inproj_kernel, tm=tm, tn=tn, ctx_len=ctx_len,
                             lf_lo=lf_lo, lf_hi=lf_lo + lf_w)
    return pl.pallas_call(
        kern,
        grid=(bsz, t // tm, nw // tn),
        in_specs=[
            pl.BlockSpec((None, tm, d), lambda b, i, j: (b, i, 0)),
            pl.BlockSpec((None, None, N_MOD, d), lambda b, i, j: (layer, b, 0, 0)),
            pl.BlockSpec((None, None, N_MOD, d), lambda b, i, j: (layer, ctx_row, 0, 0)),
            pl.BlockSpec((None, None, 1, d), lambda b, i, j: (layer, 1, 0, 0)),
            pl.BlockSpec((None, d, tn), lambda b, i, j: (0, 0, j)),
            pl.BlockSpec((None, 1, lf_w), lambda b, i, j: (layer, 0, 0)),
            pl.BlockSpec((None, 1, lf_w), lambda b, i, j: (layer, 0, 0)),
        ],
        out_specs=pl.BlockSpec((None, tm, tn), lambda b, i, j: (b, i, j)),
        out_shape=jax.ShapeDtypeStruct((bsz, t, nw), BF16),
        scratch_shapes=[pltpu.VMEM((tm, d), BF16), pltpu.VMEM((2, 2, ROW_BLOCK, d), F32)],
        compiler_params=_params("arbitrary", "arbitrary", "arbitrary"),
        name="inproj",
    )(xa, mods, mods, norm_g, w, c0, c1)


def _hgrn_tables(c):
    levels = int(math.log2(c))
    t = np.arange(c)[:, None]
    r = np.arange(c)[None, :]
    masks = np.zeros((2, levels + 1, c, c), np.float32)
    masks[:, levels] = np.eye(c)
    sums = np.zeros((2, 1 + HGRN_MXU_LEVELS, c, c), np.float32)
    sums[0, 0] = r <= t
    sums[1, 0] = r >= t
    for l in range(levels):
        h = 1 << l
        mid = (t // (2 * h)) * 2 * h + h - 1
        later = (t & h) != 0
        if l < HGRN_MXU_LEVELS:
            sums[0, 1 + l] = np.where(later, (r > mid) & (r <= t), (r > t) & (r <= mid))
            sums[1, 1 + l] = np.where(later, (r > mid) & (r < t), (r >= t) & (r <= mid))
        same = (t // (2 * h)) == (r // (2 * h))
        masks[0, l] = same & later & ((r & h) == 0)
        masks[1, l] = same & (~later) & ((r & h) != 0)
    return sums.reshape(2, (1 + HGRN_MXU_LEVELS) * c, c), masks


def _level_weight(l, d, g):
    c, dk = g.shape
    h = 1 << l
    assert h % 8 == 0
    mid = h - 1 if d == 0 else h
    g3 = g.reshape(c // (2 * h), 2 * h, dk)
    gm = g3[:, mid:mid + 1, :]
    ge, gl = g3[:, :h, :], g3[:, h:, :]
    early, late = (gm - ge, gl - gm) if d == 0 else (ge - gm, gm - gl)
    return jnp.exp2(jnp.concatenate([early, late], axis=1)).reshape(c, dk)


_NT = (((1,), (1,)), ((), ()))
_TN = (((0,), (0,)), ((), ()))


def _hgrn_kernel(q_ref, v_ref, lff_ref, lfb_ref, ga_ref, sums_ref, masks_ref, gn_ref, y_ref,
                 of_sc, ob_sc, e_sc, a_sc, qd_sc, kd_sc, g_sc, *, n_ctx_chunks, n_chunks, scale):
    c = HGRN_CHUNK
    levels = masks_ref.shape[1] - 1
    lf_refs = (lff_ref, lfb_ref)
    o_scs = (of_sc, ob_sc)

    def rows_of(d, p):
        if d == 0:
            ci = p
        else:
            ci = jnp.where(p < n_ctx_chunks, n_ctx_chunks - 1 - p, n_chunks - 1 + n_ctx_chunks - p)
        return pl.ds(pl.multiple_of(ci * c, c), c)

    def stage_sums(p, slot):
        for d in range(2):
            lf = lf_refs[d][rows_of(d, p), :]
            e_sc[d, slot] = jnp.dot(sums_ref[d], lf, preferred_element_type=F32)

    def stage_scores(p, slot):
        for d in range(2):
            rows = rows_of(d, p)
            qb = q_ref[rows, :]
            kb = (1.0 - jnp.exp2(lf_refs[d][rows, :].astype(F32))).astype(BF16)
            g = e_sc[d, slot, 0:c, :]
            total = (c - 1) if d == 0 else 0
            g_tot = g[total:total + 1, :]
            qd_sc[d, slot] = qb * jnp.exp2(g).astype(BF16)
            kd_sc[d, slot] = kb * jnp.exp2(g_tot - g).astype(BF16)
            g_sc[d, slot] = jnp.exp2(g_tot)
            kbt = kb.T
            a = jnp.dot(qb, kbt, preferred_element_type=F32).astype(BF16)
            a = a * masks_ref[d, levels]
            for l in range(levels):
                if l < HGRN_MXU_LEVELS:
                    w = jnp.exp2(e_sc[d, slot, (1 + l) * c:(2 + l) * c, :])
                else:
                    w = _level_weight(l, d, g)
                w = w.astype(BF16)
                p_l = jnp.dot(qb * w, kbt * w.T, preferred_element_type=F32)
                a = a + p_l.astype(BF16) * masks_ref[d, l]
            a_sc[d, slot] = a

    def stage_state(p, slot, states):
        new = []
        for d in range(2):
            rows = rows_of(d, p)
            v = v_ref[rows, :]
            st = states[d]
            o = lax.dot_general(qd_sc[d, slot], st.astype(BF16), _NT, preferred_element_type=F32)
            o_scs[d][rows, :] = o + jnp.dot(a_sc[d, slot], v, preferred_element_type=F32)
            new.append(st * g_sc[d, slot] + lax.dot_general(v, kd_sc[d, slot], _TN,
                                                             preferred_element_type=F32))
        return tuple(new)

    def step(j, par, states):
        stage_sums(j, par)
        states = stage_state(j - 2, par, states)
        stage_scores(j - 1, 1 - par)
        return states

    def body(m, states):
        j = 2 + 2 * m
        return step(j + 1, 1, step(j, 0, states))

    assert n_chunks % 2 == 0
    last = n_chunks - 1
    zero = jnp.zeros((A_DV, A_DK), F32)
    stage_sums(0, 0)
    stage_scores(0, 0)
    stage_sums(1, 1)
    states = lax.fori_loop(0, (n_chunks - 2) // 2, body, (zero, zero))
    states = stage_state(last - 1, (last - 1) % 2, states)
    stage_scores(last, last % 2)
    stage_state(last, last % 2, states)

    def readout(j, carry):
        rows = pl.ds(pl.multiple_of(j * c, c), c)
        o = (of_sc[rows, :] + ob_sc[rows, :]) * scale
        ms = jnp.mean(o * o, axis=-1, keepdims=True)
        ga = ga_ref[rows, :].astype(F32)
        y = (o * lax.rsqrt(ms + EPS) * gn_ref[...]) * (ga * _sigmoid(ga))
        y_ref[rows, :] = y.astype(y_ref.dtype)
        return carry

    lax.fori_loop(0, n_chunks, readout, 0, unroll=n_chunks // 2)


def _hgrn(z, sums, masks, gn, *, layer, ctx_len, col_q, col_v, col_ff, col_fb, col_ga):
    bsz, t, _ = z.shape
    c = HGRN_CHUNK
    dk = A_DK
    assert t // c >= 2
    kern = functools.partial(_hgrn_kernel, n_ctx_chunks=ctx_len // c, n_chunks=t // c,
                             scale=A_DK ** -0.5)

    def col(start):
        return pl.BlockSpec((None, t, dk), lambda b, h: (b, 0, start // dk + h))

    return pl.pallas_call(
        kern,
        grid=(bsz, A_HEADS),
        in_specs=[
            col(col_q), col(col_v), col(col_ff), col(col_fb), col(col_ga),
            pl.BlockSpec(sums.shape, lambda b, h: (0, 0, 0)),
            pl.BlockSpec(masks.shape, lambda b, h: (0, 0, 0, 0)),
            pl.BlockSpec((None, 1, A_DV), lambda b, h: (layer, 0, 0)),
        ],
        out_specs=pl.BlockSpec((None, t, A_DV), lambda b, h: (b, 0, h)),
        out_shape=jax.ShapeDtypeStruct((bsz, t, A_HEADS * A_DV), BF16),
        scratch_shapes=[
            pltpu.VMEM((t, A_DV), F32), pltpu.VMEM((t, A_DV), F32),
            pltpu.VMEM((2, 2) + sums.shape[1:], F32),
            pltpu.VMEM((2, 2, c, c), BF16),
            pltpu.VMEM((2, 2, c, dk), BF16),
            pltpu.VMEM((2, 2, c, dk), BF16),
            pltpu.VMEM((2, 2, 1, dk), F32),
        ],
        compiler_params=_params("arbitrary", "arbitrary"),
        name="hgrn",
    )(z, z, z, z, z, sums, masks, gn)


def _mix_kernel(x_ref, ya_ref, *refs, tm, ctx_len, n_piece):
    m_refs = refs[:3 * n_piece]
    (u_ref, v_ref, hc_ref, gb_ref, gc_ref, mx_ref, mc_ref, lng_ref, lnb_ref, ws_ref, bs_ref,
     cw_ref, wpa_ref, wpb_ref, wpc_ref, wo_ref, o_ref, yb_sc, mg_sc) = refs[3 * n_piece:]
    i = pl.program_id(1)
    tile_is_ctx = i * tm < ctx_len

    vg = _gelu_tanh(v_ref[...].astype(F32))
    mu = jnp.mean(vg, axis=-1, keepdims=True)
    dv = vg - mu
    var = jnp.mean(dv * dv, axis=-1, keepdims=True)
    vn = (dv * lax.rsqrt(var + EPS) * lng_ref[...] + lnb_ref[...]).astype(BF16)
    for n in range(tm // B_CHUNK):
        rs = slice(n * B_CHUNK, (n + 1) * B_CHUNK)
        for g in range(B_GROUPS):
            cs = slice(g * B_GROUP_CH, (g + 1) * B_GROUP_CH)
            mixed = jnp.dot(ws_ref[g], vn[rs, cs], preferred_element_type=F32) + bs_ref[g]
            yb_sc[rs, cs] = (_gelu_tanh(u_ref[rs, cs].astype(F32)) * mixed).astype(BF16)

    period = jnp.where(tile_is_ctx, ctx_len, GRID_W)
    pos = (i * tm + lax.broadcasted_iota(jnp.int32, (tm, 1), 0)) & (period - 1)
    tt = gc_ref[...].astype(F32) * hc_ref[...].astype(F32)
    prev = jnp.where(pos == 0, 0.0, pltpu.roll(tt, 1, axis=0))
    nxt = jnp.where(pos == period - 1, 0.0, pltpu.roll(tt, tm - 1, axis=0))
    conv = cw_ref[0:1, :] * prev + cw_ref[1:2, :] * tt + cw_ref[2:3, :] * nxt
    yc = (gb_ref[...].astype(F32) * conv).astype(BF16)

    d = x_ref.shape[-1]
    pw = d // n_piece
    branches = ((ya_ref[...], wpa_ref), (yb_sc[...], wpb_ref), (yc, wpc_ref))
    for k in range(n_piece):
        cs = slice(k * pw, (k + 1) * pw)
        merged2 = None
        for g, (y, w_ref) in enumerate(branches):
            p = jnp.dot(y, w_ref[:, cs], preferred_element_type=F32)
            p = p + jnp.tanh((0.5 * m_refs[g * n_piece + k][...]).astype(F32)) * p
            merged2 = p if merged2 is None else merged2 + p
        mg_sc[:, cs] = merged2.astype(BF16)
    mg = mg_sc[...]
    for k in range(n_piece):
        cs = slice(k * pw, (k + 1) * pw)
        out2 = jnp.dot(mg, wo_ref[:, cs], preferred_element_type=F32)
        gate = jnp.where(tile_is_ctx, mc_ref[5:6, cs], mx_ref[5:6, cs])
        o_ref[:, cs] = x_ref[:, cs] + (0.5 * gate) * out2


def _mix(xa, ya, z, mods, ln_g, ln_b, ws, bs, cw, wpa, wpb, wpc, wo, *, layer, ctx_len, col_m,
         col_s):
    bsz, t, d = xa.shape
    tm = MIX_TM
    assert ctx_len % tm == 0 and tm % B_CHUNK == 0 and tm % GRID_W == 0
    assert ctx_len & (ctx_len - 1) == 0 and GRID_W & (GRID_W - 1) == 0
    bw = B_GROUPS * B_GROUP_CH
    ctx_row = bsz
    pw = min(MIX_PIECE, d)
    n_piece = d // pw
    kern = functools.partial(_mix_kernel, tm=tm, ctx_len=ctx_len, n_piece=n_piece)

    def zcol(start, width):
        assert start % width == 0
        return pl.BlockSpec((None, tm, width), lambda b, i: (b, i, start // width))

    def per_layer(arr, index=layer):
        zeros = (0,) * (arr.ndim - 1)
        return pl.BlockSpec((None,) + arr.shape[1:], lambda b, i: (index,) + zeros,
                            pipeline_mode=pl.Buffered(1))

    return pl.pallas_call(
        kern,
        grid=(bsz, t // tm),
        in_specs=[
            pl.BlockSpec((None, tm, d), lambda b, i: (b, i, 0)),
            pl.BlockSpec((None, tm, ya.shape[2]), lambda b, i: (b, i, 0)),
            *[zcol(col_m + g * d + k * pw, pw) for g in range(3) for k in range(n_piece)],
            zcol(col_s, bw), zcol(col_s + bw, bw), zcol(col_s + 2 * bw, bw),
            zcol(col_s + 3 * bw, bw), zcol(col_s + 4 * bw, bw),
            pl.BlockSpec((None, None, N_MOD, d), lambda b, i: (layer, b, 0, 0)),
            pl.BlockSpec((None, None, N_MOD, d), lambda b, i: (layer, ctx_row, 0, 0)),
            per_layer(ln_g), per_layer(ln_b), per_layer(ws), per_layer(bs), per_layer(cw),
            per_layer(wpa, 0), per_layer(wpb, 0), per_layer(wpc, 0), per_layer(wo, 0),
        ],
        out_specs=pl.BlockSpec((None, tm, d), lambda b, i: (b, i, 0)),
        out_shape=jax.ShapeDtypeStruct(xa.shape, F32),
        scratch_shapes=[pltpu.VMEM((tm, bw), BF16), pltpu.VMEM((tm, d), BF16)],
        compiler_params=_params("arbitrary", "arbitrary"),
        name="mix",
    )(xa, ya, *([z] * (3 * n_piece + 5)), mods, mods, ln_g, ln_b, ws, bs, cw, wpa, wpb, wpc, wo)


def _in_columns():
    wk = A_HEADS * A_DK
    wv = A_HEADS * A_DV
    bw = B_GROUPS * B_GROUP_CH
    s = 3 * wk + 2 * wv
    assert C_WIDTH == bw
    return dict(q=0, v=wk, ff=wk + wv, fb=2 * wk + wv, ga=3 * wk + wv, s=s, m=s + 5 * bw)


def kernel(x, c, ctx, c_ctx, w_mod, b_mod, norm_g, final_norm_g, ffn_w_gate, ffn_w_up,
           ffn_w_down, w_in, hgrn_lb_logits, hgrn_out_norm_g, gmlp_ln_g, gmlp_ln_b,
           gmlp_w_s, gmlp_b_s, conv_w, w_proj_a, w_proj_b, w_proj_c, w_out):
    bsz, _, d = x.shape
    depth = w_mod.shape[0]
    ctx_len = ctx.shape[1]
    assert ctx_len % ROW_BLOCK == 0
    xa = jnp.concatenate([ctx, x], axis=1)

    mod_rows = 8 * (-(-(bsz + 1) // 8))
    cvec = jnp.zeros((mod_rows, d), F32).at[:bsz].set(c).at[bsz].set(c_ctx)
    mods = _modulation(cvec, w_mod, b_mod).reshape(depth, mod_rows, N_MOD, d)

    lb = jnp.cumsum(jax.nn.softmax(hgrn_lb_logits.astype(F32), axis=0), axis=0)
    lb = (lb - lb[:1]).reshape(depth, 1, -1)
    fg_c0 = 0.5 * (1.0 + lb)
    fg_c1 = 0.5 * (1.0 - lb)

    sums_np, masks_np = _hgrn_tables(HGRN_CHUNK)
    sums = jnp.asarray(sums_np, BF16)
    masks = jnp.asarray(masks_np, BF16)

    cols = _in_columns()
    ws = gmlp_w_s.astype(BF16)
    ffn_set = (ffn_w_gate, ffn_w_up, ffn_w_down)
    mix_set = (w_in, w_proj_a, w_proj_b, w_proj_c, w_out)
    ffn_w = [w[0:1].astype(BF16) for w in ffn_set]
    mix_w = [w[0:1].astype(BF16) for w in mix_set]
    norm_g4 = norm_g.reshape(depth, -1, 1, d)
    gn = hgrn_out_norm_g.reshape(depth, 1, -1)
    ln_g = gmlp_ln_g.reshape(depth, 1, -1)
    ln_b = gmlp_ln_b.reshape(depth, 1, -1)
    bs = gmlp_b_s[..., None]

    for l in range(depth):
        more = l + 1 < depth
        xa, next_ffn_w = _ffn(xa, mods, norm_g4, *ffn_w, layer=l, slot=0, which=0,
                              ctx_len=ctx_len,
                              casts=[(w, l + 1) for w in ffn_set] if more else ())
        w_in_l, wpa, wpb, wpc, wo = mix_w
        z = _inproj(xa, mods, norm_g4, w_in_l, fg_c0, fg_c1, layer=l, ctx_len=ctx_len,
                    lf_lo=cols["ff"])
        ya = _hgrn(z, sums, masks, gn, layer=l, ctx_len=ctx_len, col_q=cols["q"],
                   col_v=cols["v"], col_ff=cols["ff"], col_fb=cols["fb"], col_ga=cols["ga"])
        xa = _mix(xa, ya, z, mods, ln_g, ln_b, ws, bs, conv_w, wpa, wpb, wpc, wo, layer=l,
                  ctx_len=ctx_len, col_m=cols["m"], col_s=cols["s"])
        xa, next_mix_w = _ffn(xa, mods, norm_g4, *ffn_w, layer=l, slot=2, which=1,
                              ctx_len=ctx_len,
                              casts=[(w, l + 1) for w in mix_set] if more else (),
                              final_g=None if more else final_norm_g)
        if more:
            ffn_w, mix_w = next_ffn_w, next_mix_w
    return xa
```

```python
import functools
import math

import numpy as np
import jax
import jax.numpy as jnp
from jax import lax
from jax.experimental import pallas as pl
from jax.experimental.pallas import tpu as pltpu

F32 = jnp.float32
BF16 = jnp.bfloat16

EPS = 1e-6
N_MOD = 9
GRID_W = 64
A_HEADS = 8
A_DK = 128
A_DV = 128
B_GROUPS = 4
B_GROUP_CH = 128
B_CHUNK = 128
C_WIDTH = 512

VMEM_LIMIT_BYTES = 60 * 1024 * 1024
MXU_COLS = 256

HGRN_CHUNK = 128
HGRN_MXU_LEVELS = 3
ROW_BLOCK = 16
FFN_TM = 544
FFN_TF = 1408
FFN_PIECE = 512
CAST_ROWS = 16
IN_TM = 1088
IN_TN = 1536
MIX_TM = 256
MIX_PIECE = 512
MOD_TN = 1024
FINAL_TM = 256


def _params(*sem, fuse_inputs=None):
    return pltpu.CompilerParams(dimension_semantics=sem, vmem_limit_bytes=VMEM_LIMIT_BYTES,
                                allow_input_fusion=fuse_inputs)


def _sigmoid(a):
    return 0.5 * (jnp.tanh(0.5 * a) + 1.0)


def _gelu_tanh(a):
    return 0.5 * a * (1.0 + jnp.tanh(math.sqrt(2.0 / math.pi) * (a + 0.044715 * (a * a * a))))


def _row_blocks(tile, tm, ctx_len, fn, *, trips):
    def body(r, carry):
        start = pl.multiple_of(r * ROW_BLOCK, ROW_BLOCK)
        fn(pl.ds(start, ROW_BLOCK), tile * tm + start < ctx_len)
        return carry

    n = tm // ROW_BLOCK
    lax.fori_loop(0, n, body, 0, unroll=n // trips if n % trips == 0 else 1)


def _adaln_rows(x_ref, h_ref, vec_sc, g_ref, mx_ref, mc_ref, k_shift, tile, tm, ctx_len):
    d = x_ref.shape[-1]
    for s, m_ref in enumerate((mx_ref, mc_ref)):
        gain = g_ref[...] * (1.0 + m_ref[k_shift + 1:k_shift + 2, :])
        vec_sc[s, 0] = jnp.broadcast_to(gain, (ROW_BLOCK, d))
        vec_sc[s, 1] = jnp.broadcast_to(m_ref[k_shift:k_shift + 1, :], (ROW_BLOCK, d))

    def block(rows, is_ctx):
        s = is_ctx.astype(jnp.int32)
        x = x_ref[rows, :]
        ms = jnp.mean(x * x, axis=-1, keepdims=True)
        h = (x * lax.rsqrt(ms + EPS)) * vec_sc[s, 0] + vec_sc[s, 1]
        h_ref[rows, :] = h.astype(h_ref.dtype)

    _row_blocks(tile, tm, ctx_len, block, trips=2)


def _mod_kernel(c_ref, w_ref, b_ref, o_ref):
    c = c_ref[...]
    s = (c * _sigmoid(c)).astype(BF16)
    o_ref[...] = jnp.dot(s, w_ref[...].astype(BF16), preferred_element_type=F32) + b_ref[...]


def _modulation(cvec, w_mod, b_mod):
    depth, d, nw = w_mod.shape
    rows = cvec.shape[0]
    tn = MOD_TN
    return pl.pallas_call(
        _mod_kernel,
        grid=(depth, nw // tn),
        in_specs=[
            pl.BlockSpec((rows, d), lambda l, j: (0, 0)),
            pl.BlockSpec((None, d, tn), lambda l, j: (l, 0, j)),
            pl.BlockSpec((None, 1, tn), lambda l, j: (l, 0, j)),
        ],
        out_specs=pl.BlockSpec((None, rows, tn), lambda l, j: (l, 0, j)),
        out_shape=jax.ShapeDtypeStruct((depth, rows, nw), F32),
        compiler_params=_params("arbitrary", "arbitrary"),
        name="modulation",
    )(cvec, w_mod, b_mod.reshape(depth, 1, nw))


def _col_pieces(width, piece):
    return [slice(lo, min(lo + piece, width)) for lo in range(0, width, piece)]


def _ffn_up_kernel(x_ref, mx_ref, mc_ref, g_ref, wg_ref, wu_ref, *rest, slot, tm, ctx_len, n_cast):
    cast_src = rest[:n_cast]
    act_ref = rest[n_cast]
    cast_dst = rest[n_cast + 1:2 * n_cast + 1]
    h_sc, vec_sc = rest[2 * n_cast + 1:]
    i = pl.program_id(1)

    @pl.when(pl.program_id(2) == 0)
    def _():
        _adaln_rows(x_ref, h_sc, vec_sc, g_ref, mx_ref, mc_ref, 3 * slot, i, tm, ctx_len)

    h = h_sc[...]
    tf = act_ref.shape[-1]
    paired = tf - tf % MXU_COLS
    for cs in _col_pieces(paired, FFN_PIECE):
        a = jnp.dot(h, wg_ref[:, cs], preferred_element_type=F32)
        u = jnp.dot(h, wu_ref[:, cs], preferred_element_type=F32)
        act_ref[:, cs] = ((a * _sigmoid(a)) * u).astype(act_ref.dtype)
    if paired < tf:
        half = tf - paired
        w_tail = jnp.concatenate([wg_ref[:, paired:], wu_ref[:, paired:]], axis=1)
        au = jnp.dot(h, w_tail, preferred_element_type=F32)
        a, u = au[:, :half], au[:, half:]
        act_ref[:, paired:] = ((a * _sigmoid(a)) * u).astype(act_ref.dtype)
    for src, dst in zip(cast_src, cast_dst):
        dst[...] = src[...].astype(dst.dtype)


def _ffn_down_kernel(act_ref, x_ref, mx_ref, mc_ref, wd_ref, *rest, slot, tm, ctx_len, row0):
    o_ref = rest[-1]
    row = row0 + pl.program_id(1) * tm + lax.broadcasted_iota(jnp.int32, (tm, 1), 0)
    is_ctx = row < ctx_len
    act = act_ref[...]
    k_gate = 3 * slot + 2
    for cs in _col_pieces(o_ref.shape[-1], FFN_PIECE):
        y = jnp.dot(act, wd_ref[:, cs], preferred_element_type=F32)
        gate = jnp.where(is_ctx, mc_ref[k_gate:k_gate + 1, cs], mx_ref[k_gate:k_gate + 1, cs])
        o_ref[:, cs] = x_ref[:, cs] + (0.5 * gate) * y
    if len(rest) == 2:
        y = o_ref[...]
        ms = jnp.mean(y * y, axis=-1, keepdims=True)
        o_ref[...] = y * lax.rsqrt(ms + EPS) * rest[0][...]


def _ffn_steps(bsz, t, f):
    return bsz * (t // FFN_TM) * (f // FFN_TF)


def _ffn(xa, mods, norm_g, wg, wu, wd, *, layer, slot, which, ctx_len, casts=(), final_g=None,
         fuse_x=False):
    bsz, t, d = xa.shape
    f = wg.shape[-1]
    tm, tf = FFN_TM, FFN_TF
    nt, nj = t // tm, f // tf
    steps = _ffn_steps(bsz, t, f)
    ctx_row = bsz
    mods_x = pl.BlockSpec((None, None, N_MOD, d), lambda b, i, *_: (layer, b, 0, 0))
    mods_c = pl.BlockSpec((None, None, N_MOD, d), lambda b, i, *_: (layer, ctx_row, 0, 0))

    def cast_specs(src, lsrc):
        lead, rows = src.shape[1:-2], src.shape[-2]
        n_lead = math.prod(lead)
        rb = next(r for r in range(CAST_ROWS, rows + 1, CAST_ROWS)
                  if rows % r == 0 and n_lead * (rows // r) <= steps)
        per_lead = rows // rb
        n_blocks = n_lead * per_lead

        def index(layer_index):
            def index_map(b, i, j):
                blk = (((b * nt + i) * nj + j) * n_blocks) // steps
                where = (blk // per_lead, blk % per_lead) if lead else (blk,)
                return (layer_index,) + where + (0,)
            return index_map

        block = (None,) + (None,) * len(lead) + (rb, src.shape[-1])
        return (pl.BlockSpec(block, index(lsrc)), pl.BlockSpec(block, index(0)),
                jax.ShapeDtypeStruct((1,) + src.shape[1:], BF16))

    cast_in, cast_out, cast_shapes = zip(*[cast_specs(s, l) for s, l in casts]) if casts else (
        (), (), ())
    act, *cast = pl.pallas_call(
        functools.partial(_ffn_up_kernel, slot=slot, tm=tm, ctx_len=ctx_len, n_cast=len(casts)),
        grid=(bsz, nt, nj),
        in_specs=[
            pl.BlockSpec((None, tm, d), lambda b, i, j: (b, i, 0)),
            mods_x, mods_c,
            pl.BlockSpec((None, None, 1, d), lambda b, i, j: (layer, slot, 0, 0)),
            pl.BlockSpec((None, None, d, tf), lambda b, i, j: (0, which, 0, j)),
            pl.BlockSpec((None, None, d, tf), lambda b, i, j: (0, which, 0, j)),
            *cast_in,
        ],
        out_specs=[pl.BlockSpec((None, tm, tf), lambda b, i, j: (b, i, j)), *cast_out],
        out_shape=[jax.ShapeDtypeStruct((bsz, t, f), BF16),
                   *cast_shapes],
        scratch_shapes=[pltpu.VMEM((tm, d), BF16), pltpu.VMEM((2, 2, ROW_BLOCK, d), F32)],
        compiler_params=_params(
            "arbitrary", "arbitrary", "arbitrary",
            fuse_inputs=[True] + [False] * (5 + len(casts)) if fuse_x else None),
        name=f"ffn{slot}_up",
    )(xa, mods, mods, norm_g, wg, wu, *[src for src, _ in casts])
    if final_g is None:
        tmd, skip, rows_out, extra, extra_specs = tm, 0, t, (), []
    else:
        tmd = FINAL_TM
        assert ctx_len % tmd == 0 and (t - ctx_len) % tmd == 0
        skip, rows_out, extra = ctx_len // tmd, t - ctx_len, (final_g.reshape(1, d),)
        extra_specs = [pl.BlockSpec((1, d), lambda b, i: (0, 0))]
    out = pl.pallas_call(
        functools.partial(_ffn_down_kernel, slot=slot, tm=tmd, ctx_len=ctx_len, row0=skip * tmd),
        grid=(bsz, rows_out // tmd),
        in_specs=[
            pl.BlockSpec((None, tmd, f), lambda b, i: (b, i + skip, 0)),
            pl.BlockSpec((None, tmd, d), lambda b, i: (b, i + skip, 0)),
            mods_x, mods_c,
            pl.BlockSpec((None, None, f, d), lambda b, i: (0, which, 0, 0),
                         pipeline_mode=pl.Buffered(1)),
            *extra_specs,
        ],
        out_specs=pl.BlockSpec((None, tmd, d), lambda b, i: (b, i, 0)),
        out_shape=jax.ShapeDtypeStruct((bsz, rows_out, d), F32),
        compiler_params=_params(
            "arbitrary", "arbitrary",
            fuse_inputs=[False, True] + [False] * (3 + len(extra)) if fuse_x else None),
        name=f"ffn{slot}_down",
    )(act, xa, mods, mods, wd, *extra)
    return out, cast


F32_TINY = 2.0 ** -126


def _log2_forget(z, c0, c1):
    return jnp.log2(jnp.maximum(c0 + c1 * jnp.tanh(0.5 * z), F32_TINY))


def _inproj_kernel(x_ref, mx_ref, mc_ref, g_ref, w_ref, c0_ref, c1_ref, o_ref, h_sc, vec_sc,
                   *, tm, tn, ctx_len, lf_lo, lf_hi):
    i = pl.program_id(1)
    j = pl.program_id(2)

    @pl.when(j == 0)
    def _():
        _adaln_rows(x_ref, h_sc, vec_sc, g_ref, mx_ref, mc_ref, 3, i, tm, ctx_len)

    z = jnp.dot(h_sc[...], w_ref[...], preferred_element_type=F32)
    o_ref[...] = z.astype(o_ref.dtype)

    for jj in range(lf_lo // tn, pl.cdiv(lf_hi, tn)):
        lo = max(lf_lo, jj * tn)
        hi = min(lf_hi, (jj + 1) * tn)
        cols = slice(lo - jj * tn, hi - jj * tn)
        lfc = slice(lo - lf_lo, hi - lf_lo)

        @pl.when(j == jj)
        def _(cols=cols, lfc=lfc):
            o_ref[:, cols] = _log2_forget(z[:, cols], c0_ref[:, lfc], c1_ref[:, lfc]).astype(
                o_ref.dtype)


def _inproj(xa, mods, norm_g, w, c0, c1, *, layer, ctx_len, lf_lo):
    bsz, t, d = xa.shape
    nw = w.shape[-1]
    tm, tn = IN_TM, IN_TN
    lf_w = c0.shape[-1]
    ctx_row = bsz
    kern = functools.partial(_inproj_kernel, tm=tm, tn=tn, ctx_len=ctx_len,
                             lf_lo=lf_lo, lf_hi=lf_lo + lf_w)
    return pl.pallas_call(
        kern,
        grid=(bsz, t // tm, nw // tn),
        in_specs=[
            pl.BlockSpec((None, tm, d), lambda b, i, j: (b, i, 0)),
            pl.BlockSpec((None, None, N_MOD, d), lambda b, i, j: (layer, b, 0, 0)),
            pl.BlockSpec((None, None, N_MOD, d), lambda b, i, j: (layer, ctx_row, 0, 0)),
            pl.BlockSpec((None, None, 1, d), lambda b, i, j: (layer, 1, 0, 0)),
            pl.BlockSpec((None, d, tn), lambda b, i, j: (0, 0, j)),
            pl.BlockSpec((None, 1, lf_w), lambda b, i, j: (layer, 0, 0)),
            pl.BlockSpec((None, 1, lf_w), lambda b, i, j: (layer, 0, 0)),
        ],
        out_specs=pl.BlockSpec((None, tm, tn), lambda b, i, j: (b, i, j)),
        out_shape=jax.ShapeDtypeStruct((bsz, t, nw), BF16),
        scratch_shapes=[pltpu.VMEM((tm, d), BF16), pltpu.VMEM((2, 2, ROW_BLOCK, d), F32)],
        compiler_params=_params("arbitrary", "arbitrary", "arbitrary"),
        name="inproj",
    )(xa, mods, mods, norm_g, w, c0, c1)


def _hgrn_tables(c):
    levels = int(math.log2(c))
    t = np.arange(c)[:, None]
    r = np.arange(c)[None, :]
    masks = np.zeros((2, levels + 1, c, c), np.float32)
    masks[:, levels] = np.eye(c)
    sums = np.zeros((2, 1 + HGRN_MXU_LEVELS, c, c), np.float32)
    sums[0, 0] = r <= t
    sums[1, 0] = r >= t
    for l in range(levels):
        h = 1 << l
        mid = (t // (2 * h)) * 2 * h + h - 1
        later = (t & h) != 0
        if l < HGRN_MXU_LEVELS:
            sums[0, 1 + l] = np.where(later, (r > mid) & (r <= t), (r > t) & (r <= mid))
            sums[1, 1 + l] = np.where(later, (r > mid) & (r < t), (r >= t) & (r <= mid))
        same = (t // (2 * h)) == (r // (2 * h))
        masks[0, l] = same & later & ((r & h) == 0)
        masks[1, l] = same & (~later) & ((r & h) != 0)
    return sums.reshape(2, (1 + HGRN_MXU_LEVELS) * c, c), masks


def _level_weight(l, d, g):
    c, dk = g.shape
    h = 1 << l
    assert h % 8 == 0
    mid = h - 1 if d == 0 else h
    g3 = g.reshape(c // (2 * h), 2 * h, dk)
    gm = g3[:, mid:mid + 1, :]
    ge, gl = g3[:, :h, :], g3[:, h:, :]
    early, late = (gm - ge, gl - gm) if d == 0 else (ge - gm, gm - gl)
    return jnp.exp2(jnp.concatenate([early, late], axis=1)).reshape(c, dk)


_NT = (((1,), (1,)), ((), ()))
_TN = (((0,), (0,)), ((), ()))


def _hgrn_kernel(q_ref, v_ref, lff_ref, lfb_ref, ga_ref, sums_ref, masks_ref, gn_ref, y_ref,
                 of_sc, ob_sc, e_sc, a_sc, qd_sc, kd_sc, g_sc, *, n_ctx_chunks, n_chunks, scale):
    c = HGRN_CHUNK
    levels = masks_ref.shape[1] - 1
    lf_refs = (lff_ref, lfb_ref)
    o_scs = (of_sc, ob_sc)

    def rows_of(d, p):
        if d == 0:
            ci = p
        else:
            ci = jnp.where(p < n_ctx_chunks, n_ctx_chunks - 1 - p, n_chunks - 1 + n_ctx_chunks - p)
        return pl.ds(pl.multiple_of(ci * c, c), c)

    def stage_sums(p, slot):
        for d in range(2):
            lf = lf_refs[d][rows_of(d, p), :]
            e_sc[d, slot] = jnp.dot(sums_ref[d], lf, preferred_element_type=F32)

    def stage_scores(p, slot):
        for d in range(2):
            rows = rows_of(d, p)
            qb = q_ref[rows, :]
            kb = (1.0 - jnp.exp2(lf_refs[d][rows, :].astype(F32))).astype(BF16)
            g = e_sc[d, slot, 0:c, :]
            total = (c - 1) if d == 0 else 0
            g_tot = g[total:total + 1, :]
            qd_sc[d, slot] = qb * jnp.exp2(g).astype(BF16)
            kd_sc[d, slot] = kb * jnp.exp2(g_tot - g).astype(BF16)
            g_sc[d, slot] = jnp.exp2(g_tot)
            kbt = kb.T
            a = jnp.dot(qb, kbt, preferred_element_type=F32).astype(BF16)
            a = a * masks_ref[d, levels]
            for l in range(levels):
                if l < HGRN_MXU_LEVELS:
                    w = jnp.exp2(e_sc[d, slot, (1 + l) * c:(2 + l) * c, :])
                else:
                    w = _level_weight(l, d, g)
                w = w.astype(BF16)
                p_l = jnp.dot(qb * w, kbt * w.T, preferred_element_type=F32)
                a = a + p_l.astype(BF16) * masks_ref[d, l]
            a_sc[d, slot] = a

    def stage_state(p, slot, states):
        new = []
        for d in range(2):
            rows = rows_of(d, p)
            v = v_ref[rows, :]
            st = states[d]
            o = lax.dot_general(qd_sc[d, slot], st.astype(BF16), _NT, preferred_element_type=F32)
            o_scs[d][rows, :] = o + jnp.dot(a_sc[d, slot], v, preferred_element_type=F32)
            new.append(st * g_sc[d, slot] + lax.dot_general(v, kd_sc[d, slot], _TN,
                                                             preferred_element_type=F32))
        return tuple(new)

    def step(j, par, states):
        stage_sums(j, par)
        states = stage_state(j - 2, par, states)
        stage_scores(j - 1, 1 - par)
        return states

    def body(m, states):
        j = 2 + 2 * m
        return step(j + 1, 1, step(j, 0, states))

    assert n_chunks % 2 == 0
    last = n_chunks - 1
    zero = jnp.zeros((A_DV, A_DK), F32)
    stage_sums(0, 0)
    stage_scores(0, 0)
    stage_sums(1, 1)
    states = lax.fori_loop(0, (n_chunks - 2) // 2, body, (zero, zero))
    states = stage_state(last - 1, (last - 1) % 2, states)
    stage_scores(last, last % 2)
    stage_state(last, last % 2, states)

    def readout(j, carry):
        rows = pl.ds(pl.multiple_of(j * c, c), c)
        o = (of_sc[rows, :] + ob_sc[rows, :]) * scale
        ms = jnp.mean(o * o, axis=-1, keepdims=True)
        ga = ga_ref[rows, :].astype(F32)
        y = (o * lax.rsqrt(ms + EPS) * gn_ref[...]) * (ga * _sigmoid(ga))
        y_ref[rows, :] = y.astype(y_ref.dtype)
        return carry

    lax.fori_loop(0, n_chunks, readout, 0, unroll=n_chunks // 2)


def _hgrn(z, sums, masks, gn, *, layer, ctx_len, col_q, col_v, col_ff, col_fb, col_ga):
    bsz, t, _ = z.shape
    c = HGRN_CHUNK
    dk = A_DK
    assert t // c >= 2
    kern = functools.partial(_hgrn_kernel, n_ctx_chunks=ctx_len // c, n_chunks=t // c,
                             scale=A_DK ** -0.5)

    def col(start):
        return pl.BlockSpec((None, t, dk), lambda b, h: (b, 0, start // dk + h))

    return pl.pallas_call(
        kern,
        grid=(bsz, A_HEADS),
        in_specs=[
            col(col_q), col(col_v), col(col_ff), col(col_fb), col(col_ga),
            pl.BlockSpec(sums.shape, lambda b, h: (0, 0, 0)),
            pl.BlockSpec(masks.shape, lambda b, h: (0, 0, 0, 0)),
            pl.BlockSpec((None, 1, A_DV), lambda b, h: (layer, 0, 0)),
        ],
        out_specs=pl.BlockSpec((None, t, A_DV), lambda b, h: (b, 0, h)),
        out_shape=jax.ShapeDtypeStruct((bsz, t, A_HEADS * A_DV), BF16),
        scratch_shapes=[
            pltpu.VMEM((t, A_DV), F32), pltpu.VMEM((t, A_DV), F32),
            pltpu.VMEM((2, 2) + sums.shape[1:], F32),
            pltpu.VMEM((2, 2, c, c), BF16),
            pltpu.VMEM((2, 2, c, dk), BF16),
            pltpu.VMEM((2, 2, c, dk), BF16),
            pltpu.VMEM((2, 2, 1, dk), F32),
        ],
        compiler_params=_params("arbitrary", "arbitrary"),
        name="hgrn",
    )(z, z, z, z, z, sums, masks, gn)


def _mix_kernel(x_ref, ya_ref, *refs, tm, ctx_len, n_piece):
    m_refs = refs[:3 * n_piece]
    (u_ref, v_ref, hc_ref, gb_ref, gc_ref, mx_ref, mc_ref, lng_ref, lnb_ref, ws_ref, bs_ref,
     cw_ref, wpa_ref, wpb_ref, wpc_ref, wo_ref, o_ref, yb_sc, mg_sc) = refs[3 * n_piece:]
    i = pl.program_id(1)
    tile_is_ctx = i * tm < ctx_len

    vg = _gelu_tanh(v_ref[...].astype(F32))
    mu = jnp.mean(vg, axis=-1, keepdims=True)
    dv = vg - mu
    var = jnp.mean(dv * dv, axis=-1, keepdims=True)
    vn = (dv * lax.rsqrt(var + EPS) * lng_ref[...] + lnb_ref[...]).astype(BF16)
    for n in range(tm // B_CHUNK):
        rs = slice(n * B_CHUNK, (n + 1) * B_CHUNK)
        for g in range(B_GROUPS):
            cs = slice(g * B_GROUP_CH, (g + 1) * B_GROUP_CH)
            mixed = jnp.dot(ws_ref[g], vn[rs, cs], preferred_element_type=F32) + bs_ref[g]
            yb_sc[rs, cs] = (_gelu_tanh(u_ref[rs, cs].astype(F32)) * mixed).astype(BF16)

    period = jnp.where(tile_is_ctx, ctx_len, GRID_W)
    pos = (i * tm + lax.broadcasted_iota(jnp.int32, (tm, 1), 0)) & (period - 1)
    tt = gc_ref[...].astype(F32) * hc_ref[...].astype(F32)
    prev = jnp.where(pos == 0, 0.0, pltpu.roll(tt, 1, axis=0))
    nxt = jnp.where(pos == period - 1, 0.0, pltpu.roll(tt, tm - 1, axis=0))
    conv = cw_ref[0:1, :] * prev + cw_ref[1:2, :] * tt + cw_ref[2:3, :] * nxt
    yc = (gb_ref[...].astype(F32) * conv).astype(BF16)

    d = x_ref.shape[-1]
    pw = d // n_piece
    branches = ((ya_ref[...], wpa_ref), (yb_sc[...], wpb_ref), (yc, wpc_ref))
    for k in range(n_piece):
        cs = slice(k * pw, (k + 1) * pw)
        merged2 = None
        for g, (y, w_ref) in enumerate(branches):
            p = jnp.dot(y, w_ref[:, cs], preferred_element_type=F32)
            p = p + jnp.tanh((0.5 * m_refs[g * n_piece + k][...]).astype(F32)) * p
            merged2 = p if merged2 is None else merged2 + p
        mg_sc[:, cs] = merged2.astype(BF16)
    mg = mg_sc[...]
    for k in range(n_piece):
        cs = slice(k * pw, (k + 1) * pw)
        out2 = jnp.dot(mg, wo_ref[:, cs], preferred_element_type=F32)
        gate = jnp.where(tile_is_ctx, mc_ref[5:6, cs], mx_ref[5:6, cs])
        o_ref[:, cs] = x_ref[:, cs] + (0.5 * gate) * out2


def _mix(xa, ya, z, mods, ln_g, ln_b, ws, bs, cw, wpa, wpb, wpc, wo, *, layer, ctx_len, col_m,
         col_s):
    bsz, t, d = xa.shape
    tm = MIX_TM
    assert ctx_len % tm == 0 and tm % B_CHUNK == 0 and tm % GRID_W == 0
    assert ctx_len & (ctx_len - 1) == 0 and GRID_W & (GRID_W - 1) == 0
    bw = B_GROUPS * B_GROUP_CH
    ctx_row = bsz
    pw = min(MIX_PIECE, d)
    n_piece = d // pw
    kern = functools.partial(_mix_kernel, tm=tm, ctx_len=ctx_len, n_piece=n_piece)

    def zcol(start, width):
        assert start % width == 0
        return pl.BlockSpec((None, tm, width), lambda b, i: (b, i, start // width))

    def per_layer(arr, index=layer):
        zeros = (0,) * (arr.ndim - 1)
        return pl.BlockSpec((None,) + arr.shape[1:], lambda b, i: (index,) + zeros,
                            pipeline_mode=pl.Buffered(1))

    return pl.pallas_call(
        kern,
        grid=(bsz, t // tm),
        in_specs=[
            pl.BlockSpec((None, tm, d), lambda b, i: (b, i, 0)),
            pl.BlockSpec((None, tm, ya.shape[2]), lambda b, i: (b, i, 0)),
            *[zcol(col_m + g * d + k * pw, pw) for g in range(3) for k in range(n_piece)],
            zcol(col_s, bw), zcol(col_s + bw, bw), zcol(col_s + 2 * bw, bw),
            zcol(col_s + 3 * bw, bw), zcol(col_s + 4 * bw, bw),
            pl.BlockSpec((None, None, N_MOD, d), lambda b, i: (layer, b, 0, 0)),
            pl.BlockSpec((None, None, N_MOD, d), lambda b, i: (layer, ctx_row, 0, 0)),
            per_layer(ln_g), per_layer(ln_b), per_layer(ws), per_layer(bs), per_layer(cw),
            per_layer(wpa, 0), per_layer(wpb, 0), per_layer(wpc, 0), per_layer(wo, 0),
        ],
        out_specs=pl.BlockSpec((None, tm, d), lambda b, i: (b, i, 0)),
        out_shape=jax.ShapeDtypeStruct(xa.shape, F32),
        scratch_shapes=[pltpu.VMEM((tm, bw), BF16), pltpu.VMEM((tm, d), BF16)],
        compiler_params=_params("arbitrary", "arbitrary"),
        name="mix",
    )(xa, ya, *([z] * (3 * n_piece + 5)), mods, mods, ln_g, ln_b, ws, bs, cw, wpa, wpb, wpc, wo)


def _in_columns():
    wk = A_HEADS * A_DK
    wv = A_HEADS * A_DV
    bw = B_GROUPS * B_GROUP_CH
    s = 3 * wk + 2 * wv
    assert C_WIDTH == bw
    return dict(q=0, v=wk, ff=wk + wv, fb=2 * wk + wv, ga=3 * wk + wv, s=s, m=s + 5 * bw)


def kernel(x, c, ctx, c_ctx, w_mod, b_mod, norm_g, final_norm_g, ffn_w_gate, ffn_w_up,
           ffn_w_down, w_in, hgrn_lb_logits, hgrn_out_norm_g, gmlp_ln_g, gmlp_ln_b,
           gmlp_w_s, gmlp_b_s, conv_w, w_proj_a, w_proj_b, w_proj_c, w_out):
    bsz, _, d = x.shape
    depth = w_mod.shape[0]
    ctx_len = ctx.shape[1]
    assert ctx_len % ROW_BLOCK == 0
    xa = jnp.concatenate([ctx, x], axis=1)

    mod_rows = 8 * (-(-(bsz + 1) // 8))
    cvec = jnp.zeros((mod_rows, d), F32).at[:bsz].set(c).at[bsz].set(c_ctx)
    mods = _modulation(cvec, w_mod, b_mod).reshape(depth, mod_rows, N_MOD, d)

    lb = jnp.cumsum(jax.nn.softmax(hgrn_lb_logits.astype(F32), axis=0), axis=0)
    lb = (lb - lb[:1]).reshape(depth, 1, -1)
    fg_c0 = 0.5 * (1.0 + lb)
    fg_c1 = 0.5 * (1.0 - lb)

    sums_np, masks_np = _hgrn_tables(HGRN_CHUNK)
    sums = jnp.asarray(sums_np, BF16)
    masks = jnp.asarray(masks_np, BF16)

    cols = _in_columns()
    ws = gmlp_w_s.astype(BF16)
    ffn_set = (ffn_w_gate, ffn_w_up, ffn_w_down)
    mix_set = (w_in, w_proj_a, w_proj_b, w_proj_c, w_out)
    ffn_w = [w[0:1].astype(BF16) for w in ffn_set]
    mix_w = [w[0:1].astype(BF16) for w in mix_set]
    norm_g4 = norm_g.reshape(depth, -1, 1, d)
    gn = hgrn_out_norm_g.reshape(depth, 1, -1)
    ln_g = gmlp_ln_g.reshape(depth, 1, -1)
    ln_b = gmlp_ln_b.reshape(depth, 1, -1)
    bs = gmlp_b_s[..., None]

    for l in range(depth):
        more = l + 1 < depth
        xa, next_ffn_w = _ffn(xa, mods, norm_g4, *ffn_w, layer=l, slot=0, which=0,
                              ctx_len=ctx_len,
                              casts=[(w, l + 1) for w in ffn_set] if more else (),
                              fuse_x=l == 0)
        w_in_l, wpa, wpb, wpc, wo = mix_w
        z = _inproj(xa, mods, norm_g4, w_in_l, fg_c0, fg_c1, layer=l, ctx_len=ctx_len,
                    lf_lo=cols["ff"])
        ya = _hgrn(z, sums, masks, gn, layer=l, ctx_len=ctx_len, col_q=cols["q"],
                   col_v=cols["v"], col_ff=cols["ff"], col_fb=cols["fb"], col_ga=cols["ga"])
        xa = _mix(xa, ya, z, mods, ln_g, ln_b, ws, bs, conv_w, wpa, wpb, wpc, wo, layer=l,
                  ctx_len=ctx_len, col_m=cols["m"], col_s=cols["s"])
        xa, next_mix_w = _ffn(xa, mods, norm_g4, *ffn_w, layer=l, slot=2, which=1,
                              ctx_len=ctx_len,
                              casts=[(w, l + 1) for w in mix_set] if more else (),
                              final_g=None if more else final_norm_g)
        if more:
            ffn_w, mix_w = next_ffn_w, next_mix_w
    return xa
```
